```python
import numpy as np
import jax
import jax.numpy as jnp
from jax import lax

D_MODEL = 1024
BATCH = 16
SEQ = 256
DEPTH = 2
DEC_BATCH = 8
DEC_SEQ = 1024
PAST_LEN = 256

GRID_W = 64
EPS = 1e-6
HEADS_A = 4
DK_A = 128
DV_A = 128
CONV_K = 5
CHUNK_A = 64
HEADS_B = 4
DK_B = 64
DV_B = 128
CHUNK_B = 64
HEADS_C = 8
KV_HEADS_C = 2
HEAD_DIM_C = 64
GROUP_C = HEADS_C // KV_HEADS_C
WINDOW = 128
Q_BLOCK = 128
ROPE_BASE = 10000.0
N_EXPERTS = 16
D_EXPERT = 512
EC_CAPACITY = 2
ADA_CHUNKS = 6

IN_SIZES = (
    HEADS_A * DK_A, HEADS_A * DK_A, HEADS_A * DV_A, HEADS_A * DV_A,
    HEADS_A, HEADS_A, HEADS_A, HEADS_A,
    HEADS_B * DK_B, HEADS_B * DK_B, HEADS_B * DV_B, HEADS_B * DV_B,
    HEADS_B, HEADS_B, HEADS_B, HEADS_B,
    HEADS_C * HEAD_DIM_C, KV_HEADS_C * HEAD_DIM_C, KV_HEADS_C * HEAD_DIM_C,
    D_MODEL, D_MODEL, D_MODEL,
)
N_IN = sum(IN_SIZES)
CONV_CH_A = 2 * HEADS_A * DK_A + HEADS_A * DV_A

kernel_name = "hybrid_diffusion_delta_mlstm_swa_ec_step"


def _rms_norm(x, g):
    x32 = x.astype(jnp.float32)
    y = x32 * lax.rsqrt(jnp.mean(x32 * x32, axis=-1, keepdims=True) + EPS)
    return (y * g.astype(jnp.float32)).astype(x.dtype)


def _l2_norm(x):
    x32 = x.astype(jnp.float32)
    return x32 * lax.rsqrt(jnp.sum(x32 * x32, axis=-1, keepdims=True) + EPS)


def _flip(x):
    return jnp.flip(x, axis=1)


def _short_conv(x, w):
    return lax.conv_general_dilated(
        x, w[:, None, :].astype(x.dtype), window_strides=(1,),
        padding=((CONV_K // 2, CONV_K // 2),),
        dimension_numbers=("NWC", "WIO", "NWC"), feature_group_count=x.shape[-1])


def _to_chunks(x, size):
    b, t, h = x.shape[:3]
    rest = x.shape[3:]
    x = x.reshape((b, t // size, size, h) + rest)
    return x.transpose((1, 0, 3, 2) + tuple(range(4, x.ndim)))


def _from_chunks(x):
    n, b, h, s, d = x.shape
    return x.transpose(1, 0, 3, 2, 4).reshape(b, n * s, h, d)


def _gated_delta_scan(q, k, v, beta, g, s0):
    dv = v.shape[-1]
    q, k, v, beta, g = (_to_chunks(a, CHUNK_A) for a in (q, k, v, beta, g))
    gc = jnp.cumsum(g, axis=-1)
    idx = jnp.arange(CHUNK_A)
    incl = idx[:, None] >= idx[None, :]
    strict = idx[:, None] > idx[None, :]
    dec_incl = jnp.exp(jnp.where(incl, gc[..., :, None] - gc[..., None, :], -jnp.inf))
    dec_strict = jnp.where(strict, dec_incl, 0.0)
    kb = k * beta[..., None]
    lower = jnp.einsum("nbhtd,nbhsd->nbhts", kb, k) * dec_strict
    rhs = jnp.concatenate([v * beta[..., None], kb * jnp.exp(gc)[..., None]], axis=-1)
    sol = lax.linalg.triangular_solve(
        lower + jnp.eye(CHUNK_A, dtype=lower.dtype), rhs,
        left_side=True, lower=True, unit_diagonal=True)
    u, w = sol[..., :dv], sol[..., dv:]
    a_intra = jnp.einsum("nbhtd,nbhsd->nbhts", q, k) * dec_incl
    g_last = gc[..., -1]
    q_dec = q * jnp.exp(gc)[..., None]
    k_dec = k * jnp.exp(g_last[..., None] - gc)[..., None]

    def step(s, xs):
        qd, kd, uc, wc, ac, gl = xs
        v_new = uc - jnp.einsum("bhtk,bhkv->bhtv", wc, s)
        o = jnp.einsum("bhtk,bhkv->bhtv", qd, s) + jnp.einsum("bhts,bhsv->bhtv", ac, v_new)
        s = jnp.exp(gl)[..., None, None] * s + jnp.einsum("bhsk,bhsv->bhkv", kd, v_new)
        return s, o

    s_fin, o = lax.scan(step, s0, (q_dec, k_dec, u, w, a_intra, g_last))
    return _from_chunks(o), s_fin


def _mlstm_scan(q, k, v, ig, lf, c0, n0, m0):
    q, k, v, ig, lf = (_to_chunks(a, CHUNK_B) for a in (q, k, v, ig, lf))
    bcum = jnp.cumsum(lf, axis=-1)
    b_last = bcum[..., -1]
    idx = jnp.arange(CHUNK_B)
    causal = idx[:, None] >= idx[None, :]
    d_log = jnp.where(causal, bcum[..., :, None] - bcum[..., None, :] + ig[..., None, :], -jnp.inf)
    d_max = jnp.max(d_log, axis=-1)
    tok_log = b_last[..., None] - bcum + ig
    tok_max = jnp.max(tok_log, axis=-1)
    qk = jnp.einsum("nbhtd,nbhsd->nbhts", q, k)

    def step(carry, xs):
        c, n, m = carry
        qc, kc, vc, bc, blc, dc, dmc, tc, tmc, qkc = xs
        m_t = jnp.maximum(bc + m[..., None], dmc)
        w_inter = jnp.exp(bc + m[..., None] - m_t)
        p = jnp.exp(dc - m_t[..., None]) * qkc
        num = (w_inter[..., None] * jnp.einsum("bhtd,bhde->bhte", qc, c)
               + jnp.einsum("bhts,bhse->bhte", p, vc))
        den = w_inter * jnp.einsum("bhtd,bhd->bht", qc, n) + jnp.sum(p, axis=-1)
        h = num / jnp.maximum(jnp.abs(den), jnp.exp(-m_t))[..., None]
        m_new = jnp.maximum(blc + m, tmc)
        w_prev = jnp.exp(blc + m - m_new)
        w_tok = jnp.exp(tc - m_new[..., None])
        c = w_prev[..., None, None] * c + jnp.einsum("bhsd,bhse->bhde", kc * w_tok[..., None], vc)
        n = w_prev[..., None] * n + jnp.einsum("bhs,bhsd->bhd", w_tok, kc)
        return (c, n, m_new), h

    (c, n, m), h = lax.scan(step, (c0, n0, m0),
                            (q, k, v, bcum, b_last, d_log, d_max, tok_log, tok_max, qk))
    return _from_chunks(h), c, n, m


def _gated_delta_branch(qa, ka, va, za, beta_f, beta_b, a_f, a_b, conv_w, a_log, dt_bias, norm_g, state0):
    b, t = qa.shape[:2]
    f32 = jnp.float32
    qkv = jax.nn.silu(_short_conv(jnp.concatenate([qa, ka, va], axis=-1), conv_w))
    q, k, v = jnp.split(qkv, [HEADS_A * DK_A, 2 * HEADS_A * DK_A], axis=-1)
    q = _l2_norm(q.reshape(b, t, HEADS_A, DK_A)) * (DK_A ** -0.5)
    k = _l2_norm(k.reshape(b, t, HEADS_A, DK_A))
    v = v.reshape(b, t, HEADS_A, DV_A).astype(f32)
    rate = jnp.exp(a_log.astype(f32))
    bias = dt_bias.astype(f32)
    bt_f = jax.nn.sigmoid(beta_f.astype(f32))
    bt_b = jax.nn.sigmoid(beta_b.astype(f32))
    g_f = -rate[0] * jax.nn.softplus(a_f.astype(f32) + bias[0])
    g_b = -rate[1] * jax.nn.softplus(a_b.astype(f32) + bias[1])
    s0 = state0.astype(f32)
    o_f, s_f = _gated_delta_scan(q, k, v, bt_f, g_f, s0[:, 0])
    o_b, s_b = _gated_delta_scan(_flip(q), _flip(k), _flip(v), _flip(bt_b), _flip(g_b), s0[:, 1])
    o = _rms_norm(o_f + _flip(o_b), norm_g) * jax.nn.silu(za.reshape(b, t, HEADS_A, DV_A).astype(f32))
    return o.reshape(b, t, HEADS_A * DV_A).astype(qa.dtype), jnp.stack([s_f, s_b], axis=1)


def _mlstm_branch(qb, kb, vb, ob, i_f, i_b, f_f, f_b, i_bias, f_bias, norm_g, c0, n0, m0):
    b, t = qb.shape[:2]
    f32 = jnp.float32
    q = qb.reshape(b, t, HEADS_B, DK_B).astype(f32)
    k = kb.reshape(b, t, HEADS_B, DK_B).astype(f32) * (DK_B ** -0.5)
    v = vb.reshape(b, t, HEADS_B, DV_B).astype(f32)
    ib = i_bias.astype(f32)
    fb = f_bias.astype(f32)
    ig_f = i_f.astype(f32) + ib[0]
    ig_b = i_b.astype(f32) + ib[1]
    lf_f = jax.nn.log_sigmoid(f_f.astype(f32) + fb[0])
    lf_b = jax.nn.log_sigmoid(f_b.astype(f32) + fb[1])
    c0 = c0.astype(f32)
    n0 = n0.astype(f32)
    m0 = m0.astype(f32)
    h_f, c_f, n_f, m_f = _mlstm_scan(q, k, v, ig_f, lf_f, c0[:, 0], n0[:, 0], m0[:, 0])
    h_b, c_b, n_b, m_b = _mlstm_scan(_flip(q), _flip(k), _flip(v), _flip(ig_b), _flip(lf_b),
                                     c0[:, 1], n0[:, 1], m0[:, 1])
    h = _rms_norm(h_f + _flip(h_b), norm_g) * jax.nn.sigmoid(ob.reshape(b, t, HEADS_B, DV_B).astype(f32))
    return (h.reshape(b, t, HEADS_B * DV_B).astype(qb.dtype),
            jnp.stack([c_f, c_b], axis=1), jnp.stack([n_f, n_b], axis=1), jnp.stack([m_f, m_b], axis=1))


def _rope_2d(x):
    t_len = x.shape[1]
    n_rows = t_len // GRID_W
    rows, cols = jnp.meshgrid(jnp.arange(n_rows), jnp.arange(GRID_W), indexing="ij")
    row = rows.reshape(-1).astype(jnp.float32)
    col = cols.reshape(-1).astype(jnp.float32)
    half = HEAD_DIM_C // 2
    nf = half // 2
    inv = jnp.power(ROPE_BASE, -jnp.arange(nf, dtype=jnp.float32) / nf)

    def rot(xp, pos):
        ang = pos[:, None] * inv[None, :]
        cos = jnp.cos(ang)[None, :, None, :]
        sin = jnp.sin(ang)[None, :, None, :]
        x1, x2 = xp[..., :nf], xp[..., nf:]
        return jnp.concatenate([x1 * cos - x2 * sin, x1 * sin + x2 * cos], axis=-1)

    x32 = x.astype(jnp.float32)
    return jnp.concatenate([rot(x32[..., :half], row), rot(x32[..., half:], col)], axis=-1).astype(x.dtype)


def _ctx_attention(q, k, v, sink):
    b, p = q.shape[:2]
    nq = p // Q_BLOCK
    scale = HEAD_DIM_C ** -0.5
    qb = q.reshape(b, nq, Q_BLOCK, KV_HEADS_C, GROUP_C, HEAD_DIM_C).transpose(1, 0, 2, 3, 4, 5)
    sink_hg = sink.astype(jnp.float32).reshape(KV_HEADS_C, GROUP_C)

    def block(qi):
        s = jnp.einsum("bqhgd,bkhd->bhgqk", qi, k).astype(jnp.float32) * scale
        sk = jnp.broadcast_to(sink_hg[None, :, :, None, None], s.shape[:-1] + (1,))
        pr = jax.nn.softmax(jnp.concatenate([s, sk], axis=-1), axis=-1)[..., :-1]
        return jnp.einsum("bhgqk,bkhd->bqhgd", pr.astype(v.dtype), v)

    o = lax.map(block, qb)
    return o.transpose(1, 0, 2, 3, 4, 5).reshape(b, p, HEADS_C * HEAD_DIM_C)


def _latent_attention(q, k, v, ck, cv, sink):
    b, t = q.shape[:2]
    p_ctx = ck.shape[1]
    nb = t // Q_BLOCK
    span = Q_BLOCK + 2 * WINDOW
    scale = HEAD_DIM_C ** -0.5
    kp = jnp.pad(k, ((0, 0), (WINDOW, WINDOW), (0, 0), (0, 0)))
    vp = jnp.pad(v, ((0, 0), (WINDOW, WINDOW), (0, 0), (0, 0)))
    qb = q.reshape(b, nb, Q_BLOCK, KV_HEADS_C, GROUP_C, HEAD_DIM_C).transpose(1, 0, 2, 3, 4, 5)
    sink_hg = sink.astype(jnp.float32).reshape(KV_HEADS_C, GROUP_C)
    offs_q = jnp.arange(Q_BLOCK)
    offs_k = jnp.arange(span) - WINDOW

    def block(args):
        i, qi = args
        start = i * Q_BLOCK
        kl = lax.dynamic_slice_in_dim(kp, start, span, axis=1)
        vl = lax.dynamic_slice_in_dim(vp, start, span, axis=1)
        qpos = start + offs_q
        kpos = start + offs_k
        valid = ((jnp.abs(qpos[:, None] - kpos[None, :]) <= WINDOW)
                 & (kpos >= 0)[None, :] & (kpos < t)[None, :])
        sl = jnp.einsum("bqhgd,bkhd->bhgqk", qi, kl).astype(jnp.float32) * scale
        sl = jnp.where(valid, sl, -jnp.inf)
        sc = jnp.einsum("bqhgd,bkhd->bhgqk", qi, ck).astype(jnp.float32) * scale
        sk = jnp.broadcast_to(sink_hg[None, :, :, None, None], sl.shape[:-1] + (1,))
        pr = jax.nn.softmax(jnp.concatenate([sl, sc, sk], axis=-1), axis=-1)
        pl = pr[..., :span].astype(v.dtype)
        pc = pr[..., span:span + p_ctx].astype(cv.dtype)
        return (jnp.einsum("bhgqk,bkhd->bqhgd", pl, vl)
                + jnp.einsum("bhgqk,bkhd->bqhgd", pc, cv).astype(v.dtype))

    o = lax.map(block, (jnp.arange(nb), qb))
    return o.transpose(1, 0, 2, 3, 4, 5).reshape(b, t, HEADS_C * HEAD_DIM_C)


def _expert_choice_ffn(h, w_router, w_gate, w_up, w_down):
    b, t, _ = h.shape
    cap = EC_CAPACITY * t // N_EXPERTS
    aff = jax.nn.softmax((h @ w_router).astype(jnp.float32), axis=-1)
    gates, idx = lax.top_k(aff.transpose(0, 2, 1), cap)
    bidx = jnp.arange(b)[:, None, None]
    xs = h[bidx, idx]
    hid = (jax.nn.silu(jnp.einsum("becd,edf->becf", xs, w_gate))
           * jnp.einsum("becd,edf->becf", xs, w_up))
    ye = jnp.einsum("becf,efd->becd", hid, w_down) * gates[..., None].astype(h.dtype)
    return jnp.zeros_like(h).at[bidx, idx].add(ye)


def _token_mix(h, lw, ctx):
    b, t = h.shape[:2]
    f32 = jnp.float32
    proj = h @ lw["w_in"]
    (qa, ka, va, za, beta_f, beta_b, a_f, a_b,
     qb, kb, vb, ob, i_f, i_b, f_f, f_b,
     qc, kc, vc, ga, gb, gc) = jnp.split(proj, np.cumsum(IN_SIZES)[:-1].tolist(), axis=-1)
    if ctx is None:
        d0 = jnp.zeros((b, 2, HEADS_A, DK_A, DV_A), f32)
        c0 = jnp.zeros((b, 2, HEADS_B, DK_B, DV_B), f32)
        n0 = jnp.zeros((b, 2, HEADS_B, DK_B), f32)
        m0 = jnp.zeros((b, 2, HEADS_B), f32)
    else:
        ck, cv, d0, c0, n0, m0 = ctx
    ya, d_new = _gated_delta_branch(qa, ka, va, za, beta_f, beta_b, a_f, a_b, lw["conv_qkv_a"],
                                    lw["delta_a_log"], lw["delta_dt_bias"], lw["delta_norm_g"], d0)
    yb, c_new, n_new, m_new = _mlstm_branch(qb, kb, vb, ob, i_f, i_b, f_f, f_b, lw["mlstm_i_bias"],
                                            lw["mlstm_f_bias"], lw["mlstm_norm_g"], c0, n0, m0)
    q = qc.reshape(b, t, HEADS_C, HEAD_DIM_C)
    k = kc.reshape(b, t, KV_HEADS_C, HEAD_DIM_C)
    v = vc.reshape(b, t, KV_HEADS_C, HEAD_DIM_C)
    if ctx is None:
        yc = _ctx_attention(q, k, v, lw["attn_sink"])
        new_ctx = (k, v, d_new, c_new, n_new, m_new)
    else:
        yc = _latent_attention(_rope_2d(q), _rope_2d(k), v, ck, cv, lw["attn_sink"])
        new_ctx = None
    mixed = (jax.nn.sigmoid(ga) * (ya @ lw["w_branch_a"])
             + jax.nn.sigmoid(gb) * (yb @ lw["w_branch_b"])
             + jax.nn.sigmoid(gc) * (yc @ lw["w_branch_c"]))
    return mixed @ lw["w_out"], new_ctx


def _layer(x, cvec, lw, ctx):
    mod = jax.nn.silu(cvec) @ lw["ada_w"] + lw["ada_b"]
    sh1, sc1, g1, sh2, sc2, g2 = [m[:, None, :] for m in jnp.split(mod, ADA_CHUNKS, axis=-1)]
    h = _rms_norm(x, lw["norm1_g"]) * (1.0 + sc1) + sh1
    mix, new_ctx = _token_mix(h, lw, ctx)
    x = x + g1 * mix
    h = _rms_norm(x, lw["norm2_g"]) * (1.0 + sc2) + sh2
    x = x + g2 * _expert_choice_ffn(h, lw["w_router"], lw["w_expert_gate"],
                                    lw["w_expert_up"], lw["w_expert_down"])
    return x, new_ctx


def setup_inputs(seed: int = 0) -> dict:
    key = jax.random.key(seed)
    ks = iter(jax.random.split(key, 40))
    f32 = jnp.float32

    def nrm(shape, scale=1.0):
        return scale * jax.random.normal(next(ks), shape, f32)

    def gain(shape):
        return 1.0 + nrm(shape, 0.02)

    return {
        "x_prompt": nrm((BATCH, SEQ, D_MODEL)),
        "x_sample": nrm((DEC_BATCH, DEC_SEQ, D_MODEL)),
        "cache_attn_k": nrm((DEC_BATCH, DEPTH, PAST_LEN, KV_HEADS_C, HEAD_DIM_C)),
        "cache_attn_v": nrm((DEC_BATCH, DEPTH, PAST_LEN, KV_HEADS_C, HEAD_DIM_C)),
        "state_delta": nrm((DEC_BATCH, DEPTH, 2, HEADS_A, DK_A, DV_A), 0.2),
        "state_mlstm_c": nrm((DEC_BATCH, DEPTH, 2, HEADS_B, DK_B, DV_B), 0.2),
        "state_mlstm_n": nrm((DEC_BATCH, DEPTH, 2, HEADS_B, DK_B), 0.2),
        "state_mlstm_m": nrm((DEC_BATCH, DEPTH, 2, HEADS_B)),
        "c": nrm((DEC_BATCH, D_MODEL)),
        "c_ctx": nrm((D_MODEL,)),
        "ada_w": nrm((DEPTH, D_MODEL, ADA_CHUNKS * D_MODEL), D_MODEL ** -0.5),
        "ada_b": nrm((DEPTH, ADA_CHUNKS * D_MODEL), 0.02),
        "norm1_g": gain((DEPTH, D_MODEL)),
        "norm2_g": gain((DEPTH, D_MODEL)),
        "w_in": nrm((DEPTH, D_MODEL, N_IN), D_MODEL ** -0.5),
        "conv_qkv_a": nrm((DEPTH, CONV_K, CONV_CH_A), CONV_K ** -0.5),
        "delta_a_log": jnp.log(jax.random.uniform(next(ks), (DEPTH, 2, HEADS_A), f32, 1.0, 16.0)),
        "delta_dt_bias": nrm((DEPTH, 2, HEADS_A), 0.5) - 4.0,
        "delta_norm_g": gain((DEPTH, DV_A)),
        "mlstm_i_bias": nrm((DEPTH, 2, HEADS_B), 0.1),
        "mlstm_f_bias": nrm((DEPTH, 2, HEADS_B), 0.5) + 3.0,
        "mlstm_norm_g": gain((DEPTH, DV_B)),
        "attn_sink": nrm((DEPTH, HEADS_C), 0.5),
        "w_branch_a": nrm((DEPTH, HEADS_A * DV_A, D_MODEL), (HEADS_A * DV_A) ** -0.5),
        "w_branch_b": nrm((DEPTH, HEADS_B * DV_B, D_MODEL), (HEADS_B * DV_B) ** -0.5),
        "w_branch_c": nrm((DEPTH, HEADS_C * HEAD_DIM_C, D_MODEL), (HEADS_C * HEAD_DIM_C) ** -0.5),
        "w_out": nrm((DEPTH, D_MODEL, D_MODEL), D_MODEL ** -0.5),
        "w_router": nrm((DEPTH, D_MODEL, N_EXPERTS), D_MODEL ** -0.5),
        "w_expert_gate": nrm((DEPTH, N_EXPERTS, D_MODEL, D_EXPERT), D_MODEL ** -0.5),
        "w_expert_up": nrm((DEPTH, N_EXPERTS, D_MODEL, D_EXPERT), D_MODEL ** -0.5),
        "w_expert_down": nrm((DEPTH, N_EXPERTS, D_EXPERT, D_MODEL), D_EXPERT ** -0.5),
        "final_norm_g": gain((D_MODEL,)),
    }


def reference(x_prompt, x_sample, cache_attn_k, cache_attn_v, state_delta, state_mlstm_c, state_mlstm_n,
              state_mlstm_m, c, c_ctx, ada_w, ada_b, norm1_g, norm2_g, w_in, conv_qkv_a, delta_a_log,
              delta_dt_bias, delta_norm_g, mlstm_i_bias, mlstm_f_bias, mlstm_norm_g, attn_sink, w_branch_a,
              w_branch_b, w_branch_c, w_out, w_router, w_expert_gate, w_expert_up, w_expert_down, final_norm_g):
    def layer_weights(l):
        return {
            "ada_w": ada_w[l], "ada_b": ada_b[l], "norm1_g": norm1_g[l], "norm2_g": norm2_g[l],
            "w_in": w_in[l], "conv_qkv_a": conv_qkv_a[l], "delta_a_log": delta_a_log[l],
            "delta_dt_bias": delta_dt_bias[l], "delta_norm_g": delta_norm_g[l],
            "mlstm_i_bias": mlstm_i_bias[l], "mlstm_f_bias": mlstm_f_bias[l], "mlstm_norm_g": mlstm_norm_g[l],
            "attn_sink": attn_sink[l], "w_branch_a": w_branch_a[l], "w_branch_b": w_branch_b[l],
            "w_branch_c": w_branch_c[l], "w_out": w_out[l], "w_router": w_router[l],
            "w_expert_gate": w_expert_gate[l], "w_expert_up": w_expert_up[l], "w_expert_down": w_expert_down[l],
        }

    xp = x_prompt
    ctx_cond = c_ctx[None, :]
    ks, vs, ds, cs, ns, ms = [], [], [], [], [], []
    for l in range(DEPTH):
        xp, (k_l, v_l, d_l, c_l, n_l, m_l) = _layer(xp, ctx_cond, layer_weights(l), None)
        ks.append(k_l)
        vs.append(v_l)
        ds.append(d_l)
        cs.append(c_l)
        ns.append(n_l)
        ms.append(m_l)
    y_prompt = _rms_norm(xp, final_norm_g)
    dt = x_prompt.dtype
    new_attn_k = jnp.stack(ks, axis=1).astype(dt)
    new_attn_v = jnp.stack(vs, axis=1).astype(dt)
    new_state_delta = jnp.stack(ds, axis=1).astype(dt)
    new_state_mlstm_c = jnp.stack(cs, axis=1).astype(dt)
    new_state_mlstm_n = jnp.stack(ns, axis=1).astype(dt)
    new_state_mlstm_m = jnp.stack(ms, axis=1).astype(dt)

    xs = x_sample
    for l in range(DEPTH):
        ctx_l = (cache_attn_k[:, l], cache_attn_v[:, l], state_delta[:, l],
                 state_mlstm_c[:, l], state_mlstm_n[:, l], state_mlstm_m[:, l])
        xs, _ = _layer(xs, c, layer_weights(l), ctx_l)
    y_sample = _rms_norm(xs, final_norm_g)

    return (y_prompt, y_sample, new_attn_k, new_attn_v, new_state_delta,
            new_state_mlstm_c, new_state_mlstm_n, new_state_mlstm_m)
```

```python
import functools

import jax
import jax.numpy as jnp
from jax import lax
from jax.experimental import pallas as pl
from jax.experimental.pallas import tpu as pltpu

F32 = jnp.float32
BF16 = jnp.bfloat16

D_MODEL = 1024
DEPTH = 2
GRID_W = 64
EPS = 1e-6
HEADS_A = 4
DK_A = 128
DV_A = 128
CONV_K = 5
CHUNK = 64
HEADS_B = 4
DK_B = 64
DV_B = 128
HEADS_C = 8
KV_HEADS_C = 2
HEAD_DIM_C = 64
GROUP_C = HEADS_C // KV_HEADS_C
WINDOW = 128
Q_BLOCK = 128
ROPE_BASE = 10000.0
N_EXPERTS = 16
D_EXPERT = 512
EC_CAPACITY = 2
ADA_CHUNKS = 6

LANE = 128
MOD_ROWS = 8
COND_ROWS = 16

T_GA, T_GB, T_GC = 0, 8, 16
T_QA, T_KA, T_VA, T_ZA = 24, 28, 32, 36
T_QB, T_KB, T_VB, T_OB = 40, 42, 44, 48
T_QC, T_KC, T_VC = 52, 56, 57
T_GATES = 58
N_TILES = 60
N_PROJ = N_TILES * LANE
GATE_STRIDE = 8
GATE_B_OFF = HEADS_A * GATE_STRIDE

VMEM_LIMIT = 48 * 1024 * 1024


def _cparams(*sem):
    return pltpu.CompilerParams(dimension_semantics=sem, vmem_limit_bytes=VMEM_LIMIT)


def _bf(x):
    return x.astype(BF16)


def _mm(a, b):
    return jnp.dot(_bf(a), _bf(b), preferred_element_type=F32)


def _mm_nt(a, b):
    return lax.dot_general(_bf(a), _bf(b), (((1,), (1,)), ((), ())), preferred_element_type=F32)


def _mm_tn(a, b):
    return lax.dot_general(_bf(a), _bf(b), (((0,), (0,)), ((), ())), preferred_element_type=F32)


def _split2(x):
    hi = _bf(x)
    return hi, _bf(x - hi.astype(F32))


def _split3(x):
    hi = _bf(x)
    r = x - hi.astype(F32)
    mid = _bf(r)
    return hi, mid, _bf(r - mid.astype(F32))


def _mm_sel(sel, x):
    hi, mid, lo = _split3(x)
    d = functools.partial(jnp.dot, preferred_element_type=F32)
    return (d(sel, hi) + d(sel, mid)) + d(sel, lo)


def _mm_hi(a, b):
    ah, al = _split2(a)
    bh, bl = _split2(b)
    d = functools.partial(jnp.dot, preferred_element_type=F32)
    return d(ah, bh) + (d(ah, bl) + d(al, bh))


def _sigmoid(x):
    return 1.0 / (1.0 + jnp.exp(-x))


def _silu(x):
    return x * _sigmoid(x)


def _softplus(x):
    return jnp.maximum(x, 0.0) + jnp.log(1.0 + jnp.exp(-jnp.abs(x)))


def _rms(x, g):
    return x * lax.rsqrt(jnp.mean(x * x, axis=-1, keepdims=True) + EPS) * g


def _chunk_masks(backward):
    ri = lax.broadcasted_iota(jnp.int32, (CHUNK, CHUNK), 0)
    ci = lax.broadcasted_iota(jnp.int32, (CHUNK, CHUNK), 1)
    if backward:
        return ri <= ci, ri < ci, ri == ci
    return ri >= ci, ri > ci, ri == ci


def _unit_triangular_inverse(low):
    ri = lax.broadcasted_iota(jnp.int32, (CHUNK, CHUNK), 0)
    ci = lax.broadcasted_iota(jnp.int32, (CHUNK, CHUNK), 1)
    inv = jnp.where(ri == ci, 1.0, 0.0) - low
    power = low
    span = 2
    while span < CHUNK:
        power = _mm_hi(power, power)
        inv = inv + _mm_hi(inv, power)
        span *= 2
    return inv


def _mod_kernel(c_ref, w_ref, b_ref, o_ref):
    o_ref[0] = _mm(_silu(c_ref[...]), w_ref[0]) + b_ref[0]


def _modulation(cond, ada_w, ada_b):
    n_out = ADA_CHUNKS * D_MODEL
    tn = 512
    return pl.pallas_call(
        _mod_kernel,
        grid=(DEPTH, n_out // tn),
        in_specs=[
            pl.BlockSpec((COND_ROWS, D_MODEL), lambda l, j: (0, 0)),
            pl.BlockSpec((1, D_MODEL, tn), lambda l, j: (l, 0, j)),
            pl.BlockSpec((1, 1, tn), lambda l, j: (l, 0, j)),
        ],
        out_specs=pl.BlockSpec((1, COND_ROWS, tn), lambda l, j: (l, 0, j)),
        out_shape=jax.ShapeDtypeStruct((DEPTH, COND_ROWS, n_out), F32),
        compiler_params=_cparams("parallel", "parallel"),
    )(cond, ada_w, ada_b.reshape(DEPTH, 1, n_out))


def _in_proj_kernel(x_ref, mod_ref, g_ref, w_ref, o_ref, h_scr):
    @pl.when(pl.program_id(1) == 0)
    def _():
        m = mod_ref[0]
        h_scr[...] = _bf(_rms(x_ref[...], g_ref[0]) * (1.0 + m[1:2]) + m[0:1])

    o_ref[...] = jnp.dot(h_scr[...], w_ref[0], preferred_element_type=F32)


def _in_proj(x2d, mod, norm_g, w_in, layer, seq, per_request):
    m_rows = x2d.shape[0]
    tm = min(1024, m_rows)
    tn = 768
    if per_request:
        assert seq % tm == 0
    return pl.pallas_call(
        _in_proj_kernel,
        grid=(m_rows // tm, N_PROJ // tn),
        in_specs=[
            pl.BlockSpec((tm, D_MODEL), lambda i, j: (i, 0)),
            pl.BlockSpec((1, MOD_ROWS, D_MODEL), lambda i, j: ((i * tm) // seq if per_request else 0, 0, 0)),
            pl.BlockSpec((1, 1, D_MODEL), lambda i, j: (layer, 0, 0)),
            pl.BlockSpec((1, D_MODEL, tn), lambda i, j: (layer, 0, j)),
        ],
        out_specs=pl.BlockSpec((tm, tn), lambda i, j: (i, j)),
        out_shape=jax.ShapeDtypeStruct((m_rows, N_PROJ), F32),
        scratch_shapes=[pltpu.VMEM((tm, D_MODEL), BF16)],
        compiler_params=_cparams("parallel", "arbitrary"),
    )(x2d, mod, norm_g.reshape(DEPTH, 1, D_MODEL), w_in)


def _gate_column(gt, lane_index):
    lane = lax.broadcasted_iota(jnp.int32, (1, LANE), 1)
    return jnp.sum(jnp.where(lane == lane_index, gt, 0.0), axis=1, keepdims=True)


def _delta_kernel(q_ref, k_ref, v_ref, z_ref, gt_ref, cq_ref, ck_ref, cv_ref, par_ref, ng_ref, *rest,
                  seq, has_init, emit_state):
    rest = list(rest)
    s0_ref = rest.pop(0) if has_init else None
    y_ref = rest.pop(0)
    st_ref = rest.pop(0) if emit_state else None
    qs, ks, vs, gates, s_scr, o_scr = rest
    head = pl.program_id(1)
    n_chunks = seq // CHUNK
    rows_t = lax.broadcasted_iota(jnp.int32, (seq, 1), 0)

    def conv_silu(x_ref, w_ref):
        x = x_ref[...]
        w = w_ref[0]
        acc = x * w[CONV_K // 2:CONV_K // 2 + 1, :]
        for j in range(CONV_K):
            s = j - CONV_K // 2
            if s == 0:
                continue
            shifted = pltpu.roll(x, (-s) % seq, 0)
            ok = (rows_t + s >= 0) & (rows_t + s < seq)
            acc = acc + jnp.where(ok, shifted, 0.0) * w[j:j + 1, :]
        return _silu(acc)

    def l2n(x):
        return x * lax.rsqrt(jnp.sum(x * x, axis=-1, keepdims=True) + EPS)

    qs[...] = l2n(conv_silu(q_ref, cq_ref)) * (DK_A ** -0.5)
    ks[...] = l2n(conv_silu(k_ref, ck_ref))
    vs[...] = conv_silu(v_ref, cv_ref)

    gt = gt_ref[...]
    par = par_ref[0]
    base = head * GATE_STRIDE
    for d in range(2):
        gates[d] = _sigmoid(_gate_column(gt, base + d))
        gates[2 + d] = -jnp.exp(par[d:d + 1, 0:1]) * _softplus(_gate_column(gt, base + 2 + d) + par[2 + d:3 + d, 0:1])
        s_scr[d] = s0_ref[0, 0, d, 0] if has_init else jnp.zeros((DK_A, DV_A), F32)

    def chunk(c, d):
        incl, strict, _ = _chunk_masks(d == 1)
        rows = pl.ds(pl.multiple_of(c * CHUNK, CHUNK), CHUNK)
        q = qs[rows, :]
        k = ks[rows, :]
        v = vs[rows, :]
        bt = gates[d, rows, :]
        g = gates[2 + d, rows, :]
        rhs = jnp.concatenate([jnp.where(strict, g, 0.0), jnp.broadcast_to(g, (CHUNK, CHUNK))], axis=1)
        cs = _mm_sel(jnp.where(incl, 1.0, 0.0).astype(BF16), rhs)
        gc = cs[:, CHUNK:CHUNK + 1]
        g_last = gc[0:1, :] if d == 1 else gc[CHUNK - 1:CHUNK, :]
        egc = jnp.exp(gc)
        dec = jnp.where(incl, jnp.exp(cs[:, :CHUNK]), 0.0)
        kk = _mm_nt(k, k)
        qk = _mm_nt(q, k)
        ainv = _unit_triangular_inverse(jnp.where(strict, bt * kk * dec, 0.0))
        sol = _mm_hi(ainv, jnp.concatenate([v * bt, k * (bt * egc)], axis=1))
        u = sol[:, :DV_A]
        w = sol[:, DV_A:]
        state = s_scr[d]
        v_new = u - _mm(w, state)
        o_scr[d, rows, :] = _mm(q * egc, state) + _mm(qk * dec, v_new)
        s_scr[d] = jnp.exp(g_last) * state + _mm_tn(k * jnp.exp(g_last - gc), v_new)

    def body(c, carry):
        chunk(c, 0)
        chunk(n_chunks - 1 - c, 1)
        return carry

    lax.fori_loop(0, n_chunks, body, 0)

    z = z_ref[...]
    y_ref[...] = _rms(o_scr[0] + o_scr[1], ng_ref[0]) * _silu(z)
    if emit_state:
        st_ref[0, 0, 0] = s_scr[0]
        st_ref[0, 1, 0] = s_scr[1]


def _delta_mixer(proj, conv_w, par, norm_g, layer, batch, seq, state0):
    has_init = state0 is not None
    emit_state = not has_init
    col = lambda tile: (lambda r, h: (r, tile + h))
    cw = lambda part: (lambda r, h: (layer, 0, part * HEADS_A + h))
    in_specs = [
        pl.BlockSpec((seq, LANE), col(T_QA)),
        pl.BlockSpec((seq, LANE), col(T_KA)),
        pl.BlockSpec((seq, LANE), col(T_VA)),
        pl.BlockSpec((seq, LANE), col(T_ZA)),
        pl.BlockSpec((seq, LANE), lambda r, h: (r, T_GATES)),
        pl.BlockSpec((1, CONV_K, LANE), cw(0)),
        pl.BlockSpec((1, CONV_K, LANE), cw(1)),
        pl.BlockSpec((1, CONV_K, LANE), cw(2)),
        pl.BlockSpec((1, 4, LANE), lambda r, h: (layer * HEADS_A + h, 0, 0)),
        pl.BlockSpec((1, 1, DV_A), lambda r, h: (layer, 0, 0)),
    ]
    args = [proj, proj, proj, proj, proj, conv_w, conv_w, conv_w, par, norm_g.reshape(DEPTH, 1, DV_A)]
    if has_init:
        in_specs.append(pl.BlockSpec((1, 1, 2, 1, DK_A, DV_A), lambda r, h: (r, layer, 0, h, 0, 0)))
        args.append(state0)
    out_specs = [pl.BlockSpec((seq, DV_A), lambda r, h: (r, h))]
    out_shape = [jax.ShapeDtypeStruct((batch * seq, HEADS_A * DV_A), F32)]
    if emit_state:
        out_specs.append(pl.BlockSpec((1, 2, 1, DK_A, DV_A), lambda r, h: (r, 0, h, 0, 0)))
        out_shape.append(jax.ShapeDtypeStruct((batch, 2, HEADS_A, DK_A, DV_A), F32))
    outs = pl.pallas_call(
        functools.partial(_delta_kernel, seq=seq, has_init=has_init, emit_state=emit_state),
        grid=(batch, HEADS_A),
        in_specs=in_specs,
        out_specs=out_specs,
        out_shape=out_shape,
        scratch_shapes=[
            pltpu.VMEM((seq, DK_A), F32), pltpu.VMEM((seq, DK_A), F32), pltpu.VMEM((seq, DV_A), F32),
            pltpu.VMEM((4, seq, 1), F32), pltpu.VMEM((2, DK_A, DV_A), F32), pltpu.VMEM((2, seq, DV_A), F32),
        ],
        compiler_params=_cparams("parallel", "parallel"),
    )(*args)
    return (outs[0], outs[1]) if emit_state else (outs[0], None)


HEADS_PER_STEP_B = 2


def _mlstm_kernel(q_ref, k_ref, v_ref, og_ref, gt_ref, par_ref, ng_ref, *rest, seq, has_init, emit_state):
    rest = list(rest)
    if has_init:
        c0_ref, n0_ref, m0_ref = rest[:3]
        rest = rest[3:]
    y_ref = rest.pop(0)
    if emit_state:
        co_ref, no_ref, mo_ref = rest[:3]
        rest = rest[3:]
    gates, c_scr, n_scr, m_scr, h_scr = rest
    pair = pl.program_id(1)
    n_chunks = seq // CHUNK
    gt = gt_ref[...]

    for j in range(HEADS_PER_STEP_B):
        par = par_ref[j]
        base = GATE_B_OFF + (pair * HEADS_PER_STEP_B + j) * GATE_STRIDE
        for d in range(2):
            gates[j * 4 + d] = _gate_column(gt, base + d) + par[d:d + 1, 0:1]
            gates[j * 4 + 2 + d] = -_softplus(-(_gate_column(gt, base + 2 + d) + par[2 + d:3 + d, 0:1]))
            idx = j * 2 + d
            if has_init:
                c_scr[idx] = c0_ref[0, 0, d, j]
                n_scr[idx] = n0_ref[0, 0, d, j]
                m_scr[idx] = m0_ref[0, 0, d, j][:, 0:1]
            else:
                c_scr[idx] = jnp.zeros((DK_B, DV_B), F32)
                n_scr[idx] = jnp.zeros((1, DK_B), F32)
                m_scr[idx] = jnp.zeros((1, 1), F32)

    def chunk(c, d, j):
        incl, strict, diag = _chunk_masks(d == 1)
        idx = j * 2 + d
        rows = pl.ds(pl.multiple_of(c * CHUNK, CHUNK), CHUNK)
        q = q_ref[rows, j * DK_B:(j + 1) * DK_B]
        k = k_ref[rows, j * DK_B:(j + 1) * DK_B] * (DK_B ** -0.5)
        v = v_ref[rows, j * DV_B:(j + 1) * DV_B]
        ig = gates[j * 4 + d, rows, :]
        lf = gates[j * 4 + 2 + d, rows, :]
        rhs = jnp.concatenate(
            [jnp.where(strict, lf, 0.0) + jnp.where(diag, ig, 0.0), jnp.broadcast_to(lf, (CHUNK, CHUNK))], axis=1)
        cs = _mm_sel(jnp.where(incl, 1.0, 0.0).astype(BF16), rhs)
        bc = cs[:, CHUNK:CHUNK + 1]
        b_last = bc[0:1, :] if d == 1 else bc[CHUNK - 1:CHUNK, :]
        d_log = jnp.where(incl, cs[:, :CHUNK], -jnp.inf)
        d_max = jnp.max(d_log, axis=1, keepdims=True)
        tok = b_last - bc + ig
        tok_max = jnp.max(tok, axis=0, keepdims=True)
        qk = _mm_nt(q, k)
        cmat = c_scr[idx]
        nvec = n_scr[idx]
        m_prev = m_scr[idx]
        m_t = jnp.maximum(bc + m_prev, d_max)
        w_inter = jnp.exp(bc + m_prev - m_t)
        p = jnp.exp(d_log - m_t) * qk
        num = w_inter * _mm(q, cmat) + _mm(p, v)
        den = w_inter * jnp.sum(q * nvec, axis=1, keepdims=True) + jnp.sum(p, axis=1, keepdims=True)
        h_scr[idx, rows, :] = num / jnp.maximum(jnp.abs(den), jnp.exp(-m_t))
        m_new = jnp.maximum(b_last + m_prev, tok_max)
        w_prev = jnp.exp(b_last + m_prev - m_new)
        kw = k * jnp.exp(tok - m_new)
        c_scr[idx] = w_prev * cmat + _mm_tn(kw, v)
        n_scr[idx] = w_prev * nvec + jnp.sum(kw, axis=0, keepdims=True)
        m_scr[idx] = m_new

    def body(c, carry):
        for j in range(HEADS_PER_STEP_B):
            chunk(c, 0, j)
            chunk(n_chunks - 1 - c, 1, j)
        return carry

    lax.fori_loop(0, n_chunks, body, 0)

    og = og_ref[...]
    for j in range(HEADS_PER_STEP_B):
        h = h_scr[j * 2] + h_scr[j * 2 + 1]
        y_ref[:, j * DV_B:(j + 1) * DV_B] = _rms(h, ng_ref[0]) * _sigmoid(og[:, j * DV_B:(j + 1) * DV_B])
        if emit_state:
            for d in range(2):
                co_ref[0, d, j] = c_scr[j * 2 + d]
                no_ref[0, d, j] = n_scr[j * 2 + d]
                mo_ref[0, d, j] = jnp.broadcast_to(m_scr[j * 2 + d], (1, LANE))


def _mlstm_mixer(proj, par, norm_g, layer, batch, seq, state0):
    has_init = state0 is not None
    emit_state = not has_init
    hp = HEADS_PER_STEP_B
    n_pairs = HEADS_B // hp
    in_specs = [
        pl.BlockSpec((seq, hp * DK_B), lambda r, p: (r, T_QB + p)),
        pl.BlockSpec((seq, hp * DK_B), lambda r, p: (r, T_KB + p)),
        pl.BlockSpec((seq, hp * DV_B), lambda r, p: (r, T_VB // hp + p)),
        pl.BlockSpec((seq, hp * DV_B), lambda r, p: (r, T_OB // hp + p)),
        pl.BlockSpec((seq, LANE), lambda r, p: (r, T_GATES)),
        pl.BlockSpec((hp, 4, LANE), lambda r, p: (layer * n_pairs + p, 0, 0)),
        pl.BlockSpec((1, 1, DV_B), lambda r, p: (layer, 0, 0)),
    ]
    args = [proj, proj, proj, proj, proj, par, norm_g.reshape(DEPTH, 1, DV_B)]
    if has_init:
        c0, n0, m0 = state0
        in_specs += [
            pl.BlockSpec((1, 1, 2, hp, DK_B, DV_B), lambda r, p: (r, layer, 0, p, 0, 0)),
            pl.BlockSpec((1, 1, 2, hp, 1, DK_B), lambda r, p: (r, layer, 0, p, 0, 0)),
            pl.BlockSpec((1, 1, 2, hp, 1, LANE), lambda r, p: (r, layer, 0, p, 0, 0)),
        ]
        args += [c0, n0, m0]
    out_specs = [pl.BlockSpec((seq, hp * DV_B), lambda r, p: (r, p))]
    out_shape = [jax.ShapeDtypeStruct((batch * seq, HEADS_B * DV_B), F32)]
    if emit_state:
        out_specs += [
            pl.BlockSpec((1, 2, hp, DK_B, DV_B), lambda r, p: (r, 0, p, 0, 0)),
            pl.BlockSpec((1, 2, hp, 1, DK_B), lambda r, p: (r, 0, p, 0, 0)),
            pl.BlockSpec((1, 2, hp, 1, LANE), lambda r, p: (r, 0, p, 0, 0)),
        ]
        out_shape += [
            jax.ShapeDtypeStruct((batch, 2, HEADS_B, DK_B, DV_B), F32),
            jax.ShapeDtypeStruct((batch, 2, HEADS_B, 1, DK_B), F32),
            jax.ShapeDtypeStruct((batch, 2, HEADS_B, 1, LANE), F32),
        ]
    outs = pl.pallas_call(
        functools.partial(_mlstm_kernel, seq=seq, has_init=has_init, emit_state=emit_state),
        grid=(batch, n_pairs),
        in_specs=in_specs,
        out_specs=out_specs,
        out_shape=out_shape,
        scratch_shapes=[
            pltpu.VMEM((4 * hp, seq, 1), F32), pltpu.VMEM((2 * hp, DK_B, DV_B), F32),
            pltpu.VMEM((2 * hp, 1, DK_B), F32), pltpu.VMEM((2 * hp, 1, 1), F32),
            pltpu.VMEM((2 * hp, seq, DV_B), F32),
        ],
        compiler_params=_cparams("parallel", "parallel"),
    )(*args)
    if emit_state:
        return outs[0], (outs[1], outs[2][:, :, :, 0, :], outs[3][:, :, :, 0, 0])
    return outs[0], None


def _stack_heads(q, kv_head):
    return jnp.concatenate(
        [q[:, (kv_head * GROUP_C + g) * HEAD_DIM_C:(kv_head * GROUP_C + g + 1) * HEAD_DIM_C] for g in range(GROUP_C)],
        axis=0)


def _sink_column(sink, kv_head, rows):
    return jnp.concatenate(
        [jnp.broadcast_to(sink[kv_head * GROUP_C + g:kv_head * GROUP_C + g + 1, 0:1], (rows, 1))
         for g in range(GROUP_C)], axis=0)


def _unstack_heads(per_kv, rows):
    return jnp.concatenate(
        [o[g * rows:(g + 1) * rows, :] for o in per_kv for g in range(GROUP_C)], axis=1)


def _ctx_attn_kernel(q_ref, k_ref, v_ref, sink_ref, o_ref, *, seq):
    q = q_ref[...]
    k = k_ref[...]
    v = v_ref[...]
    sink = sink_ref[0]
    scale = HEAD_DIM_C ** -0.5
    outs = []
    for kv in range(KV_HEADS_C):
        lanes = slice(kv * HEAD_DIM_C, (kv + 1) * HEAD_DIM_C)
        s = _mm_nt(_stack_heads(q, kv), k[:, lanes]) * scale
        sk = _sink_column(sink, kv, seq)
        m = jnp.maximum(jnp.max(s, axis=1, keepdims=True), sk)
        e = jnp.exp(s - m)
        den = jnp.sum(e, axis=1, keepdims=True) + jnp.exp(sk - m)
        outs.append(_mm(e / den, v[:, lanes]))
    o_ref[...] = _unstack_heads(outs, seq)


def _ctx_attention(proj, sink, layer, batch, seq):
    width = HEADS_C * HEAD_DIM_C
    return pl.pallas_call(
        functools.partial(_ctx_attn_kernel, seq=seq),
        grid=(batch,),
        in_specs=[
            pl.BlockSpec((seq, width), lambda r: (r, T_QC * LANE // width)),
            pl.BlockSpec((seq, LANE), lambda r: (r, T_KC)),
            pl.BlockSpec((seq, LANE), lambda r: (r, T_VC)),
            pl.BlockSpec((1, HEADS_C, LANE), lambda r: (layer, 0, 0)),
        ],
        out_specs=pl.BlockSpec((seq, width), lambda r: (r, 0)),
        out_shape=jax.ShapeDtypeStruct((batch * seq, width), F32),
        compiler_params=_cparams("parallel"),
    )(proj, proj, proj, sink)


def _rope(x, cos, sin):
    quarter = HEAD_DIM_C // 4
    lane = lax.broadcasted_iota(jnp.int32, (1, LANE), 1)
    first = (lane % (2 * quarter)) < quarter
    partner = jnp.where(first, -pltpu.roll(x, LANE - quarter, 1), pltpu.roll(x, quarter, 1))
    return x * cos + partner * sin


def _latent_attn_kernel(q_ref, k_ref, v_ref, ck_ref, cv_ref, cq_ref, sq_ref, cos_ref, sin_ref, sink_ref, o_ref, *, seq):
    blk = pl.program_id(1)
    span = Q_BLOCK + 2 * WINDOW
    start = blk * Q_BLOCK
    k_start = pl.multiple_of(jnp.clip(start - WINDOW, 0, seq - span), Q_BLOCK)
    win = pl.ds(k_start, span)
    cq = cq_ref[...]
    sq = sq_ref[...]
    q = jnp.concatenate(
        [_rope(q_ref[:, s * LANE:(s + 1) * LANE], cq, sq) for s in range(HEADS_C * HEAD_DIM_C // LANE)], axis=1)
    k = _rope(k_ref[win, :], cos_ref[win, :], sin_ref[win, :])
    v = v_ref[win, :]
    ck = ck_ref[0, 0]
    cv = cv_ref[0, 0]
    sink = sink_ref[0]
    scale = HEAD_DIM_C ** -0.5
    q_pos = start + lax.broadcasted_iota(jnp.int32, (GROUP_C * Q_BLOCK, 1), 0) % Q_BLOCK
    k_pos = k_start + lax.broadcasted_iota(jnp.int32, (1, span), 1)
    valid = jnp.abs(q_pos - k_pos) <= WINDOW
    outs = []
    for kv in range(KV_HEADS_C):
        lanes = slice(kv * HEAD_DIM_C, (kv + 1) * HEAD_DIM_C)
        qs = _stack_heads(q, kv)
        sl = jnp.where(valid, _mm_nt(qs, k[:, lanes]) * scale, -jnp.inf)
        sc = _mm_nt(qs, ck[:, lanes]) * scale
        sk = _sink_column(sink, kv, Q_BLOCK)
        m = jnp.maximum(jnp.maximum(jnp.max(sl, axis=1, keepdims=True), jnp.max(sc, axis=1, keepdims=True)), sk)
        el = jnp.exp(sl - m)
        ec = jnp.exp(sc - m)
        den = jnp.sum(el, axis=1, keepdims=True) + jnp.sum(ec, axis=1, keepdims=True) + jnp.exp(sk - m)
        outs.append(_mm(el / den, v[:, lanes]) + _mm(ec / den, cv[:, lanes]))
    o_ref[...] = _unstack_heads(outs, Q_BLOCK)


def _latent_attention(proj, cache_k, cache_v, cos, sin, sink, layer, batch, seq):
    width = HEADS_C * HEAD_DIM_C
    n_blk = seq // Q_BLOCK
    past = cache_k.shape[2]
    return pl.pallas_call(
        functools.partial(_latent_attn_kernel, seq=seq),
        grid=(batch, n_blk),
        in_specs=[
            pl.BlockSpec((Q_BLOCK, width), lambda r, i: (r * n_blk + i, T_QC * LANE // width)),
            pl.BlockSpec((seq, LANE), lambda r, i: (r, T_KC)),
            pl.BlockSpec((seq, LANE), lambda r, i: (r, T_VC)),
            pl.BlockSpec((1, 1, past, LANE), lambda r, i: (r, layer, 0, 0)),
            pl.BlockSpec((1, 1, past, LANE), lambda r, i: (r, layer, 0, 0)),
            pl.BlockSpec((Q_BLOCK, LANE), lambda r, i: (i, 0)),
            pl.BlockSpec((Q_BLOCK, LANE), lambda r, i: (i, 0)),
            pl.BlockSpec((seq, LANE), lambda r, i: (0, 0)),
            pl.BlockSpec((seq, LANE), lambda r, i: (0, 0)),
            pl.BlockSpec((1, HEADS_C, LANE), lambda r, i: (layer, 0, 0)),
        ],
        out_specs=pl.BlockSpec((Q_BLOCK, width), lambda r, i: (r * n_blk + i, 0)),
        out_shape=jax.ShapeDtypeStruct((batch * seq, width), F32),
        compiler_params=_cparams("parallel", "parallel"),
    )(proj, proj, proj, cache_k, cache_v, cos, sin, cos, sin, sink)


def _rope_tables(seq):
    quarter = HEAD_DIM_C // 4
    pos = jnp.arange(seq)
    row = (pos // GRID_W).astype(F32)
    col = (pos % GRID_W).astype(F32)
    inv = jnp.power(ROPE_BASE, -jnp.arange(quarter, dtype=F32) / quarter)
    ang_row = row[:, None] * inv[None, :]
    ang_col = col[:, None] * inv[None, :]
    ang = jnp.concatenate([ang_row, ang_row, ang_col, ang_col], axis=1)
    ang = jnp.concatenate([ang] * (LANE // HEAD_DIM_C), axis=1)
    return jnp.cos(ang), jnp.sin(ang)


def _mix_kernel(x_ref, ya_ref, yb_ref, yc_ref, ga_ref, gb_ref, gc_ref, mod_ref, n2_ref,
                wa_ref, wb_ref, wc_ref, wo_ref, wr_ref, xo_ref, h_ref, aff_ref, afft_ref):
    m = mod_ref[0]
    mixed = (_sigmoid(ga_ref[...]) * _mm(ya_ref[...], wa_ref[0])
             + _sigmoid(gb_ref[...]) * _mm(yb_ref[...], wb_ref[0])
             + _sigmoid(gc_ref[...]) * _mm(yc_ref[...], wc_ref[0]))
    x = x_ref[...] + m[2:3] * _mm(mixed, wo_ref[0])
    xo_ref[...] = x
    h = _bf(_rms(x, n2_ref[0]) * (1.0 + m[4:5]) + m[3:4])
    h_ref[...] = h
    logits = jnp.dot(h, wr_ref[0], preferred_element_type=F32)
    lane = lax.broadcasted_iota(jnp.int32, (1, LANE), 1)
    logits = jnp.where(lane < N_EXPERTS, logits, -jnp.inf)
    e = jnp.exp(logits - jnp.max(logits, axis=1, keepdims=True))
    aff = e / jnp.sum(e, axis=1, keepdims=True)
    aff_ref[...] = aff
    afft_ref[...] = aff.T[:N_EXPERTS, :]


def _mix(x2d, ya, yb, yc, proj, mod, norm2_g, wa, wb, wc, wo, wr, layer, seq, per_request):
    m_rows = x2d.shape[0]
    tm = 256
    gate_blk = lambda tile: (lambda i: (i, tile * LANE // D_MODEL))
    wspec = lambda w: pl.BlockSpec((1,) + w.shape[1:], lambda i: (layer, 0, 0))
    branch = pl.BlockSpec((tm, ya.shape[1]), lambda i: (i, 0))
    return pl.pallas_call(
        _mix_kernel,
        grid=(m_rows // tm,),
        in_specs=[
            pl.BlockSpec((tm, D_MODEL), lambda i: (i, 0)),
            branch, branch, branch,
            pl.BlockSpec((tm, D_MODEL), gate_blk(T_GA)),
            pl.BlockSpec((tm, D_MODEL), gate_blk(T_GB)),
            pl.BlockSpec((tm, D_MODEL), gate_blk(T_GC)),
            pl.BlockSpec((1, MOD_ROWS, D_MODEL), lambda i: ((i * tm) // seq if per_request else 0, 0, 0)),
            pl.BlockSpec((1, 1, D_MODEL), lambda i: (layer, 0, 0)),
            wspec(wa), wspec(wb), wspec(wc), wspec(wo), wspec(wr),
        ],
        out_specs=[
            pl.BlockSpec((tm, D_MODEL), lambda i: (i, 0)),
            pl.BlockSpec((tm, D_MODEL), lambda i: (i, 0)),
            pl.BlockSpec((tm, LANE), lambda i: (i, 0)),
            pl.BlockSpec((N_EXPERTS, tm), lambda i: (0, i)),
        ],
        out_shape=[
            jax.ShapeDtypeStruct((m_rows, D_MODEL), F32),
            jax.ShapeDtypeStruct((m_rows, D_MODEL), BF16),
            jax.ShapeDtypeStruct((m_rows, LANE), F32),
            jax.ShapeDtypeStruct((N_EXPERTS, m_rows), F32),
        ],
        compiler_params=_cparams("parallel"),
    )(x2d, ya, yb, yc, proj, proj, proj, mod, norm2_g.reshape(DEPTH, 1, D_MODEL), wa, wb, wc, wo, wr)


RANK_ROWS = 128


def _slot_onehot(rank_row, cap):
    slot = lax.broadcasted_iota(jnp.int32, (cap, 1), 0).astype(F32)
    return jnp.where(rank_row == slot, 1.0, 0.0).astype(BF16)


def _gather_kernel(aff_ref, afft_ref, h_ref, xs_ref, rank_ref, *, seq, cap):
    e = pl.program_id(1)
    a_row = afft_ref[pl.ds(e, 1), :]
    t_idx = lax.broadcasted_iota(jnp.int32, (1, seq), 1)
    lane = lax.broadcasted_iota(jnp.int32, (1, LANE), 1)
    rank = jnp.zeros((1, seq), F32)
    for blk in range(seq // RANK_ROWS):
        rows = slice(blk * RANK_ROWS, (blk + 1) * RANK_ROWS)
        a_col = jnp.sum(jnp.where(lane == e, aff_ref[rows, :], 0.0), axis=1, keepdims=True)
        s_idx = blk * RANK_ROWS + lax.broadcasted_iota(jnp.int32, (RANK_ROWS, 1), 0)
        beats = (a_col > a_row) | ((a_col == a_row) & (s_idx < t_idx))
        rank = rank + jnp.sum(jnp.where(beats, 1.0, 0.0), axis=0, keepdims=True)
    rank_ref[pl.ds(e, 1), :] = rank
    xs_ref[0, 0] = _bf(jnp.dot(_slot_onehot(rank, cap), h_ref[...], preferred_element_type=F32))


def _gather(aff, afft, h2, batch, seq, cap):
    return pl.pallas_call(
        functools.partial(_gather_kernel, seq=seq, cap=cap),
        grid=(batch, N_EXPERTS),
        in_specs=[
            pl.BlockSpec((seq, LANE), lambda r, e: (r, 0)),
            pl.BlockSpec((N_EXPERTS, seq), lambda r, e: (0, r)),
            pl.BlockSpec((seq, D_MODEL), lambda r, e: (r, 0)),
        ],
        out_specs=[
            pl.BlockSpec((1, 1, cap, D_MODEL), lambda r, e: (e, r, 0, 0)),
            pl.BlockSpec((N_EXPERTS, seq), lambda r, e: (0, r)),
        ],
        out_shape=[
            jax.ShapeDtypeStruct((N_EXPERTS, batch, cap, D_MODEL), BF16),
            jax.ShapeDtypeStruct((N_EXPERTS, batch * seq), F32),
        ],
        compiler_params=_cparams("parallel", "arbitrary"),
    )(aff, afft, h2)


def _ffn_kernel(x_ref, wg_ref, wu_ref, wd_ref, y_ref):
    x = x_ref[0]
    hid = _silu(_mm(x, wg_ref[0, 0])) * _mm(x, wu_ref[0, 0])
    y_ref[0] = _mm(hid, wd_ref[0, 0])


def _expert_ffn(xs, w_gate, w_up, w_down, layer):
    n_rows = xs.shape[1]
    return pl.pallas_call(
        _ffn_kernel,
        grid=(N_EXPERTS,),
        in_specs=[
            pl.BlockSpec((1, n_rows, D_MODEL), lambda e: (e, 0, 0)),
            pl.BlockSpec((1, 1, D_MODEL, D_EXPERT), lambda e: (layer, e, 0, 0)),
            pl.BlockSpec((1, 1, D_MODEL, D_EXPERT), lambda e: (layer, e, 0, 0)),
            pl.BlockSpec((1, 1, D_EXPERT, D_MODEL), lambda e: (layer, e, 0, 0)),
        ],
        out_specs=pl.BlockSpec((1, n_rows, D_MODEL), lambda e: (e, 0, 0)),
        out_shape=jax.ShapeDtypeStruct((N_EXPERTS, n_rows, D_MODEL), F32),
        compiler_params=_cparams("parallel"),
    )(xs, w_gate, w_up, w_down)


def _scatter_kernel(x_ref, aff_ref, rank_ref, ye_ref, mod_ref, fg_ref, o_ref, acc, *, cap, final):
    e = pl.program_id(1)

    @pl.when(e == 0)
    def _():
        acc[...] = jnp.zeros_like(acc)

    lane = lax.broadcasted_iota(jnp.int32, (1, LANE), 1)
    gate = jnp.sum(jnp.where(lane == e, aff_ref[...], 0.0), axis=1, keepdims=True)
    onehot = _slot_onehot(rank_ref[pl.ds(e, 1), :], cap)
    hi, lo = _split2(ye_ref[0, 0])
    tn = (((0,), (0,)), ((), ()))
    spread = (lax.dot_general(onehot, hi, tn, preferred_element_type=F32)
              + lax.dot_general(onehot, lo, tn, preferred_element_type=F32))
    acc[...] += gate * spread

    @pl.when(e == N_EXPERTS - 1)
    def _():
        x = x_ref[...] + mod_ref[0][5:6] * acc[...]
        o_ref[...] = _rms(x, fg_ref[...]) if final else x


def _scatter(x2d, aff, rank, ye, mod, final_g, batch, seq, cap, per_request, final):
    return pl.pallas_call(
        functools.partial(_scatter_kernel, cap=cap, final=final),
        grid=(batch, N_EXPERTS),
        in_specs=[
            pl.BlockSpec((seq, D_MODEL), lambda r, e: (r, 0)),
            pl.BlockSpec((seq, LANE), lambda r, e: (r, 0)),
            pl.BlockSpec((N_EXPERTS, seq), lambda r, e: (0, r)),
            pl.BlockSpec((1, 1, cap, D_MODEL), lambda r, e: (e, r, 0, 0)),
            pl.BlockSpec((1, MOD_ROWS, D_MODEL), lambda r, e: (r if per_request else 0, 0, 0)),
            pl.BlockSpec((1, D_MODEL), lambda r, e: (0, 0)),
        ],
        out_specs=pl.BlockSpec((seq, D_MODEL), lambda r, e: (r, 0)),
        out_shape=jax.ShapeDtypeStruct((batch * seq, D_MODEL), F32),
        scratch_shapes=[pltpu.VMEM((seq, D_MODEL), F32)],
        compiler_params=_cparams("parallel", "arbitrary"),
    )(x2d, aff, rank, ye, mod, final_g.reshape(1, D_MODEL))


def _reorder_w_in(w_in):
    a_main = w_in[:, :, 0:2048]
    a_gate = w_in[:, :, 2048:2064]
    b_main = w_in[:, :, 2064:3600]
    b_gate = w_in[:, :, 3600:3616]
    c_main = w_in[:, :, 3616:4384]
    merge = w_in[:, :, 4384:7456]

    def per_head(g, heads):
        g = g.reshape(DEPTH, D_MODEL, 4, heads).transpose(0, 1, 3, 2)
        g = jnp.pad(g, ((0, 0), (0, 0), (0, 0), (0, GATE_STRIDE - 4)))
        return g.reshape(DEPTH, D_MODEL, heads * GATE_STRIDE)

    gates = jnp.concatenate([per_head(a_gate, HEADS_A), per_head(b_gate, HEADS_B)], axis=-1)
    pad = jnp.zeros((DEPTH, D_MODEL, N_PROJ - T_GATES * LANE - gates.shape[-1]), w_in.dtype)
    return _bf(jnp.concatenate([merge, a_main, b_main, c_main, gates, pad], axis=-1))


def _lane_rows(rows):
    p = jnp.stack(rows, axis=-1)
    p = p.reshape(-1, len(rows))
    return jnp.broadcast_to(p[:, :, None], p.shape + (LANE,)).astype(F32)


def kernel(x_prompt, x_sample, cache_attn_k, cache_attn_v, state_delta, state_mlstm_c, state_mlstm_n, state_mlstm_m, c, c_ctx, ada_w, ada_b, norm1_g, norm2_g, w_in, conv_qkv_a, delta_a_log, delta_dt_bias, delta_norm_g, mlstm_i_bias, mlstm_f_bias, mlstm_norm_g, attn_sink, w_branch_a, w_branch_b, w_branch_c, w_out, w_router, w_expert_gate, w_expert_up, w_expert_down, final_norm_g):
    batch_p, seq_p, _ = x_prompt.shape
    batch_s, seq_s, _ = x_sample.shape
    past = cache_attn_k.shape[2]

    cond = jnp.concatenate([c_ctx[None, :], c, jnp.zeros((COND_ROWS - 1 - batch_s, D_MODEL), F32)], axis=0)
    mod = _modulation(cond, ada_w, ada_b).reshape(DEPTH, COND_ROWS, ADA_CHUNKS, D_MODEL)
    mod = jnp.pad(mod, ((0, 0), (0, 0), (0, MOD_ROWS - ADA_CHUNKS), (0, 0)))

    w_in_r = _reorder_w_in(w_in)
    wa, wb, wc, wo = _bf(w_branch_a), _bf(w_branch_b), _bf(w_branch_c), _bf(w_out)
    wr = _bf(jnp.pad(w_router, ((0, 0), (0, 0), (0, LANE - N_EXPERTS))))
    par_a = _lane_rows([delta_a_log[:, 0], delta_a_log[:, 1], delta_dt_bias[:, 0], delta_dt_bias[:, 1]])
    par_b = _lane_rows([mlstm_i_bias[:, 0], mlstm_i_bias[:, 1], mlstm_f_bias[:, 0], mlstm_f_bias[:, 1]])
    sink = jnp.broadcast_to(attn_sink[:, :, None], (DEPTH, HEADS_C, LANE)).astype(F32)
    cache_k = cache_attn_k.reshape(batch_s, DEPTH, past, KV_HEADS_C * HEAD_DIM_C)
    cache_v = cache_attn_v.reshape(batch_s, DEPTH, past, KV_HEADS_C * HEAD_DIM_C)
    state_n = state_mlstm_n.reshape(batch_s, DEPTH, 2, HEADS_B, 1, DK_B)
    state_m = jnp.broadcast_to(state_mlstm_m[..., None, None], (batch_s, DEPTH, 2, HEADS_B, 1, LANE)).astype(F32)
    cos, sin = _rope_tables(seq_s)

    def layer(x2d, l, batch, seq, latent):
        mod_l = mod[l, 1:1 + batch] if latent else mod[l, 0:1]
        cap = EC_CAPACITY * seq // N_EXPERTS
        proj = _in_proj(x2d, mod_l, norm1_g, w_in_r, l, seq, latent)
        ya, d_new = _delta_mixer(proj, conv_qkv_a, par_a, delta_norm_g, l, batch, seq,
                                 state_delta if latent else None)
        yb, b_new = _mlstm_mixer(proj, par_b, mlstm_norm_g, l, batch, seq,
                                 (state_mlstm_c, state_n, state_m) if latent else None)
        if latent:
            yc = _latent_attention(proj, cache_k, cache_v, cos, sin, sink, l, batch, seq)
        else:
            yc = _ctx_attention(proj, sink, l, batch, seq)
        x1, h2, aff, afft = _mix(x2d, ya, yb, yc, proj, mod_l, norm2_g, wa, wb, wc, wo, wr, l, seq, latent)
        xs, rank = _gather(aff, afft, h2, batch, seq, cap)
        ye = _expert_ffn(xs.reshape(N_EXPERTS, batch * cap, D_MODEL), w_expert_gate, w_expert_up, w_expert_down, l)
        x2 = _scatter(x1, aff, rank, ye.reshape(N_EXPERTS, batch, cap, D_MODEL), mod_l, final_norm_g,
                      batch, seq, cap, latent, l == DEPTH - 1)
        return x2, proj, d_new, b_new

    xp = x_prompt.reshape(batch_p * seq_p, D_MODEL)
    ks, vs, ds, cs, ns, ms = [], [], [], [], [], []
    for l in range(DEPTH):
        xp, proj, d_new, (c_new, n_new, m_new) = layer(xp, l, batch_p, seq_p, False)
        ks.append(proj[:, T_KC * LANE:(T_KC + 1) * LANE].reshape(batch_p, seq_p, KV_HEADS_C, HEAD_DIM_C))
        vs.append(proj[:, T_VC * LANE:(T_VC + 1) * LANE].reshape(batch_p, seq_p, KV_HEADS_C, HEAD_DIM_C))
        ds.append(d_new)
        cs.append(c_new)
        ns.append(n_new)
        ms.append(m_new)

    xs = x_sample.reshape(batch_s * seq_s, D_MODEL)
    for l in range(DEPTH):
        xs, _, _, _ = layer(xs, l, batch_s, seq_s, True)

    stack = lambda parts: jnp.stack(parts, axis=1)
    return (xp.reshape(batch_p, seq_p, D_MODEL), xs.reshape(batch_s, seq_s, D_MODEL),
            stack(ks), stack(vs), stack(ds), stack(cs), stack(ns), stack(ms))
```

```python
import functools

import jax
import jax.numpy as jnp
from jax import lax
from jax.experimental import pallas as pl
from jax.experimental.pallas import tpu as pltpu

F32 = jnp.float32
BF16 = jnp.bfloat16

D_MODEL = 1024
DEPTH = 2
GRID_W = 64
EPS = 1e-6
HEADS_A = 4
DK_A = 128
DV_A = 128
CONV_K = 5
CHUNK = 64
HEADS_B = 4
DK_B = 64
DV_B = 128
HEADS_C = 8
KV_HEADS_C = 2
HEAD_DIM_C = 64
GROUP_C = HEADS_C // KV_HEADS_C
WINDOW = 128
Q_BLOCK = 128
ROPE_BASE = 10000.0
N_EXPERTS = 16
D_EXPERT = 512
EC_CAPACITY = 2
ADA_CHUNKS = 6

LANE = 128
MOD_ROWS = 8
COND_ROWS = 16

T_GA, T_GB, T_GC = 0, 8, 16
T_QA, T_KA, T_VA, T_ZA = 24, 28, 32, 36
T_QB, T_KB, T_VB, T_OB = 40, 42, 44, 48
T_QC, T_KC, T_VC = 52, 56, 57
T_GATES = 58
N_TILES = 60
N_PROJ = N_TILES * LANE
GATE_STRIDE = 8
GATE_B_OFF = HEADS_A * GATE_STRIDE

VMEM_LIMIT = 48 * 1024 * 1024


def _cparams(*sem):
    return pltpu.CompilerParams(dimension_semantics=sem, vmem_limit_bytes=VMEM_LIMIT)


def _bf(x):
    return x.astype(BF16)


def _mm(a, b):
    return jnp.dot(_bf(a), _bf(b), preferred_element_type=F32)


def _mm_nt(a, b):
    return lax.dot_general(_bf(a), _bf(b), (((1,), (1,)), ((), ())), preferred_element_type=F32)


def _mm_tn(a, b):
    return lax.dot_general(_bf(a), _bf(b), (((0,), (0,)), ((), ())), preferred_element_type=F32)


def _split2(x):
    hi = _bf(x)
    return hi, _bf(x - hi.astype(F32))


def _split3(x):
    hi = _bf(x)
    r = x - hi.astype(F32)
    mid = _bf(r)
    return hi, mid, _bf(r - mid.astype(F32))


def _mm_sel(sel, x):
    hi, mid, lo = _split3(x)
    d = functools.partial(jnp.dot, preferred_element_type=F32)
    return (d(sel, hi) + d(sel, mid)) + d(sel, lo)


def _mm_hi(a, b):
    ah, al = _split2(a)
    bh, bl = _split2(b)
    d = functools.partial(jnp.dot, preferred_element_type=F32)
    return d(ah, bh) + (d(ah, bl) + d(al, bh))


def _sigmoid(x):
    return 1.0 / (1.0 + jnp.exp(-x))


def _silu(x):
    return x * _sigmoid(x)


def _softplus(x):
    return jnp.maximum(x, 0.0) + jnp.log(1.0 + jnp.exp(-jnp.abs(x)))


def _rms(x, g):
    return x * lax.rsqrt(jnp.mean(x * x, axis=-1, keepdims=True) + EPS) * g


def _chunk_masks(backward):
    ri = lax.broadcasted_iota(jnp.int32, (CHUNK, CHUNK), 0)
    ci = lax.broadcasted_iota(jnp.int32, (CHUNK, CHUNK), 1)
    if backward:
        return ri <= ci, ri < ci, ri == ci
    return ri >= ci, ri > ci, ri == ci


def _mod_kernel(c_ref, w_ref, b_ref, o_ref):
    o_ref[0] = _mm(_silu(c_ref[...]), w_ref[0]) + b_ref[0]


def _modulation(cond, ada_w, ada_b):
    n_out = ADA_CHUNKS * D_MODEL
    tn = 512
    return pl.pallas_call(
        _mod_kernel,
        grid=(DEPTH, n_out // tn),
        in_specs=[
            pl.BlockSpec((COND_ROWS, D_MODEL), lambda l, j: (0, 0)),
            pl.BlockSpec((1, D_MODEL, tn), lambda l, j: (l, 0, j)),
            pl.BlockSpec((1, 1, tn), lambda l, j: (l, 0, j)),
        ],
        out_specs=pl.BlockSpec((1, COND_ROWS, tn), lambda l, j: (l, 0, j)),
        out_shape=jax.ShapeDtypeStruct((DEPTH, COND_ROWS, n_out), F32),
        compiler_params=_cparams("parallel", "parallel"),
    )(cond, ada_w, ada_b.reshape(DEPTH, 1, n_out))


def _in_proj_kernel(x_ref, mod_ref, g_ref, w_ref, o_ref, h_scr):
    @pl.when(pl.program_id(1) == 0)
    def _():
        m = mod_ref[0]
        h_scr[...] = _bf(_rms(x_ref[...], g_ref[0]) * (1.0 + m[1:2]) + m[0:1])

    o_ref[...] = jnp.dot(h_scr[...], w_ref[0], preferred_element_type=F32)


def _in_proj(x2d, mod, norm_g, w_in, layer, seq, per_request):
    m_rows = x2d.shape[0]
    tm = min(1024, m_rows)
    tn = 768
    if per_request:
        assert seq % tm == 0
    return pl.pallas_call(
        _in_proj_kernel,
        grid=(m_rows // tm, N_PROJ // tn),
        in_specs=[
            pl.BlockSpec((tm, D_MODEL), lambda i, j: (i, 0)),
            pl.BlockSpec((1, MOD_ROWS, D_MODEL), lambda i, j: ((i * tm) // seq if per_request else 0, 0, 0)),
            pl.BlockSpec((1, 1, D_MODEL), lambda i, j: (layer, 0, 0)),
            pl.BlockSpec((1, D_MODEL, tn), lambda i, j: (layer, 0, j)),
        ],
        out_specs=pl.BlockSpec((tm, tn), lambda i, j: (i, j)),
        out_shape=jax.ShapeDtypeStruct((m_rows, N_PROJ), F32),
        scratch_shapes=[pltpu.VMEM((tm, D_MODEL), BF16)],
        compiler_params=_cparams("parallel", "arbitrary"),
    )(x2d, mod, norm_g.reshape(DEPTH, 1, D_MODEL), w_in)


PREP_CHUNKS = 2
INV_GROUP = 8


def _gate_column(gt, lane_index):
    lane = lax.broadcasted_iota(jnp.int32, (1, LANE), 1)
    return jnp.sum(jnp.where(lane == lane_index, gt, 0.0), axis=1, keepdims=True)


def _delta_kernel(q_ref, k_ref, v_ref, z_ref, gt_ref, cq_ref, ck_ref, cv_ref, par_ref, ng_ref, *rest,
                  seq, has_init, emit_state):
    rest = list(rest)
    s0_ref = rest.pop(0) if has_init else None
    y_ref = rest.pop(0)
    st_ref = rest.pop(0) if emit_state else None
    qs, ks, vs, gates, s_scr, o_scr, u_scr, wq_scr, akd_scr, dk_scr, pw_scr, inv_scr, rhs_scr = rest
    head = pl.program_id(1)
    n_chunks = seq // CHUNK
    rows_t = lax.broadcasted_iota(jnp.int32, (seq, 1), 0)

    def conv_silu(x_ref, w_ref):
        x = x_ref[...]
        w = w_ref[0]
        acc = x * w[CONV_K // 2:CONV_K // 2 + 1, :]
        for j in range(CONV_K):
            s = j - CONV_K // 2
            if s == 0:
                continue
            shifted = pltpu.roll(x, (-s) % seq, 0)
            ok = (rows_t + s >= 0) & (rows_t + s < seq)
            acc = acc + jnp.where(ok, shifted, 0.0) * w[j:j + 1, :]
        return _silu(acc)

    def l2n(x):
        return x * lax.rsqrt(jnp.sum(x * x, axis=-1, keepdims=True) + EPS)

    qs[...] = l2n(conv_silu(q_ref, cq_ref)) * (DK_A ** -0.5)
    ks[...] = l2n(conv_silu(k_ref, ck_ref))
    vs[...] = conv_silu(v_ref, cv_ref)

    gt = gt_ref[...]
    par = par_ref[0]
    base = head * GATE_STRIDE
    for d in range(2):
        gates[d] = _sigmoid(_gate_column(gt, base + d))
        gates[2 + d] = -jnp.exp(par[d:d + 1, 0:1]) * _softplus(_gate_column(gt, base + 2 + d) + par[2 + d:3 + d, 0:1])
        s_scr[d] = s0_ref[0, 0, d, 0] if has_init else jnp.zeros((DK_A, DV_A), F32)

    def chunk_rows(c):
        return pl.ds(pl.multiple_of(c * CHUNK, CHUNK), CHUNK)

    n_prob = 2 * n_chunks
    ri = lax.broadcasted_iota(jnp.int32, (CHUNK, CHUNK), 0)
    ci = lax.broadcasted_iota(jnp.int32, (CHUNK, CHUNK), 1)
    dot = functools.partial(jnp.dot, preferred_element_type=F32)

    def setup_body(i, carry):
        loaded = []
        for cc in range(PREP_CHUNKS):
            c = i * PREP_CHUNKS + cc
            rows = chunk_rows(c)
            loaded.append((c, qs[rows, :], ks[rows, :], vs[rows, :],
                           [(gates[d, rows, :], gates[2 + d, rows, :]) for d in range(2)]))
        results = []
        for c, q, k, v, gate_cols in loaded:
            kk = _mm_nt(k, k)
            qk = _mm_nt(q, k)
            for d in range(2):
                bt, g = gate_cols[d]
                incl, strict, _ = _chunk_masks(d == 1)
                rhs = jnp.concatenate([jnp.where(strict, g, 0.0), jnp.broadcast_to(g, (CHUNK, CHUNK))], axis=1)
                cs = _mm_sel(jnp.where(incl, 1.0, 0.0).astype(BF16), rhs)
                gc = cs[:, CHUNK:CHUNK + 1]
                g_last = gc[0:1, :] if d == 1 else gc[CHUNK - 1:CHUNK, :]
                egc = jnp.exp(gc)
                dec = jnp.where(incl, jnp.exp(cs[:, :CHUNK]), 0.0)
                low = jnp.where(strict, bt * kk * dec, 0.0)
                rhs2 = jnp.concatenate([v * bt, k * (bt * egc)], axis=1)
                akd = _bf(jnp.concatenate([qk * dec, (k * jnp.exp(g_last - gc)).T], axis=0))
                decay = jnp.broadcast_to(jnp.exp(g_last), (1, DV_A))
                results.append((d * n_chunks + c, low, rhs2, akd, _bf(q * egc), decay))
        for p, low, rhs2, akd, qd, decay in results:
            pw_scr[p] = low
            inv_scr[p] = jnp.where(ri == ci, 1.0, 0.0) - low
            rhs_scr[p] = rhs2
            akd_scr[p] = akd
            wq_scr[p, CHUNK:, :] = qd
            dk_scr[p] = decay
        return carry

    lax.fori_loop(0, n_chunks // PREP_CHUNKS, setup_body, 0)

    def doubling_pass(do_product, do_square):
        def body(i, carry):
            loaded = []
            for j in range(INV_GROUP):
                p = i * INV_GROUP + j
                loaded.append((p, pw_scr[p], inv_scr[p] if do_product else None))
            results = []
            for p, power, inv in loaded:
                ph, pl_ = _split2(power)
                new_inv = None
                if do_product:
                    ih, il = _split2(inv)
                    new_inv = inv + (dot(ih, ph) + (dot(ih, pl_) + dot(il, ph)))
                new_power = dot(ph, ph) + (dot(ph, pl_) + dot(pl_, ph)) if do_square else None
                results.append((p, new_power, new_inv))
            for p, new_power, new_inv in results:
                if do_square:
                    pw_scr[p] = new_power
                if do_product:
                    inv_scr[p] = new_inv
            return carry

        lax.fori_loop(0, n_prob // INV_GROUP, body, 0)

    doubling_pass(False, True)
    span = 4
    while span < CHUNK:
        doubling_pass(True, True)
        span *= 2
    doubling_pass(True, False)

    def solve_body(i, carry):
        loaded = []
        for j in range(INV_GROUP):
            p = i * INV_GROUP + j
            loaded.append((p, inv_scr[p], rhs_scr[p]))
        results = [(p, _mm_hi(inv, rhs2)) for p, inv, rhs2 in loaded]
        for p, sol in results:
            u_scr[p] = sol[:, :DV_A]
            wq_scr[p, :CHUNK, :] = _bf(sol[:, DV_A:])
        return carry

    lax.fori_loop(0, n_prob // INV_GROUP, solve_body, 0)

    def scan_body(i, carry):
        loaded = []
        for d in range(2):
            c = n_chunks - 1 - i if d == 1 else i
            p = d * n_chunks + c
            loaded.append((c, s_scr[d], u_scr[p], wq_scr[p], akd_scr[p], dk_scr[p]))
        results = []
        for c, state, u, wq, akd, decay in loaded:
            ws = dot(wq, _bf(state))
            av = dot(akd, _bf(u - ws[:CHUNK]))
            results.append((c, ws[CHUNK:] + av[:CHUNK], decay * state + av[CHUNK:]))
        for d, (c, o, state) in enumerate(results):
            o_scr[d, chunk_rows(c), :] = o
            s_scr[d] = state
        return carry

    lax.fori_loop(0, n_chunks, scan_body, 0)

    z = z_ref[...]
    y_ref[...] = _rms(o_scr[0] + o_scr[1], ng_ref[0]) * _silu(z)
    if emit_state:
        st_ref[0, 0, 0] = s_scr[0]
        st_ref[0, 1, 0] = s_scr[1]


def _delta_mixer(proj, conv_w, par, norm_g, layer, batch, seq, state0):
    has_init = state0 is not None
    emit_state = not has_init
    col = lambda tile: (lambda r, h: (r, tile + h))
    cw = lambda part: (lambda r, h: (layer, 0, part * HEADS_A + h))
    in_specs = [
        pl.BlockSpec((seq, LANE), col(T_QA)),
        pl.BlockSpec((seq, LANE), col(T_KA)),
        pl.BlockSpec((seq, LANE), col(T_VA)),
        pl.BlockSpec((seq, LANE), col(T_ZA)),
        pl.BlockSpec((seq, LANE), lambda r, h: (r, T_GATES)),
        pl.BlockSpec((1, CONV_K, LANE), cw(0)),
        pl.BlockSpec((1, CONV_K, LANE), cw(1)),
        pl.BlockSpec((1, CONV_K, LANE), cw(2)),
        pl.BlockSpec((1, 4, LANE), lambda r, h: (layer * HEADS_A + h, 0, 0)),
        pl.BlockSpec((1, 1, DV_A), lambda r, h: (layer, 0, 0)),
    ]
    args = [proj, proj, proj, proj, proj, conv_w, conv_w, conv_w, par, norm_g.reshape(DEPTH, 1, DV_A)]
    if has_init:
        in_specs.append(pl.BlockSpec((1, 1, 2, 1, DK_A, DV_A), lambda r, h: (r, layer, 0, h, 0, 0)))
        args.append(state0)
    out_specs = [pl.BlockSpec((seq, DV_A), lambda r, h: (r, h))]
    out_shape = [jax.ShapeDtypeStruct((batch * seq, HEADS_A * DV_A), F32)]
    if emit_state:
        out_specs.append(pl.BlockSpec((1, 2, 1, DK_A, DV_A), lambda r, h: (r, 0, h, 0, 0)))
        out_shape.append(jax.ShapeDtypeStruct((batch, 2, HEADS_A, DK_A, DV_A), F32))
    n_prob = 2 * (seq // CHUNK)
    outs = pl.pallas_call(
        functools.partial(_delta_kernel, seq=seq, has_init=has_init, emit_state=emit_state),
        grid=(batch, HEADS_A),
        in_specs=in_specs,
        out_specs=out_specs,
        out_shape=out_shape,
        scratch_shapes=[
            pltpu.VMEM((seq, DK_A), F32), pltpu.VMEM((seq, DK_A), F32), pltpu.VMEM((seq, DV_A), F32),
            pltpu.VMEM((4, seq, 1), F32), pltpu.VMEM((2, DK_A, DV_A), F32), pltpu.VMEM((2, seq, DV_A), F32),
            pltpu.VMEM((n_prob, CHUNK, DV_A), F32), pltpu.VMEM((n_prob, 2 * CHUNK, DK_A), BF16),
            pltpu.VMEM((n_prob, CHUNK + DK_A, CHUNK), BF16), pltpu.VMEM((n_prob, 1, DV_A), F32),
            pltpu.VMEM((n_prob, CHUNK, CHUNK), F32), pltpu.VMEM((n_prob, CHUNK, CHUNK), F32),
            pltpu.VMEM((n_prob, CHUNK, DV_A + DK_A), F32),
        ],
        compiler_params=_cparams("parallel", "parallel"),
    )(*args)
    return (outs[0], outs[1]) if emit_state else (outs[0], None)


HEADS_PER_STEP_B = 2


def _mlstm_kernel(q_ref, k_ref, v_ref, og_ref, gt_ref, par_ref, ng_ref, *rest, seq, has_init, emit_state):
    rest = list(rest)
    if has_init:
        c0_ref, n0_ref, m0_ref = rest[:3]
        rest = rest[3:]
    y_ref = rest.pop(0)
    if emit_state:
        co_ref, no_ref, mo_ref = rest[:3]
        rest = rest[3:]
    gates, c_scr, n_scr, m_scr, h_scr = rest
    pair = pl.program_id(1)
    n_chunks = seq // CHUNK
    gt = gt_ref[...]

    for j in range(HEADS_PER_STEP_B):
        par = par_ref[j]
        base = GATE_B_OFF + (pair * HEADS_PER_STEP_B + j) * GATE_STRIDE
        for d in range(2):
            gates[j * 4 + d] = _gate_column(gt, base + d) + par[d:d + 1, 0:1]
            gates[j * 4 + 2 + d] = -_softplus(-(_gate_column(gt, base + 2 + d) + par[2 + d:3 + d, 0:1]))
            idx = j * 2 + d
            if has_init:
                c_scr[idx] = c0_ref[0, 0, d, j]
                n_scr[idx] = n0_ref[0, 0, d, j]
                m_scr[idx] = m0_ref[0, 0, d, j][:, 0:1]
            else:
                c_scr[idx] = jnp.zeros((DK_B, DV_B), F32)
                n_scr[idx] = jnp.zeros((1, DK_B), F32)
                m_scr[idx] = jnp.zeros((1, 1), F32)

    def chunk(c, d, j):
        incl, strict, diag = _chunk_masks(d == 1)
        idx = j * 2 + d
        rows = pl.ds(pl.multiple_of(c * CHUNK, CHUNK), CHUNK)
        q = q_ref[rows, j * DK_B:(j + 1) * DK_B]
        k = k_ref[rows, j * DK_B:(j + 1) * DK_B] * (DK_B ** -0.5)
        v = v_ref[rows, j * DV_B:(j + 1) * DV_B]
        ig = gates[j * 4 + d, rows, :]
        lf = gates[j * 4 + 2 + d, rows, :]
        rhs = jnp.concatenate(
            [jnp.where(strict, lf, 0.0) + jnp.where(diag, ig, 0.0), jnp.broadcast_to(lf, (CHUNK, CHUNK))], axis=1)
        cs = _mm_sel(jnp.where(incl, 1.0, 0.0).astype(BF16), rhs)
        bc = cs[:, CHUNK:CHUNK + 1]
        b_last = bc[0:1, :] if d == 1 else bc[CHUNK - 1:CHUNK, :]
        d_log = jnp.where(incl, cs[:, :CHUNK], -jnp.inf)
        d_max = jnp.max(d_log, axis=1, keepdims=True)
        tok = b_last - bc + ig
        tok_max = jnp.max(tok, axis=0, keepdims=True)
        qk = _mm_nt(q, k)
        cmat = c_scr[idx]
        nvec = n_scr[idx]
        m_prev = m_scr[idx]
        m_t = jnp.maximum(bc + m_prev, d_max)
        w_inter = jnp.exp(bc + m_prev - m_t)
        p = jnp.exp(d_log - m_t) * qk
        num = w_inter * _mm(q, cmat) + _mm(p, v)
        den = w_inter * jnp.sum(q * nvec, axis=1, keepdims=True) + jnp.sum(p, axis=1, keepdims=True)
        h_scr[idx, rows, :] = num / jnp.maximum(jnp.abs(den), jnp.exp(-m_t))
        m_new = jnp.maximum(b_last + m_prev, tok_max)
        w_prev = jnp.exp(b_last + m_prev - m_new)
        kw = k * jnp.exp(tok - m_new)
        c_scr[idx] = w_prev * cmat + _mm_tn(kw, v)
        n_scr[idx] = w_prev * nvec + jnp.sum(kw, axis=0, keepdims=True)
        m_scr[idx] = m_new

    def body(c, carry):
        for j in range(HEADS_PER_STEP_B):
            chunk(c, 0, j)
            chunk(n_chunks - 1 - c, 1, j)
        return carry

    lax.fori_loop(0, n_chunks, body, 0)

    og = og_ref[...]
    for j in range(HEADS_PER_STEP_B):
        h = h_scr[j * 2] + h_scr[j * 2 + 1]
        y_ref[:, j * DV_B:(j + 1) * DV_B] = _rms(h, ng_ref[0]) * _sigmoid(og[:, j * DV_B:(j + 1) * DV_B])
        if emit_state:
            for d in range(2):
                co_ref[0, d, j] = c_scr[j * 2 + d]
                no_ref[0, d, j] = n_scr[j * 2 + d]
                mo_ref[0, d, j] = jnp.broadcast_to(m_scr[j * 2 + d], (1, LANE))


def _mlstm_mixer(proj, par, norm_g, layer, batch, seq, state0):
    has_init = state0 is not None
    emit_state = not has_init
    hp = HEADS_PER_STEP_B
    n_pairs = HEADS_B // hp
    in_specs = [
        pl.BlockSpec((seq, hp * DK_B), lambda r, p: (r, T_QB + p)),
        pl.BlockSpec((seq, hp * DK_B), lambda r, p: (r, T_KB + p)),
        pl.BlockSpec((seq, hp * DV_B), lambda r, p: (r, T_VB // hp + p)),
        pl.BlockSpec((seq, hp * DV_B), lambda r, p: (r, T_OB // hp + p)),
        pl.BlockSpec((seq, LANE), lambda r, p: (r, T_GATES)),
        pl.BlockSpec((hp, 4, LANE), lambda r, p: (layer * n_pairs + p, 0, 0)),
        pl.BlockSpec((1, 1, DV_B), lambda r, p: (layer, 0, 0)),
    ]
    args = [proj, proj, proj, proj, proj, par, norm_g.reshape(DEPTH, 1, DV_B)]
    if has_init:
        c0, n0, m0 = state0
        in_specs += [
            pl.BlockSpec((1, 1, 2, hp, DK_B, DV_B), lambda r, p: (r, layer, 0, p, 0, 0)),
            pl.BlockSpec((1, 1, 2, hp, 1, DK_B), lambda r, p: (r, layer, 0, p, 0, 0)),
            pl.BlockSpec((1, 1, 2, hp, 1, LANE), lambda r, p: (r, layer, 0, p, 0, 0)),
        ]
        args += [c0, n0, m0]
    out_specs = [pl.BlockSpec((seq, hp * DV_B), lambda r, p: (r, p))]
    out_shape = [jax.ShapeDtypeStruct((batch * seq, HEADS_B * DV_B), F32)]
    if emit_state:
        out_specs += [
            pl.BlockSpec((1, 2, hp, DK_B, DV_B), lambda r, p: (r, 0, p, 0, 0)),
            pl.BlockSpec((1, 2, hp, 1, DK_B), lambda r, p: (r, 0, p, 0, 0)),
            pl.BlockSpec((1, 2, hp, 1, LANE), lambda r, p: (r, 0, p, 0, 0)),
        ]
        out_shape += [
            jax.ShapeDtypeStruct((batch, 2, HEADS_B, DK_B, DV_B), F32),
            jax.ShapeDtypeStruct((batch, 2, HEADS_B, 1, DK_B), F32),
            jax.ShapeDtypeStruct((batch, 2, HEADS_B, 1, LANE), F32),
        ]
    outs = pl.pallas_call(
        functools.partial(_mlstm_kernel, seq=seq, has_init=has_init, emit_state=emit_state),
        grid=(batch, n_pairs),
        in_specs=in_specs,
        out_specs=out_specs,
        out_shape=out_shape,
        scratch_shapes=[
            pltpu.VMEM((4 * hp, seq, 1), F32), pltpu.VMEM((2 * hp, DK_B, DV_B), F32),
            pltpu.VMEM((2 * hp, 1, DK_B), F32), pltpu.VMEM((2 * hp, 1, 1), F32),
            pltpu.VMEM((2 * hp, seq, DV_B), F32),
        ],
        compiler_params=_cparams("parallel", "parallel"),
    )(*args)
    if emit_state:
        return outs[0], (outs[1], outs[2][:, :, :, 0, :], outs[3][:, :, :, 0, 0])
    return outs[0], None


def _stack_heads(q, kv_head):
    return jnp.concatenate(
        [q[:, (kv_head * GROUP_C + g) * HEAD_DIM_C:(kv_head * GROUP_C + g + 1) * HEAD_DIM_C] for g in range(GROUP_C)],
        axis=0)


def _sink_column(sink, kv_head, rows):
    return jnp.concatenate(
        [jnp.broadcast_to(sink[kv_head * GROUP_C + g:kv_head * GROUP_C + g + 1, 0:1], (rows, 1))
         for g in range(GROUP_C)], axis=0)


def _unstack_heads(per_kv, rows):
    return jnp.concatenate(
        [o[g * rows:(g + 1) * rows, :] for o in per_kv for g in range(GROUP_C)], axis=1)


def _ctx_attn_kernel(q_ref, k_ref, v_ref, sink_ref, o_ref, *, seq):
    q = q_ref[...]
    k = k_ref[...]
    v = v_ref[...]
    sink = sink_ref[0]
    scale = HEAD_DIM_C ** -0.5
    outs = []
    for kv in range(KV_HEADS_C):
        lanes = slice(kv * HEAD_DIM_C, (kv + 1) * HEAD_DIM_C)
        s = _mm_nt(_stack_heads(q, kv), k[:, lanes]) * scale
        sk = _sink_column(sink, kv, seq)
        m = jnp.maximum(jnp.max(s, axis=1, keepdims=True), sk)
        e = jnp.exp(s - m)
        den = jnp.sum(e, axis=1, keepdims=True) + jnp.exp(sk - m)
        outs.append(_mm(e / den, v[:, lanes]))
    o_ref[...] = _unstack_heads(outs, seq)


def _ctx_attention(proj, sink, layer, batch, seq):
    width = HEADS_C * HEAD_DIM_C
    return pl.pallas_call(
        functools.partial(_ctx_attn_kernel, seq=seq),
        grid=(batch,),
        in_specs=[
            pl.BlockSpec((seq, width), lambda r: (r, T_QC * LANE // width)),
            pl.BlockSpec((seq, LANE), lambda r: (r, T_KC)),
            pl.BlockSpec((seq, LANE), lambda r: (r, T_VC)),
            pl.BlockSpec((1, HEADS_C, LANE), lambda r: (layer, 0, 0)),
        ],
        out_specs=pl.BlockSpec((seq, width), lambda r: (r, 0)),
        out_shape=jax.ShapeDtypeStruct((batch * seq, width), F32),
        compiler_params=_cparams("parallel"),
    )(proj, proj, proj, sink)


def _rope(x, cos, sin):
    quarter = HEAD_DIM_C // 4
    lane = lax.broadcasted_iota(jnp.int32, (1, LANE), 1)
    first = (lane % (2 * quarter)) < quarter
    partner = jnp.where(first, -pltpu.roll(x, LANE - quarter, 1), pltpu.roll(x, quarter, 1))
    return x * cos + partner * sin


def _latent_attn_kernel(q_ref, k_ref, v_ref, ck_ref, cv_ref, cq_ref, sq_ref, cos_ref, sin_ref, sink_ref, o_ref, *, seq):
    blk = pl.program_id(1)
    span = Q_BLOCK + 2 * WINDOW
    start = blk * Q_BLOCK
    k_start = pl.multiple_of(jnp.clip(start - WINDOW, 0, seq - span), Q_BLOCK)
    win = pl.ds(k_start, span)
    cq = cq_ref[...]
    sq = sq_ref[...]
    q = jnp.concatenate(
        [_rope(q_ref[:, s * LANE:(s + 1) * LANE], cq, sq) for s in range(HEADS_C * HEAD_DIM_C // LANE)], axis=1)
    k = _rope(k_ref[win, :], cos_ref[win, :], sin_ref[win, :])
    v = v_ref[win, :]
    ck = ck_ref[0, 0]
    cv = cv_ref[0, 0]
    sink = sink_ref[0]
    scale = HEAD_DIM_C ** -0.5
    q_pos = start + lax.broadcasted_iota(jnp.int32, (GROUP_C * Q_BLOCK, 1), 0) % Q_BLOCK
    k_pos = k_start + lax.broadcasted_iota(jnp.int32, (1, span), 1)
    valid = jnp.abs(q_pos - k_pos) <= WINDOW
    outs = []
    for kv in range(KV_HEADS_C):
        lanes = slice(kv * HEAD_DIM_C, (kv + 1) * HEAD_DIM_C)
        qs = _stack_heads(q, kv)
        sl = jnp.where(valid, _mm_nt(qs, k[:, lanes]) * scale, -jnp.inf)
        sc = _mm_nt(qs, ck[:, lanes]) * scale
        sk = _sink_column(sink, kv, Q_BLOCK)
        m = jnp.maximum(jnp.maximum(jnp.max(sl, axis=1, keepdims=True), jnp.max(sc, axis=1, keepdims=True)), sk)
        el = jnp.exp(sl - m)
        ec = jnp.exp(sc - m)
        den = jnp.sum(el, axis=1, keepdims=True) + jnp.sum(ec, axis=1, keepdims=True) + jnp.exp(sk - m)
        outs.append(_mm(el / den, v[:, lanes]) + _mm(ec / den, cv[:, lanes]))
    o_ref[...] = _unstack_heads(outs, Q_BLOCK)


def _latent_attention(proj, cache_k, cache_v, cos, sin, sink, layer, batch, seq):
    width = HEADS_C * HEAD_DIM_C
    n_blk = seq // Q_BLOCK
    past = cache_k.shape[2]
    return pl.pallas_call(
        functools.partial(_latent_attn_kernel, seq=seq),
        grid=(batch, n_blk),
        in_specs=[
            pl.BlockSpec((Q_BLOCK, width), lambda r, i: (r * n_blk + i, T_QC * LANE // width)),
            pl.BlockSpec((seq, LANE), lambda r, i: (r, T_KC)),
            pl.BlockSpec((seq, LANE), lambda r, i: (r, T_VC)),
            pl.BlockSpec((1, 1, past, LANE), lambda r, i: (r, layer, 0, 0)),
            pl.BlockSpec((1, 1, past, LANE), lambda r, i: (r, layer, 0, 0)),
            pl.BlockSpec((Q_BLOCK, LANE), lambda r, i: (i, 0)),
            pl.BlockSpec((Q_BLOCK, LANE), lambda r, i: (i, 0)),
            pl.BlockSpec((seq, LANE), lambda r, i: (0, 0)),
            pl.BlockSpec((seq, LANE), lambda r, i: (0, 0)),
            pl.BlockSpec((1, HEADS_C, LANE), lambda r, i: (layer, 0, 0)),
        ],
        out_specs=pl.BlockSpec((Q_BLOCK, width), lambda r, i: (r * n_blk + i, 0)),
        out_shape=jax.ShapeDtypeStruct((batch * seq, width), F32),
        compiler_params=_cparams("parallel", "parallel"),
    )(proj, proj, proj, cache_k, cache_v, cos, sin, cos, sin, sink)


def _rope_tables(seq):
    quarter = HEAD_DIM_C // 4
    pos = jnp.arange(seq)
    row = (pos // GRID_W).astype(F32)
    col = (pos % GRID_W).astype(F32)
    inv = jnp.power(ROPE_BASE, -jnp.arange(quarter, dtype=F32) / quarter)
    ang_row = row[:, None] * inv[None, :]
    ang_col = col[:, None] * inv[None, :]
    ang = jnp.concatenate([ang_row, ang_row, ang_col, ang_col], axis=1)
    ang = jnp.concatenate([ang] * (LANE // HEAD_DIM_C), axis=1)
    return jnp.cos(ang), jnp.sin(ang)


def _mix_kernel(x_ref, ya_ref, yb_ref, yc_ref, ga_ref, gb_ref, gc_ref, mod_ref, n2_ref,
                wa_ref, wb_ref, wc_ref, wo_ref, wr_ref, xo_ref, h_ref, aff_ref, afft_ref):
    m = mod_ref[0]
    mixed = (_sigmoid(ga_ref[...]) * _mm(ya_ref[...], wa_ref[0])
             + _sigmoid(gb_ref[...]) * _mm(yb_ref[...], wb_ref[0])
             + _sigmoid(gc_ref[...]) * _mm(yc_ref[...], wc_ref[0]))
    x = x_ref[...] + m[2:3] * _mm(mixed, wo_ref[0])
    xo_ref[...] = x
    h = _bf(_rms(x, n2_ref[0]) * (1.0 + m[4:5]) + m[3:4])
    h_ref[...] = h
    logits = jnp.dot(h, wr_ref[0], preferred_element_type=F32)
    lane = lax.broadcasted_iota(jnp.int32, (1, LANE), 1)
    logits = jnp.where(lane < N_EXPERTS, logits, -jnp.inf)
    e = jnp.exp(logits - jnp.max(logits, axis=1, keepdims=True))
    aff = e / jnp.sum(e, axis=1, keepdims=True)
    aff_ref[...] = aff
    afft_ref[...] = aff.T[:N_EXPERTS, :]


def _mix(x2d, ya, yb, yc, proj, mod, norm2_g, wa, wb, wc, wo, wr, layer, seq, per_request):
    m_rows = x2d.shape[0]
    tm = 256
    gate_blk = lambda tile: (lambda i: (i, tile * LANE // D_MODEL))
    wspec = lambda w: pl.BlockSpec((1,) + w.shape[1:], lambda i: (layer, 0, 0))
    branch = pl.BlockSpec((tm, ya.shape[1]), lambda i: (i, 0))
    return pl.pallas_call(
        _mix_kernel,
        grid=(m_rows // tm,),
        in_specs=[
            pl.BlockSpec((tm, D_MODEL), lambda i: (i, 0)),
            branch, branch, branch,
            pl.BlockSpec((tm, D_MODEL), gate_blk(T_GA)),
            pl.BlockSpec((tm, D_MODEL), gate_blk(T_GB)),
            pl.BlockSpec((tm, D_MODEL), gate_blk(T_GC)),
            pl.BlockSpec((1, MOD_ROWS, D_MODEL), lambda i: ((i * tm) // seq if per_request else 0, 0, 0)),
            pl.BlockSpec((1, 1, D_MODEL), lambda i: (layer, 0, 0)),
            wspec(wa), wspec(wb), wspec(wc), wspec(wo), wspec(wr),
        ],
        out_specs=[
            pl.BlockSpec((tm, D_MODEL), lambda i: (i, 0)),
            pl.BlockSpec((tm, D_MODEL), lambda i: (i, 0)),
            pl.BlockSpec((tm, LANE), lambda i: (i, 0)),
            pl.BlockSpec((N_EXPERTS, tm), lambda i: (0, i)),
        ],
        out_shape=[
            jax.ShapeDtypeStruct((m_rows, D_MODEL), F32),
            jax.ShapeDtypeStruct((m_rows, D_MODEL), BF16),
            jax.ShapeDtypeStruct((m_rows, LANE), F32),
            jax.ShapeDtypeStruct((N_EXPERTS, m_rows), F32),
        ],
        compiler_params=_cparams("parallel"),
    )(x2d, ya, yb, yc, proj, proj, proj, mod, norm2_g.reshape(DEPTH, 1, D_MODEL), wa, wb, wc, wo, wr)


RANK_ROWS = 128


def _slot_onehot(rank_row, cap):
    slot = lax.broadcasted_iota(jnp.int32, (cap, 1), 0).astype(F32)
    return jnp.where(rank_row == slot, 1.0, 0.0).astype(BF16)


def _gather_kernel(aff_ref, afft_ref, h_ref, xs_ref, rank_ref, *, seq, cap):
    e = pl.program_id(1)
    a_row = afft_ref[pl.ds(e, 1), :]
    t_idx = lax.broadcasted_iota(jnp.int32, (1, seq), 1)
    lane = lax.broadcasted_iota(jnp.int32, (1, LANE), 1)
    rank = jnp.zeros((1, seq), F32)
    for blk in range(seq // RANK_ROWS):
        rows = slice(blk * RANK_ROWS, (blk + 1) * RANK_ROWS)
        a_col = jnp.sum(jnp.where(lane == e, aff_ref[rows, :], 0.0), axis=1, keepdims=True)
        s_idx = blk * RANK_ROWS + lax.broadcasted_iota(jnp.int32, (RANK_ROWS, 1), 0)
        beats = (a_col > a_row) | ((a_col == a_row) & (s_idx < t_idx))
        rank = rank + jnp.sum(jnp.where(beats, 1.0, 0.0), axis=0, keepdims=True)
    rank_ref[pl.ds(e, 1), :] = rank
    xs_ref[0, 0] = _bf(jnp.dot(_slot_onehot(rank, cap), h_ref[...], preferred_element_type=F32))


def _gather(aff, afft, h2, batch, seq, cap):
    return pl.pallas_call(
        functools.partial(_gather_kernel, seq=seq, cap=cap),
        grid=(batch, N_EXPERTS),
        in_specs=[
            pl.BlockSpec((seq, LANE), lambda r, e: (r, 0)),
            pl.BlockSpec((N_EXPERTS, seq), lambda r, e: (0, r)),
            pl.BlockSpec((seq, D_MODEL), lambda r, e: (r, 0)),
        ],
        out_specs=[
            pl.BlockSpec((1, 1, cap, D_MODEL), lambda r, e: (e, r, 0, 0)),
            pl.BlockSpec((N_EXPERTS, seq), lambda r, e: (0, r)),
        ],
        out_shape=[
            jax.ShapeDtypeStruct((N_EXPERTS, batch, cap, D_MODEL), BF16),
            jax.ShapeDtypeStruct((N_EXPERTS, batch * seq), F32),
        ],
        compiler_params=_cparams("parallel", "arbitrary"),
    )(aff, afft, h2)


def _ffn_kernel(x_ref, wg_ref, wu_ref, wd_ref, y_ref):
    x = x_ref[0]
    hid = _silu(_mm(x, wg_ref[0, 0])) * _mm(x, wu_ref[0, 0])
    y_ref[0] = _mm(hid, wd_ref[0, 0])


def _expert_ffn(xs, w_gate, w_up, w_down, layer):
    n_rows = xs.shape[1]
    return pl.pallas_call(
        _ffn_kernel,
        grid=(N_EXPERTS,),
        in_specs=[
            pl.BlockSpec((1, n_rows, D_MODEL), lambda e: (e, 0, 0)),
            pl.BlockSpec((1, 1, D_MODEL, D_EXPERT), lambda e: (layer, e, 0, 0)),
            pl.BlockSpec((1, 1, D_MODEL, D_EXPERT), lambda e: (layer, e, 0, 0)),
            pl.BlockSpec((1, 1, D_EXPERT, D_MODEL), lambda e: (layer, e, 0, 0)),
        ],
        out_specs=pl.BlockSpec((1, n_rows, D_MODEL), lambda e: (e, 0, 0)),
        out_shape=jax.ShapeDtypeStruct((N_EXPERTS, n_rows, D_MODEL), F32),
        compiler_params=_cparams("parallel"),
    )(xs, w_gate, w_up, w_down)


def _scatter_kernel(x_ref, aff_ref, rank_ref, ye_ref, mod_ref, fg_ref, o_ref, acc, *, cap, final):
    e = pl.program_id(1)

    @pl.when(e == 0)
    def _():
        acc[...] = jnp.zeros_like(acc)

    lane = lax.broadcasted_iota(jnp.int32, (1, LANE), 1)
    gate = jnp.sum(jnp.where(lane == e, aff_ref[...], 0.0), axis=1, keepdims=True)
    onehot = _slot_onehot(rank_ref[pl.ds(e, 1), :], cap)
    hi, lo = _split2(ye_ref[0, 0])
    tn = (((0,), (0,)), ((), ()))
    spread = (lax.dot_general(onehot, hi, tn, preferred_element_type=F32)
              + lax.dot_general(onehot, lo, tn, preferred_element_type=F32))
    acc[...] += gate * spread

    @pl.when(e == N_EXPERTS - 1)
    def _():
        x = x_ref[...] + mod_ref[0][5:6] * acc[...]
        o_ref[...] = _rms(x, fg_ref[...]) if final else x


def _scatter(x2d, aff, rank, ye, mod, final_g, batch, seq, cap, per_request, final):
    return pl.pallas_call(
        functools.partial(_scatter_kernel, cap=cap, final=final),
        grid=(batch, N_EXPERTS),
        in_specs=[
            pl.BlockSpec((seq, D_MODEL), lambda r, e: (r, 0)),
            pl.BlockSpec((seq, LANE), lambda r, e: (r, 0)),
            pl.BlockSpec((N_EXPERTS, seq), lambda r, e: (0, r)),
            pl.BlockSpec((1, 1, cap, D_MODEL), lambda r, e: (e, r, 0, 0)),
            pl.BlockSpec((1, MOD_ROWS, D_MODEL), lambda r, e: (r if per_request else 0, 0, 0)),
            pl.BlockSpec((1, D_MODEL), lambda r, e: (0, 0)),
        ],
        out_specs=pl.BlockSpec((seq, D_MODEL), lambda r, e: (r, 0)),
        out_shape=jax.ShapeDtypeStruct((batch * seq, D_MODEL), F32),
        scratch_shapes=[pltpu.VMEM((seq, D_MODEL), F32)],
        compiler_params=_cparams("parallel", "arbitrary"),
    )(x2d, aff, rank, ye, mod, final_g.reshape(1, D_MODEL))


def _reorder_w_in(w_in):
    a_main = w_in[:, :, 0:2048]
    a_gate = w_in[:, :, 2048:2064]
    b_main = w_in[:, :, 2064:3600]
    b_gate = w_in[:, :, 3600:3616]
    c_main = w_in[:, :, 3616:4384]
    merge = w_in[:, :, 4384:7456]

    def per_head(g, heads):
        g = g.reshape(DEPTH, D_MODEL, 4, heads).transpose(0, 1, 3, 2)
        g = jnp.pad(g, ((0, 0), (0, 0), (0, 0), (0, GATE_STRIDE - 4)))
        return g.reshape(DEPTH, D_MODEL, heads * GATE_STRIDE)

    gates = jnp.concatenate([per_head(a_gate, HEADS_A), per_head(b_gate, HEADS_B)], axis=-1)
    pad = jnp.zeros((DEPTH, D_MODEL, N_PROJ - T_GATES * LANE - gates.shape[-1]), w_in.dtype)
    return _bf(jnp.concatenate([merge, a_main, b_main, c_main, gates, pad], axis=-1))


def _lane_rows(rows):
    p = jnp.stack(rows, axis=-1)
    p = p.reshape(-1, len(rows))
    return jnp.broadcast_to(p[:, :, None], p.shape + (LANE,)).astype(F32)


def kernel(x_prompt, x_sample, cache_attn_k, cache_attn_v, state_delta, state_mlstm_c, state_mlstm_n, state_mlstm_m, c, c_ctx, ada_w, ada_b, norm1_g, norm2_g, w_in, conv_qkv_a, delta_a_log, delta_dt_bias, delta_norm_g, mlstm_i_bias, mlstm_f_bias, mlstm_norm_g, attn_sink, w_branch_a, w_branch_b, w_branch_c, w_out, w_router, w_expert_gate, w_expert_up, w_expert_down, final_norm_g):
    batch_p, seq_p, _ = x_prompt.shape
    batch_s, seq_s, _ = x_sample.shape
    past = cache_attn_k.shape[2]

    cond = jnp.concatenate([c_ctx[None, :], c, jnp.zeros((COND_ROWS - 1 - batch_s, D_MODEL), F32)], axis=0)
    mod = _modulation(cond, ada_w, ada_b).reshape(DEPTH, COND_ROWS, ADA_CHUNKS, D_MODEL)
    mod = jnp.pad(mod, ((0, 0), (0, 0), (0, MOD_ROWS - ADA_CHUNKS), (0, 0)))

    w_in_r = _reorder_w_in(w_in)
    wa, wb, wc, wo = _bf(w_branch_a), _bf(w_branch_b), _bf(w_branch_c), _bf(w_out)
    wr = _bf(jnp.pad(w_router, ((0, 0), (0, 0), (0, LANE - N_EXPERTS))))
    par_a = _lane_rows([delta_a_log[:, 0], delta_a_log[:, 1], delta_dt_bias[:, 0], delta_dt_bias[:, 1]])
    par_b = _lane_rows([mlstm_i_bias[:, 0], mlstm_i_bias[:, 1], mlstm_f_bias[:, 0], mlstm_f_bias[:, 1]])
    sink = jnp.broadcast_to(attn_sink[:, :, None], (DEPTH, HEADS_C, LANE)).astype(F32)
    cache_k = cache_attn_k.reshape(batch_s, DEPTH, past, KV_HEADS_C * HEAD_DIM_C)
    cache_v = cache_attn_v.reshape(batch_s, DEPTH, past, KV_HEADS_C * HEAD_DIM_C)
    state_n = state_mlstm_n.reshape(batch_s, DEPTH, 2, HEADS_B, 1, DK_B)
    state_m = jnp.broadcast_to(state_mlstm_m[..., None, None], (batch_s, DEPTH, 2, HEADS_B, 1, LANE)).astype(F32)
    cos, sin = _rope_tables(seq_s)

    def layer(x2d, l, batch, seq, latent):
        mod_l = mod[l, 1:1 + batch] if latent else mod[l, 0:1]
        cap = EC_CAPACITY * seq // N_EXPERTS
        proj = _in_proj(x2d, mod_l, norm1_g, w_in_r, l, seq, latent)
        ya, d_new = _delta_mixer(proj, conv_qkv_a, par_a, delta_norm_g, l, batch, seq,
                                 state_delta if latent else None)
        yb, b_new = _mlstm_mixer(proj, par_b, mlstm_norm_g, l, batch, seq,
                                 (state_mlstm_c, state_n, state_m) if latent else None)
        if latent:
            yc = _latent_attention(proj, cache_k, cache_v, cos, sin, sink, l, batch, seq)
        else:
            yc = _ctx_attention(proj, sink, l, batch, seq)
        x1, h2, aff, afft = _mix(x2d, ya, yb, yc, proj, mod_l, norm2_g, wa, wb, wc, wo, wr, l, seq, latent)
        xs, rank = _gather(aff, afft, h2, batch, seq, cap)
        ye = _expert_ffn(xs.reshape(N_EXPERTS, batch * cap, D_MODEL), w_expert_gate, w_expert_up, w_expert_down, l)
        x2 = _scatter(x1, aff, rank, ye.reshape(N_EXPERTS, batch, cap, D_MODEL), mod_l, final_norm_g,
                      batch, seq, cap, latent, l == DEPTH - 1)
        return x2, proj, d_new, b_new

    xp = x_prompt.reshape(batch_p * seq_p, D_MODEL)
    ks, vs, ds, cs, ns, ms = [], [], [], [], [], []
    for l in range(DEPTH):
        xp, proj, d_new, (c_new, n_new, m_new) = layer(xp, l, batch_p, seq_p, False)
        ks.append(proj[:, T_KC * LANE:(T_KC + 1) * LANE].reshape(batch_p, seq_p, KV_HEADS_C, HEAD_DIM_C))
        vs.append(proj[:, T_VC * LANE:(T_VC + 1) * LANE].reshape(batch_p, seq_p, KV_HEADS_C, HEAD_DIM_C))
        ds.append(d_new)
        cs.append(c_new)
        ns.append(n_new)
        ms.append(m_new)

    xs = x_sample.reshape(batch_s * seq_s, D_MODEL)
    for l in range(DEPTH):
        xs, _, _, _ = layer(xs, l, batch_s, seq_s, True)

    stack = lambda parts: jnp.stack(parts, axis=1)
    return (xp.reshape(batch_p, seq_p, D_MODEL), xs.reshape(batch_s, seq_s, D_MODEL),
            stack(ks), stack(vs), stack(ds), stack(cs), stack(ns), stack(ms))
```

```python
import functools

import jax
import jax.numpy as jnp
from jax import lax
from jax.experimental import pallas as pl
from jax.experimental.pallas import tpu as pltpu

F32 = jnp.float32
BF16 = jnp.bfloat16

D_MODEL = 1024
DEPTH = 2
GRID_W = 64
EPS = 1e-6
HEADS_A = 4
DK_A = 128
DV_A = 128
CONV_K = 5
CHUNK = 64
HEADS_B = 4
DK_B = 64
DV_B = 128
HEADS_C = 8
KV_HEADS_C = 2
HEAD_DIM_C = 64
GROUP_C = HEADS_C // KV_HEADS_C
WINDOW = 128
Q_BLOCK = 128
ROPE_BASE = 10000.0
N_EXPERTS = 16
D_EXPERT = 512
EC_CAPACITY = 2
ADA_CHUNKS = 6

LANE = 128
MOD_ROWS = 8
COND_ROWS = 16

T_GA, T_GB, T_GC = 0, 8, 16
T_QA, T_KA, T_VA, T_ZA = 24, 28, 32, 36
T_QB, T_KB, T_VB, T_OB = 40, 42, 44, 48
T_QC, T_KC, T_VC = 52, 56, 57
T_GATES = 58
N_TILES = 60
N_PROJ = N_TILES * LANE
GATE_STRIDE = 8
GATE_B_OFF = HEADS_A * GATE_STRIDE

VMEM_LIMIT = 48 * 1024 * 1024


def _cparams(*sem):
    return pltpu.CompilerParams(dimension_semantics=sem, vmem_limit_bytes=VMEM_LIMIT)


def _bf(x):
    return x.astype(BF16)


def _mm(a, b):
    return jnp.dot(_bf(a), _bf(b), preferred_element_type=F32)


def _mm_nt(a, b):
    return lax.dot_general(_bf(a), _bf(b), (((1,), (1,)), ((), ())), preferred_element_type=F32)


def _mm_tn(a, b):
    return lax.dot_general(_bf(a), _bf(b), (((0,), (0,)), ((), ())), preferred_element_type=F32)


def _split2(x):
    hi = _bf(x)
    return hi, _bf(x - hi.astype(F32))


def _split3(x):
    hi = _bf(x)
    r = x - hi.astype(F32)
    mid = _bf(r)
    return hi, mid, _bf(r - mid.astype(F32))


def _mm_sel(sel, x):
    hi, mid, lo = _split3(x)
    d = functools.partial(jnp.dot, preferred_element_type=F32)
    return (d(sel, hi) + d(sel, mid)) + d(sel, lo)


def _mm_hi(a, b):
    ah, al = _split2(a)
    bh, bl = _split2(b)
    d = functools.partial(jnp.dot, preferred_element_type=F32)
    return d(ah, bh) + (d(ah, bl) + d(al, bh))


def _sigmoid(x):
    return 1.0 / (1.0 + jnp.exp(-x))


def _silu(x):
    return x * _sigmoid(x)


def _softplus(x):
    return jnp.maximum(x, 0.0) + jnp.log(1.0 + jnp.exp(-jnp.abs(x)))


def _rms(x, g):
    return x * lax.rsqrt(jnp.mean(x * x, axis=-1, keepdims=True) + EPS) * g


def _chunk_masks(backward):
    ri = lax.broadcasted_iota(jnp.int32, (CHUNK, CHUNK), 0)
    ci = lax.broadcasted_iota(jnp.int32, (CHUNK, CHUNK), 1)
    if backward:
        return ri <= ci, ri < ci, ri == ci
    return ri >= ci, ri > ci, ri == ci


def _mod_kernel(c_ref, w_ref, b_ref, o_ref):
    o_ref[0] = _mm(_silu(c_ref[...]), w_ref[0]) + b_ref[0]


def _modulation(cond, ada_w, ada_b):
    n_out = ADA_CHUNKS * D_MODEL
    tn = 512
    return pl.pallas_call(
        _mod_kernel,
        grid=(DEPTH, n_out // tn),
        in_specs=[
            pl.BlockSpec((COND_ROWS, D_MODEL), lambda l, j: (0, 0)),
            pl.BlockSpec((1, D_MODEL, tn), lambda l, j: (l, 0, j)),
            pl.BlockSpec((1, 1, tn), lambda l, j: (l, 0, j)),
        ],
        out_specs=pl.BlockSpec((1, COND_ROWS, tn), lambda l, j: (l, 0, j)),
        out_shape=jax.ShapeDtypeStruct((DEPTH, COND_ROWS, n_out), F32),
        compiler_params=_cparams("parallel", "parallel"),
    )(cond, ada_w, ada_b.reshape(DEPTH, 1, n_out))


def _in_proj_kernel(x_ref, mod_ref, g_ref, w_ref, o_ref, h_scr):
    @pl.when(pl.program_id(1) == 0)
    def _():
        m = mod_ref[0]
        h_scr[...] = _bf(_rms(x_ref[...], g_ref[0]) * (1.0 + m[1:2]) + m[0:1])

    o_ref[...] = jnp.dot(h_scr[...], w_ref[0], preferred_element_type=F32)


def _in_proj(x2d, mod, norm_g, w_in, layer, seq, per_request):
    m_rows = x2d.shape[0]
    tm = min(1024, m_rows)
    tn = 768
    if per_request:
        assert seq % tm == 0
    return pl.pallas_call(
        _in_proj_kernel,
        grid=(m_rows // tm, N_PROJ // tn),
        in_specs=[
            pl.BlockSpec((tm, D_MODEL), lambda i, j: (i, 0)),
            pl.BlockSpec((1, MOD_ROWS, D_MODEL), lambda i, j: ((i * tm) // seq if per_request else 0, 0, 0)),
            pl.BlockSpec((1, 1, D_MODEL), lambda i, j: (layer, 0, 0)),
            pl.BlockSpec((1, D_MODEL, tn), lambda i, j: (layer, 0, j)),
        ],
        out_specs=pl.BlockSpec((tm, tn), lambda i, j: (i, j)),
        out_shape=jax.ShapeDtypeStruct((m_rows, N_PROJ), F32),
        scratch_shapes=[pltpu.VMEM((tm, D_MODEL), BF16)],
        compiler_params=_cparams("parallel", "arbitrary"),
    )(x2d, mod, norm_g.reshape(DEPTH, 1, D_MODEL), w_in)


PREP_CHUNKS = 2
INV_GROUP = 8


def _gate_dense(gt_parts, lane_index):
    row = lax.broadcasted_iota(jnp.int32, (LANE, LANE), 0)
    sel = jnp.where(row == lane_index, 1.0, 0.0).astype(BF16)
    hi, mid, lo = gt_parts
    d = functools.partial(jnp.dot, preferred_element_type=F32)
    return (d(hi, sel) + d(mid, sel)) + d(lo, sel)


def _delta_kernel(q_ref, k_ref, v_ref, z_ref, gt_ref, cq_ref, ck_ref, cv_ref, par_ref, ng_ref, *rest,
                  seq, has_init, emit_state):
    rest = list(rest)
    s0_ref = rest.pop(0) if has_init else None
    y_ref = rest.pop(0)
    st_ref = rest.pop(0) if emit_state else None
    qs, ks, vs, gates, s_scr, o_scr, u_scr, wq_scr, akd_scr, dk_scr, pw_scr, inv_scr, rhs_scr = rest
    head = pl.program_id(1)
    n_chunks = seq // CHUNK
    rows_t = lax.broadcasted_iota(jnp.int32, (seq, 1), 0)

    def conv_silu(x_ref, w_ref):
        x = x_ref[...]
        w = w_ref[0]
        acc = x * w[CONV_K // 2:CONV_K // 2 + 1, :]
        for j in range(CONV_K):
            s = j - CONV_K // 2
            if s == 0:
                continue
            shifted = pltpu.roll(x, (-s) % seq, 0)
            ok = (rows_t + s >= 0) & (rows_t + s < seq)
            acc = acc + jnp.where(ok, shifted, 0.0) * w[j:j + 1, :]
        return _silu(acc)

    def l2n(x):
        return x * lax.rsqrt(jnp.sum(x * x, axis=-1, keepdims=True) + EPS)

    qs[...] = l2n(conv_silu(q_ref, cq_ref)) * (DK_A ** -0.5)
    ks[...] = l2n(conv_silu(k_ref, ck_ref))
    vs[...] = conv_silu(v_ref, cv_ref)

    gt_parts = _split3(gt_ref[...])
    par = par_ref[0]
    base = head * GATE_STRIDE
    for d in range(2):
        gates[d] = _sigmoid(_gate_dense(gt_parts, base + d))
        gates[2 + d] = -jnp.exp(par[d:d + 1, :]) * _softplus(_gate_dense(gt_parts, base + 2 + d) + par[2 + d:3 + d, :])
        s_scr[d] = s0_ref[0, 0, d, 0] if has_init else jnp.zeros((DK_A, DV_A), F32)

    def chunk_rows(c):
        return pl.ds(pl.multiple_of(c * CHUNK, CHUNK), CHUNK)

    n_prob = 2 * n_chunks
    ri = lax.broadcasted_iota(jnp.int32, (CHUNK, CHUNK), 0)
    ci = lax.broadcasted_iota(jnp.int32, (CHUNK, CHUNK), 1)
    dot = functools.partial(jnp.dot, preferred_element_type=F32)

    def setup_body(i, carry):
        loaded = []
        for cc in range(PREP_CHUNKS):
            c = i * PREP_CHUNKS + cc
            rows = chunk_rows(c)
            loaded.append((c, qs[rows, :], ks[rows, :], vs[rows, :],
                           [(gates[d, rows, :], gates[2 + d, rows, :]) for d in range(2)]))
        results = []
        for c, q, k, v, gate_cols in loaded:
            kk = _mm_nt(k, k)
            qk = _mm_nt(q, k)
            for d in range(2):
                bt, g = gate_cols[d]
                incl, strict, _ = _chunk_masks(d == 1)
                rhs = jnp.concatenate(
                    [jnp.where(strict, g[:, :CHUNK], 0.0), jnp.zeros((CHUNK, LANE - CHUNK), F32), g], axis=1)
                cs = _mm_sel(jnp.where(incl, 1.0, 0.0).astype(BF16), rhs)
                gc = cs[:, LANE:]
                g_last = gc[0:1, :] if d == 1 else gc[CHUNK - 1:CHUNK, :]
                egc = jnp.exp(gc)
                dec = jnp.where(incl, jnp.exp(cs[:, :CHUNK]), 0.0)
                low = jnp.where(strict, bt[:, :CHUNK] * kk * dec, 0.0)
                rhs2 = jnp.concatenate([v * bt, k * (bt * egc)], axis=1)
                akd = _bf(jnp.concatenate([qk * dec, (k * jnp.exp(g_last - gc)).T], axis=0))
                results.append((d * n_chunks + c, low, rhs2, akd, _bf(q * egc), jnp.exp(g_last)))
        for p, low, rhs2, akd, qd, decay in results:
            pw_scr[p] = low
            inv_scr[p] = jnp.where(ri == ci, 1.0, 0.0) - low
            rhs_scr[p] = rhs2
            akd_scr[p] = akd
            wq_scr[p, CHUNK:, :] = qd
            dk_scr[p] = decay
        return carry

    lax.fori_loop(0, n_chunks // PREP_CHUNKS, setup_body, 0)

    def doubling_pass(do_product, do_square):
        def body(i, carry):
            loaded = []
            for j in range(INV_GROUP):
                p = i * INV_GROUP + j
                loaded.append((p, pw_scr[p], inv_scr[p] if do_product else None))
            results = []
            for p, power, inv in loaded:
                ph, pl_ = _split2(power)
                new_inv = None
                if do_product:
                    ih, il = _split2(inv)
                    new_inv = inv + (dot(ih, ph) + (dot(ih, pl_) + dot(il, ph)))
                new_power = dot(ph, ph) + (dot(ph, pl_) + dot(pl_, ph)) if do_square else None
                results.append((p, new_power, new_inv))
            for p, new_power, new_inv in results:
                if do_square:
                    pw_scr[p] = new_power
                if do_product:
                    inv_scr[p] = new_inv
            return carry

        lax.fori_loop(0, n_prob // INV_GROUP, body, 0)

    doubling_pass(False, True)
    span = 4
    while span < CHUNK:
        doubling_pass(True, True)
        span *= 2
    doubling_pass(True, False)

    def solve_body(i, carry):
        loaded = []
        for j in range(INV_GROUP):
            p = i * INV_GROUP + j
            loaded.append((p, inv_scr[p], rhs_scr[p]))
        results = [(p, _mm_hi(inv, rhs2)) for p, inv, rhs2 in loaded]
        for p, sol in results:
            u_scr[p] = sol[:, :DV_A]
            wq_scr[p, :CHUNK, :] = _bf(sol[:, DV_A:])
        return carry

    lax.fori_loop(0, n_prob // INV_GROUP, solve_body, 0)

    def scan_body(i, carry):
        loaded = []
        for d in range(2):
            c = n_chunks - 1 - i if d == 1 else i
            p = d * n_chunks + c
            loaded.append((c, s_scr[d], u_scr[p], wq_scr[p], akd_scr[p], dk_scr[p]))
        results = []
        for c, state, u, wq, akd, decay in loaded:
            ws = dot(wq, _bf(state))
            av = dot(akd, _bf(u - ws[:CHUNK]))
            results.append((c, ws[CHUNK:] + av[:CHUNK], decay * state + av[CHUNK:]))
        for d, (c, o, state) in enumerate(results):
            o_scr[d, chunk_rows(c), :] = o
            s_scr[d] = state
        return carry

    lax.fori_loop(0, n_chunks, scan_body, 0)

    z = z_ref[...]
    y_ref[...] = _rms(o_scr[0] + o_scr[1], ng_ref[0]) * _silu(z)
    if emit_state:
        st_ref[0, 0, 0] = s_scr[0]
        st_ref[0, 1, 0] = s_scr[1]


def _delta_mixer(proj, conv_w, par, norm_g, layer, batch, seq, state0):
    has_init = state0 is not None
    emit_state = not has_init
    col = lambda tile: (lambda r, h: (r, tile + h))
    cw = lambda part: (lambda r, h: (layer, 0, part * HEADS_A + h))
    in_specs = [
        pl.BlockSpec((seq, LANE), col(T_QA)),
        pl.BlockSpec((seq, LANE), col(T_KA)),
        pl.BlockSpec((seq, LANE), col(T_VA)),
        pl.BlockSpec((seq, LANE), col(T_ZA)),
        pl.BlockSpec((seq, LANE), lambda r, h: (r, T_GATES)),
        pl.BlockSpec((1, CONV_K, LANE), cw(0)),
        pl.BlockSpec((1, CONV_K, LANE), cw(1)),
        pl.BlockSpec((1, CONV_K, LANE), cw(2)),
        pl.BlockSpec((1, 4, LANE), lambda r, h: (layer * HEADS_A + h, 0, 0)),
        pl.BlockSpec((1, 1, DV_A), lambda r, h: (layer, 0, 0)),
    ]
    args = [proj, proj, proj, proj, proj, conv_w, conv_w, conv_w, par, norm_g.reshape(DEPTH, 1, DV_A)]
    if has_init:
        in_specs.append(pl.BlockSpec((1, 1, 2, 1, DK_A, DV_A), lambda r, h: (r, layer, 0, h, 0, 0)))
        args.append(state0)
    out_specs = [pl.BlockSpec((seq, DV_A), lambda r, h: (r, h))]
    out_shape = [jax.ShapeDtypeStruct((batch * seq, HEADS_A * DV_A), F32)]
    if emit_state:
        out_specs.append(pl.BlockSpec((1, 2, 1, DK_A, DV_A), lambda r, h: (r, 0, h, 0, 0)))
        out_shape.append(jax.ShapeDtypeStruct((batch, 2, HEADS_A, DK_A, DV_A), F32))
    n_prob = 2 * (seq // CHUNK)
    outs = pl.pallas_call(
        functools.partial(_delta_kernel, seq=seq, has_init=has_init, emit_state=emit_state),
        grid=(batch, HEADS_A),
        in_specs=in_specs,
        out_specs=out_specs,
        out_shape=out_shape,
        scratch_shapes=[
            pltpu.VMEM((seq, DK_A), F32), pltpu.VMEM((seq, DK_A), F32), pltpu.VMEM((seq, DV_A), F32),
            pltpu.VMEM((4, seq, LANE), F32), pltpu.VMEM((2, DK_A, DV_A), F32), pltpu.VMEM((2, seq, DV_A), F32),
            pltpu.VMEM((n_prob, CHUNK, DV_A), F32), pltpu.VMEM((n_prob, 2 * CHUNK, DK_A), BF16),
            pltpu.VMEM((n_prob, CHUNK + DK_A, CHUNK), BF16), pltpu.VMEM((n_prob, 1, DV_A), F32),
            pltpu.VMEM((n_prob, CHUNK, CHUNK), F32), pltpu.VMEM((n_prob, CHUNK, CHUNK), F32),
            pltpu.VMEM((n_prob, CHUNK, DV_A + DK_A), F32),
        ],
        compiler_params=_cparams("parallel", "parallel"),
    )(*args)
    return (outs[0], outs[1]) if emit_state else (outs[0], None)


HEADS_PER_STEP_B = 2


def _mlstm_kernel(q_ref, k_ref, v_ref, og_ref, gt_ref, par_ref, ng_ref, *rest, seq, has_init, emit_state):
    rest = list(rest)
    if has_init:
        c0_ref, n0_ref, m0_ref = rest[:3]
        rest = rest[3:]
    y_ref = rest.pop(0)
    if emit_state:
        co_ref, no_ref, mo_ref = rest[:3]
        rest = rest[3:]
    gates, c_scr, n_scr, m_scr, h_scr = rest
    pair = pl.program_id(1)
    n_chunks = seq // CHUNK
    gt_parts = _split3(gt_ref[...])

    for j in range(HEADS_PER_STEP_B):
        par = par_ref[j]
        base = GATE_B_OFF + (pair * HEADS_PER_STEP_B + j) * GATE_STRIDE
        for d in range(2):
            gates[j * 4 + d] = _gate_dense(gt_parts, base + d) + par[d:d + 1, :]
            gates[j * 4 + 2 + d] = -_softplus(-(_gate_dense(gt_parts, base + 2 + d) + par[2 + d:3 + d, :]))
            idx = j * 2 + d
            if has_init:
                c_scr[idx] = c0_ref[0, 0, d, j]
                n_scr[idx] = n0_ref[0, 0, d, j]
                m_scr[idx] = m0_ref[0, 0, d, j]
            else:
                c_scr[idx] = jnp.zeros((DK_B, DV_B), F32)
                n_scr[idx] = jnp.zeros((1, DK_B), F32)
                m_scr[idx] = jnp.zeros((1, LANE), F32)

    n_chain = 2 * HEADS_PER_STEP_B

    def scan_body(i, carry):
        loaded = []
        for chain in range(n_chain):
            j, d = chain // 2, chain % 2
            c = n_chunks - 1 - i if d == 1 else i
            rows = pl.ds(pl.multiple_of(c * CHUNK, CHUNK), CHUNK)
            loaded.append((chain, rows, q_ref[rows, j * DK_B:(j + 1) * DK_B],
                           k_ref[rows, j * DK_B:(j + 1) * DK_B] * (DK_B ** -0.5),
                           v_ref[rows, j * DV_B:(j + 1) * DV_B],
                           gates[j * 4 + d, rows, :], gates[j * 4 + 2 + d, rows, :],
                           c_scr[chain], n_scr[chain], m_scr[chain]))
        results = []
        for chain, rows, q, k, v, ig, lf, cmat, nvec, m_prev in loaded:
            d = chain % 2
            incl, strict, diag = _chunk_masks(d == 1)
            rhs = jnp.concatenate(
                [jnp.where(strict, lf[:, :CHUNK], 0.0) + jnp.where(diag, ig[:, :CHUNK], 0.0),
                 jnp.zeros((CHUNK, LANE - CHUNK), F32), lf], axis=1)
            cs = _mm_sel(jnp.where(incl, 1.0, 0.0).astype(BF16), rhs)
            bc = cs[:, LANE:]
            b_last = bc[0:1, :] if d == 1 else bc[CHUNK - 1:CHUNK, :]
            d_log = jnp.where(incl, cs[:, :CHUNK], -jnp.inf)
            d_max = jnp.max(d_log, axis=1, keepdims=True)
            tok = b_last - bc + ig
            m_t = jnp.maximum(bc + m_prev, d_max)
            w_inter = jnp.exp(bc + m_prev - m_t)
            pm = jnp.exp(d_log - m_t[:, :CHUNK]) * _mm_nt(q, k)
            num = w_inter * _mm(q, cmat) + _mm(pm, v)
            den = jnp.sum(w_inter[:, :DK_B] * (q * nvec) + pm, axis=1, keepdims=True)
            m_new = jnp.maximum(b_last + m_prev, jnp.max(tok, axis=0, keepdims=True))
            w_prev = jnp.exp(b_last + m_prev - m_new)
            kw = k * jnp.exp(tok - m_new)[:, :DK_B]
            results.append((chain, rows, num / jnp.maximum(jnp.abs(den), jnp.exp(-m_t)),
                            w_prev * cmat + _mm_tn(kw, v),
                            w_prev[:, :DK_B] * nvec + jnp.sum(kw, axis=0, keepdims=True), m_new))
        for chain, rows, h, cmat, nvec, m_new in results:
            h_scr[chain, rows, :] = h
            c_scr[chain] = cmat
            n_scr[chain] = nvec
            m_scr[chain] = m_new
        return carry

    lax.fori_loop(0, n_chunks, scan_body, 0)

    og = og_ref[...]
    for j in range(HEADS_PER_STEP_B):
        h = h_scr[j * 2] + h_scr[j * 2 + 1]
        y_ref[:, j * DV_B:(j + 1) * DV_B] = _rms(h, ng_ref[0]) * _sigmoid(og[:, j * DV_B:(j + 1) * DV_B])
        if emit_state:
            for d in range(2):
                co_ref[0, d, j] = c_scr[j * 2 + d]
                no_ref[0, d, j] = n_scr[j * 2 + d]
                mo_ref[0, d, j] = m_scr[j * 2 + d]


def _mlstm_mixer(proj, par, norm_g, layer, batch, seq, state0):
    has_init = state0 is not None
    emit_state = not has_init
    hp = HEADS_PER_STEP_B
    n_pairs = HEADS_B // hp
    in_specs = [
        pl.BlockSpec((seq, hp * DK_B), lambda r, p: (r, T_QB + p)),
        pl.BlockSpec((seq, hp * DK_B), lambda r, p: (r, T_KB + p)),
        pl.BlockSpec((seq, hp * DV_B), lambda r, p: (r, T_VB // hp + p)),
        pl.BlockSpec((seq, hp * DV_B), lambda r, p: (r, T_OB // hp + p)),
        pl.BlockSpec((seq, LANE), lambda r, p: (r, T_GATES)),
        pl.BlockSpec((hp, 4, LANE), lambda r, p: (layer * n_pairs + p, 0, 0)),
        pl.BlockSpec((1, 1, DV_B), lambda r, p: (layer, 0, 0)),
    ]
    args = [proj, proj, proj, proj, proj, par, norm_g.reshape(DEPTH, 1, DV_B)]
    if has_init:
        c0, n0, m0 = state0
        in_specs += [
            pl.BlockSpec((1, 1, 2, hp, DK_B, DV_B), lambda r, p: (r, layer, 0, p, 0, 0)),
            pl.BlockSpec((1, 1, 2, hp, 1, DK_B), lambda r, p: (r, layer, 0, p, 0, 0)),
            pl.BlockSpec((1, 1, 2, hp, 1, LANE), lambda r, p: (r, layer, 0, p, 0, 0)),
        ]
        args += [c0, n0, m0]
    out_specs = [pl.BlockSpec((seq, hp * DV_B), lambda r, p: (r, p))]
    out_shape = [jax.ShapeDtypeStruct((batch * seq, HEADS_B * DV_B), F32)]
    if emit_state:
        out_specs += [
            pl.BlockSpec((1, 2, hp, DK_B, DV_B), lambda r, p: (r, 0, p, 0, 0)),
            pl.BlockSpec((1, 2, hp, 1, DK_B), lambda r, p: (r, 0, p, 0, 0)),
            pl.BlockSpec((1, 2, hp, 1, LANE), lambda r, p: (r, 0, p, 0, 0)),
        ]
        out_shape += [
            jax.ShapeDtypeStruct((batch, 2, HEADS_B, DK_B, DV_B), F32),
            jax.ShapeDtypeStruct((batch, 2, HEADS_B, 1, DK_B), F32),
            jax.ShapeDtypeStruct((batch, 2, HEADS_B, 1, LANE), F32),
        ]
    outs = pl.pallas_call(
        functools.partial(_mlstm_kernel, seq=seq, has_init=has_init, emit_state=emit_state),
        grid=(batch, n_pairs),
        in_specs=in_specs,
        out_specs=out_specs,
        out_shape=out_shape,
        scratch_shapes=[
            pltpu.VMEM((4 * hp, seq, LANE), F32), pltpu.VMEM((2 * hp, DK_B, DV_B), F32),
            pltpu.VMEM((2 * hp, 1, DK_B), F32), pltpu.VMEM((2 * hp, 1, LANE), F32),
            pltpu.VMEM((2 * hp, seq, DV_B), F32),
        ],
        compiler_params=_cparams("parallel", "parallel"),
    )(*args)
    if emit_state:
        return outs[0], (outs[1], outs[2][:, :, :, 0, :], outs[3][:, :, :, 0, 0])
    return outs[0], None


def _stack_heads(q, kv_head):
    return jnp.concatenate(
        [q[:, (kv_head * GROUP_C + g) * HEAD_DIM_C:(kv_head * GROUP_C + g + 1) * HEAD_DIM_C] for g in range(GROUP_C)],
        axis=0)


def _sink_column(sink, kv_head, rows):
    return jnp.concatenate(
        [jnp.broadcast_to(sink[kv_head * GROUP_C + g:kv_head * GROUP_C + g + 1, 0:1], (rows, 1))
         for g in range(GROUP_C)], axis=0)


def _unstack_heads(per_kv, rows):
    return jnp.concatenate(
        [o[g * rows:(g + 1) * rows, :] for o in per_kv for g in range(GROUP_C)], axis=1)


def _ctx_attn_kernel(q_ref, k_ref, v_ref, sink_ref, o_ref, *, seq):
    q = q_ref[...]
    k = k_ref[...]
    v = v_ref[...]
    sink = sink_ref[0]
    scale = HEAD_DIM_C ** -0.5
    outs = []
    for kv in range(KV_HEADS_C):
        lanes = slice(kv * HEAD_DIM_C, (kv + 1) * HEAD_DIM_C)
        s = _mm_nt(_stack_heads(q, kv), k[:, lanes]) * scale
        sk = _sink_column(sink, kv, seq)
        m = jnp.maximum(jnp.max(s, axis=1, keepdims=True), sk)
        e = jnp.exp(s - m)
        den = jnp.sum(e, axis=1, keepdims=True) + jnp.exp(sk - m)
        outs.append(_mm(e / den, v[:, lanes]))
    o_ref[...] = _unstack_heads(outs, seq)


def _ctx_attention(proj, sink, layer, batch, seq):
    width = HEADS_C * HEAD_DIM_C
    return pl.pallas_call(
        functools.partial(_ctx_attn_kernel, seq=seq),
        grid=(batch,),
        in_specs=[
            pl.BlockSpec((seq, width), lambda r: (r, T_QC * LANE // width)),
            pl.BlockSpec((seq, LANE), lambda r: (r, T_KC)),
            pl.BlockSpec((seq, LANE), lambda r: (r, T_VC)),
            pl.BlockSpec((1, HEADS_C, LANE), lambda r: (layer, 0, 0)),
        ],
        out_specs=pl.BlockSpec((seq, width), lambda r: (r, 0)),
        out_shape=jax.ShapeDtypeStruct((batch * seq, width), F32),
        compiler_params=_cparams("parallel"),
    )(proj, proj, proj, sink)


def _rope(x, cos, sin):
    quarter = HEAD_DIM_C // 4
    lane = lax.broadcasted_iota(jnp.int32, (1, LANE), 1)
    first = (lane % (2 * quarter)) < quarter
    partner = jnp.where(first, -pltpu.roll(x, LANE - quarter, 1), pltpu.roll(x, quarter, 1))
    return x * cos + partner * sin


def _latent_attn_kernel(q_ref, k_ref, v_ref, ck_ref, cv_ref, cq_ref, sq_ref, cos_ref, sin_ref, sink_ref, o_ref, *, seq):
    blk = pl.program_id(1)
    span = Q_BLOCK + 2 * WINDOW
    start = blk * Q_BLOCK
    k_start = pl.multiple_of(jnp.clip(start - WINDOW, 0, seq - span), Q_BLOCK)
    win = pl.ds(k_start, span)
    cq = cq_ref[...]
    sq = sq_ref[...]
    q = jnp.concatenate(
        [_rope(q_ref[:, s * LANE:(s + 1) * LANE], cq, sq) for s in range(HEADS_C * HEAD_DIM_C // LANE)], axis=1)
    k = _rope(k_ref[win, :], cos_ref[win, :], sin_ref[win, :])
    v = v_ref[win, :]
    ck = ck_ref[0, 0]
    cv = cv_ref[0, 0]
    sink = sink_ref[0]
    scale = HEAD_DIM_C ** -0.5
    q_pos = start + lax.broadcasted_iota(jnp.int32, (GROUP_C * Q_BLOCK, 1), 0) % Q_BLOCK
    k_pos = k_start + lax.broadcasted_iota(jnp.int32, (1, span), 1)
    valid = jnp.abs(q_pos - k_pos) <= WINDOW
    outs = []
    for kv in range(KV_HEADS_C):
        lanes = slice(kv * HEAD_DIM_C, (kv + 1) * HEAD_DIM_C)
        qs = _stack_heads(q, kv)
        sl = jnp.where(valid, _mm_nt(qs, k[:, lanes]) * scale, -jnp.inf)
        sc = _mm_nt(qs, ck[:, lanes]) * scale
        sk = _sink_column(sink, kv, Q_BLOCK)
        m = jnp.maximum(jnp.maximum(jnp.max(sl, axis=1, keepdims=True), jnp.max(sc, axis=1, keepdims=True)), sk)
        el = jnp.exp(sl - m)
        ec = jnp.exp(sc - m)
        den = jnp.sum(el, axis=1, keepdims=True) + jnp.sum(ec, axis=1, keepdims=True) + jnp.exp(sk - m)
        outs.append(_mm(el / den, v[:, lanes]) + _mm(ec / den, cv[:, lanes]))
    o_ref[...] = _unstack_heads(outs, Q_BLOCK)


def _latent_attention(proj, cache_k, cache_v, cos, sin, sink, layer, batch, seq):
    width = HEADS_C * HEAD_DIM_C
    n_blk = seq // Q_BLOCK
    past = cache_k.shape[2]
    return pl.pallas_call(
        functools.partial(_latent_attn_kernel, seq=seq),
        grid=(batch, n_blk),
        in_specs=[
            pl.BlockSpec((Q_BLOCK, width), lambda r, i: (r * n_blk + i, T_QC * LANE // width)),
            pl.BlockSpec((seq, LANE), lambda r, i: (r, T_KC)),
            pl.BlockSpec((seq, LANE), lambda r, i: (r, T_VC)),
            pl.BlockSpec((1, 1, past, LANE), lambda r, i: (r, layer, 0, 0)),
            pl.BlockSpec((1, 1, past, LANE), lambda r, i: (r, layer, 0, 0)),
            pl.BlockSpec((Q_BLOCK, LANE), lambda r, i: (i, 0)),
            pl.BlockSpec((Q_BLOCK, LANE), lambda r, i: (i, 0)),
            pl.BlockSpec((seq, LANE), lambda r, i: (0, 0)),
            pl.BlockSpec((seq, LANE), lambda r, i: (0, 0)),
            pl.BlockSpec((1, HEADS_C, LANE), lambda r, i: (layer, 0, 0)),
        ],
        out_specs=pl.BlockSpec((Q_BLOCK, width), lambda r, i: (r * n_blk + i, 0)),
        out_shape=jax.ShapeDtypeStruct((batch * seq, width), F32),
        compiler_params=_cparams("parallel", "parallel"),
    )(proj, proj, proj, cache_k, cache_v, cos, sin, cos, sin, sink)


def _rope_tables(seq):
    quarter = HEAD_DIM_C // 4
    pos = jnp.arange(seq)
    row = (pos // GRID_W).astype(F32)
    col = (pos % GRID_W).astype(F32)
    inv = jnp.power(ROPE_BASE, -jnp.arange(quarter, dtype=F32) / quarter)
    ang_row = row[:, None] * inv[None, :]
    ang_col = col[:, None] * inv[None, :]
    ang = jnp.concatenate([ang_row, ang_row, ang_col, ang_col], axis=1)
    ang = jnp.concatenate([ang] * (LANE // HEAD_DIM_C), axis=1)
    return jnp.cos(ang), jnp.sin(ang)


def _mix_kernel(x_ref, ya_ref, yb_ref, yc_ref, ga_ref, gb_ref, gc_ref, mod_ref, n2_ref,
                wa_ref, wb_ref, wc_ref, wo_ref, wr_ref, xo_ref, h_ref, aff_ref, afft_ref):
    m = mod_ref[0]
    mixed = (_sigmoid(ga_ref[...]) * _mm(ya_ref[...], wa_ref[0])
             + _sigmoid(gb_ref[...]) * _mm(yb_ref[...], wb_ref[0])
             + _sigmoid(gc_ref[...]) * _mm(yc_ref[...], wc_ref[0]))
    x = x_ref[...] + m[2:3] * _mm(mixed, wo_ref[0])
    xo_ref[...] = x
    h = _bf(_rms(x, n2_ref[0]) * (1.0 + m[4:5]) + m[3:4])
    h_ref[...] = h
    logits = jnp.dot(h, wr_ref[0], preferred_element_type=F32)
    lane = lax.broadcasted_iota(jnp.int32, (1, LANE), 1)
    logits = jnp.where(lane < N_EXPERTS, logits, -jnp.inf)
    e = jnp.exp(logits - jnp.max(logits, axis=1, keepdims=True))
    aff = e / jnp.sum(e, axis=1, keepdims=True)
    aff_ref[...] = aff
    afft_ref[...] = aff.T[:N_EXPERTS, :]


def _mix(x2d, ya, yb, yc, proj, mod, norm2_g, wa, wb, wc, wo, wr, layer, seq, per_request):
    m_rows = x2d.shape[0]
    tm = 256
    gate_blk = lambda tile: (lambda i: (i, tile * LANE // D_MODEL))
    wspec = lambda w: pl.BlockSpec((1,) + w.shape[1:], lambda i: (layer, 0, 0))
    branch = pl.BlockSpec((tm, ya.shape[1]), lambda i: (i, 0))
    return pl.pallas_call(
        _mix_kernel,
        grid=(m_rows // tm,),
        in_specs=[
            pl.BlockSpec((tm, D_MODEL), lambda i: (i, 0)),
            branch, branch, branch,
            pl.BlockSpec((tm, D_MODEL), gate_blk(T_GA)),
            pl.BlockSpec((tm, D_MODEL), gate_blk(T_GB)),
            pl.BlockSpec((tm, D_MODEL), gate_blk(T_GC)),
            pl.BlockSpec((1, MOD_ROWS, D_MODEL), lambda i: ((i * tm) // seq if per_request else 0, 0, 0)),
            pl.BlockSpec((1, 1, D_MODEL), lambda i: (layer, 0, 0)),
            wspec(wa), wspec(wb), wspec(wc), wspec(wo), wspec(wr),
        ],
        out_specs=[
            pl.BlockSpec((tm, D_MODEL), lambda i: (i, 0)),
            pl.BlockSpec((tm, D_MODEL), lambda i: (i, 0)),
            pl.BlockSpec((tm, LANE), lambda i: (i, 0)),
            pl.BlockSpec((N_EXPERTS, tm), lambda i: (0, i)),
        ],
        out_shape=[
            jax.ShapeDtypeStruct((m_rows, D_MODEL), F32),
            jax.ShapeDtypeStruct((m_rows, D_MODEL), BF16),
            jax.ShapeDtypeStruct((m_rows, LANE), F32),
            jax.ShapeDtypeStruct((N_EXPERTS, m_rows), F32),
        ],
        compiler_params=_cparams("parallel"),
    )(x2d, ya, yb, yc, proj, proj, proj, mod, norm2_g.reshape(DEPTH, 1, D_MODEL), wa, wb, wc, wo, wr)


SCATTER_K = 512


def _gather_kernel(aff_ref, afft_ref, h_ref, xs_ref, gate_ref, rankc_ref, rank_scr, onehot_scr, col_scr, *,
                   seq, cap):
    n_blk = seq // LANE
    ri = lax.broadcasted_iota(jnp.int32, (LANE, LANE), 0)
    ci = lax.broadcasted_iota(jnp.int32, (LANE, LANE), 1)
    lane = lax.broadcasted_iota(jnp.int32, (1, LANE), 1)
    slot = lax.broadcasted_iota(jnp.int32, (cap, 1), 0).astype(F32)

    def expert_body(e, carry):
        a_row = afft_ref[pl.ds(e, 1), :]
        a_col = jnp.sum(jnp.where(lane == e, aff_ref[...], 0.0), axis=1, keepdims=True)
        col_scr[...] = jnp.broadcast_to(a_col, (seq, LANE))
        rank_parts = []
        for tb in range(n_blk):
            a_t = a_row[:, tb * LANE:(tb + 1) * LANE]
            count = jnp.zeros((LANE, LANE), F32)
            for sb in range(n_blk):
                a_s = col_scr[sb * LANE:(sb + 1) * LANE, :]
                if sb < tb:
                    beats = a_s >= a_t
                elif sb > tb:
                    beats = a_s > a_t
                else:
                    beats = (a_s > a_t) | ((a_s == a_t) & (ri < ci))
                count = count + jnp.where(beats, 1.0, 0.0)
            rank_parts.append(jnp.sum(count, axis=0, keepdims=True))
        rank = jnp.concatenate(rank_parts, axis=1)
        rank_scr[pl.ds(e, 1), :] = rank
        chosen = rank == slot
        onehot_scr[pl.ds(pl.multiple_of(e * cap, cap), cap), :] = jnp.where(chosen, 1.0, 0.0).astype(BF16)
        gate_ref[e, 0] = jnp.sum(jnp.where(chosen, a_row, 0.0), axis=1, keepdims=True)
        return carry

    lax.fori_loop(0, N_EXPERTS, expert_body, 0)
    ranks = jnp.concatenate([rank_scr[...], jnp.zeros((LANE - N_EXPERTS, seq), F32)], axis=0)
    rankc_ref[...] = ranks.T
    per_group = SCATTER_K // cap
    h = h_ref[...]
    for i in range(N_EXPERTS // per_group):
        rows = jnp.dot(onehot_scr[i * SCATTER_K:(i + 1) * SCATTER_K, :], h, preferred_element_type=F32)
        xs_ref[i * per_group:(i + 1) * per_group, 0] = _bf(rows).reshape(per_group, cap, D_MODEL)


def _gather(aff, afft, h2, batch, seq, cap):
    return pl.pallas_call(
        functools.partial(_gather_kernel, seq=seq, cap=cap),
        grid=(batch,),
        in_specs=[
            pl.BlockSpec((seq, LANE), lambda r: (r, 0)),
            pl.BlockSpec((N_EXPERTS, seq), lambda r: (0, r)),
            pl.BlockSpec((seq, D_MODEL), lambda r: (r, 0)),
        ],
        out_specs=[
            pl.BlockSpec((N_EXPERTS, 1, cap, D_MODEL), lambda r: (0, r, 0, 0)),
            pl.BlockSpec((N_EXPERTS, 1, cap, 1), lambda r: (0, r, 0, 0)),
            pl.BlockSpec((seq, LANE), lambda r: (r, 0)),
        ],
        out_shape=[
            jax.ShapeDtypeStruct((N_EXPERTS, batch, cap, D_MODEL), BF16),
            jax.ShapeDtypeStruct((N_EXPERTS, batch, cap, 1), F32),
            jax.ShapeDtypeStruct((batch * seq, LANE), F32),
        ],
        scratch_shapes=[pltpu.VMEM((N_EXPERTS, seq), F32), pltpu.VMEM((N_EXPERTS * cap, seq), BF16),
                        pltpu.VMEM((seq, LANE), F32)],
        compiler_params=_cparams("parallel"),
    )(aff, afft, h2)


def _ffn_kernel(x_ref, gate_ref, wg_ref, wu_ref, wd_ref, hi_ref, lo_ref):
    x = x_ref[0]
    hid = _silu(_mm(x, wg_ref[0, 0])) * _mm(x, wu_ref[0, 0])
    y = _mm(hid, wd_ref[0, 0]) * gate_ref[0]
    hi, lo = _split2(y)
    hi_ref[0] = hi
    lo_ref[0] = lo


def _expert_ffn(xs, gate, w_gate, w_up, w_down, layer):
    n_rows = xs.shape[1]
    rows = pl.BlockSpec((1, n_rows, D_MODEL), lambda e: (e, 0, 0))
    out = jax.ShapeDtypeStruct((N_EXPERTS, n_rows, D_MODEL), BF16)
    return pl.pallas_call(
        _ffn_kernel,
        grid=(N_EXPERTS,),
        in_specs=[
            rows,
            pl.BlockSpec((1, n_rows, 1), lambda e: (e, 0, 0)),
            pl.BlockSpec((1, 1, D_MODEL, D_EXPERT), lambda e: (layer, e, 0, 0)),
            pl.BlockSpec((1, 1, D_MODEL, D_EXPERT), lambda e: (layer, e, 0, 0)),
            pl.BlockSpec((1, 1, D_EXPERT, D_MODEL), lambda e: (layer, e, 0, 0)),
        ],
        out_specs=[rows, rows],
        out_shape=[out, out],
        compiler_params=_cparams("parallel"),
    )(xs, gate, w_gate, w_up, w_down)


def _scatter_kernel(x_ref, rankc_ref, hi_ref, lo_ref, mod_ref, fg_ref, o_ref, acc, *, cap, final):
    g = pl.program_id(1)
    n_groups = pl.num_programs(1)
    per_lane_tile = LANE // cap
    rank_parts = _split3(rankc_ref[...])
    row = lax.broadcasted_iota(jnp.int32, (LANE, LANE), 0)
    lane = lax.broadcasted_iota(jnp.int32, (LANE, LANE), 1)
    lane_slot = (lax.broadcasted_iota(jnp.int32, (1, LANE), 1) % cap).astype(F32)
    d = functools.partial(jnp.dot, preferred_element_type=F32)
    tiles = []
    for b in range(SCATTER_K // LANE):
        first = (g * (SCATTER_K // LANE) + b) * per_lane_tile
        sel = jnp.where(row == first + lane // cap, 1.0, 0.0).astype(BF16)
        hi, mid, lo = rank_parts
        token_rank = (d(hi, sel) + d(mid, sel)) + d(lo, sel)
        tiles.append(jnp.where(token_rank == lane_slot, 1.0, 0.0).astype(BF16))
    onehot = jnp.concatenate(tiles, axis=1)
    spread = (d(onehot, hi_ref[:, 0].reshape(SCATTER_K, D_MODEL))
              + d(onehot, lo_ref[:, 0].reshape(SCATTER_K, D_MODEL)))

    @pl.when(g == 0)
    def _():
        acc[...] = spread

    @pl.when(g > 0)
    def _():
        acc[...] += spread

    @pl.when(g == n_groups - 1)
    def _():
        x = x_ref[...] + mod_ref[0][5:6] * acc[...]
        o_ref[...] = _rms(x, fg_ref[...]) if final else x


def _scatter(x2d, rankc, ye_hi, ye_lo, mod, final_g, batch, seq, cap, per_request, final):
    experts_per_group = SCATTER_K // cap
    slots = pl.BlockSpec((experts_per_group, 1, cap, D_MODEL), lambda r, g: (g, r, 0, 0))
    return pl.pallas_call(
        functools.partial(_scatter_kernel, cap=cap, final=final),
        grid=(batch, N_EXPERTS // experts_per_group),
        in_specs=[
            pl.BlockSpec((seq, D_MODEL), lambda r, g: (r, 0)),
            pl.BlockSpec((seq, LANE), lambda r, g: (r, 0)),
            slots, slots,
            pl.BlockSpec((1, MOD_ROWS, D_MODEL), lambda r, g: (r if per_request else 0, 0, 0)),
            pl.BlockSpec((1, D_MODEL), lambda r, g: (0, 0)),
        ],
        out_specs=pl.BlockSpec((seq, D_MODEL), lambda r, g: (r, 0)),
        out_shape=jax.ShapeDtypeStruct((batch * seq, D_MODEL), F32),
        scratch_shapes=[pltpu.VMEM((seq, D_MODEL), F32)],
        compiler_params=_cparams("parallel", "arbitrary"),
    )(x2d, rankc, ye_hi, ye_lo, mod, final_g.reshape(1, D_MODEL))


def _reorder_w_in(w_in):
    a_main = w_in[:, :, 0:2048]
    a_gate = w_in[:, :, 2048:2064]
    b_main = w_in[:, :, 2064:3600]
    b_gate = w_in[:, :, 3600:3616]
    c_main = w_in[:, :, 3616:4384]
    merge = w_in[:, :, 4384:7456]

    def per_head(g, heads):
        g = g.reshape(DEPTH, D_MODEL, 4, heads).transpose(0, 1, 3, 2)
        g = jnp.pad(g, ((0, 0), (0, 0), (0, 0), (0, GATE_STRIDE - 4)))
        return g.reshape(DEPTH, D_MODEL, heads * GATE_STRIDE)

    gates = jnp.concatenate([per_head(a_gate, HEADS_A), per_head(b_gate, HEADS_B)], axis=-1)
    pad = jnp.zeros((DEPTH, D_MODEL, N_PROJ - T_GATES * LANE - gates.shape[-1]), w_in.dtype)
    return _bf(jnp.concatenate([merge, a_main, b_main, c_main, gates, pad], axis=-1))


def _lane_rows(rows):
    p = jnp.stack(rows, axis=-1)
    p = p.reshape(-1, len(rows))
    return jnp.broadcast_to(p[:, :, None], p.shape + (LANE,)).astype(F32)


def kernel(x_prompt, x_sample, cache_attn_k, cache_attn_v, state_delta, state_mlstm_c, state_mlstm_n, state_mlstm_m, c, c_ctx, ada_w, ada_b, norm1_g, norm2_g, w_in, conv_qkv_a, delta_a_log, delta_dt_bias, delta_norm_g, mlstm_i_bias, mlstm_f_bias, mlstm_norm_g, attn_sink, w_branch_a, w_branch_b, w_branch_c, w_out, w_router, w_expert_gate, w_expert_up, w_expert_down, final_norm_g):
    batch_p, seq_p, _ = x_prompt.shape
    batch_s, seq_s, _ = x_sample.shape
    past = cache_attn_k.shape[2]

    cond = jnp.concatenate([c_ctx[None, :], c, jnp.zeros((COND_ROWS - 1 - batch_s, D_MODEL), F32)], axis=0)
    mod = _modulation(cond, ada_w, ada_b).reshape(DEPTH, COND_ROWS, ADA_CHUNKS, D_MODEL)
    mod = jnp.pad(mod, ((0, 0), (0, 0), (0, MOD_ROWS - ADA_CHUNKS), (0, 0)))

    w_in_r = _reorder_w_in(w_in)
    wa, wb, wc, wo = _bf(w_branch_a), _bf(w_branch_b), _bf(w_branch_c), _bf(w_out)
    wr = _bf(jnp.pad(w_router, ((0, 0), (0, 0), (0, LANE - N_EXPERTS))))
    par_a = _lane_rows([delta_a_log[:, 0], delta_a_log[:, 1], delta_dt_bias[:, 0], delta_dt_bias[:, 1]])
    par_b = _lane_rows([mlstm_i_bias[:, 0], mlstm_i_bias[:, 1], mlstm_f_bias[:, 0], mlstm_f_bias[:, 1]])
    sink = jnp.broadcast_to(attn_sink[:, :, None], (DEPTH, HEADS_C, LANE)).astype(F32)
    cache_k = cache_attn_k.reshape(batch_s, DEPTH, past, KV_HEADS_C * HEAD_DIM_C)
    cache_v = cache_attn_v.reshape(batch_s, DEPTH, past, KV_HEADS_C * HEAD_DIM_C)
    state_n = state_mlstm_n.reshape(batch_s, DEPTH, 2, HEADS_B, 1, DK_B)
    state_m = jnp.broadcast_to(state_mlstm_m[..., None, None], (batch_s, DEPTH, 2, HEADS_B, 1, LANE)).astype(F32)
    cos, sin = _rope_tables(seq_s)

    def layer(x2d, l, batch, seq, latent):
        mod_l = mod[l, 1:1 + batch] if latent else mod[l, 0:1]
        cap = EC_CAPACITY * seq // N_EXPERTS
        proj = _in_proj(x2d, mod_l, norm1_g, w_in_r, l, seq, latent)
        ya, d_new = _delta_mixer(proj, conv_qkv_a, par_a, delta_norm_g, l, batch, seq,
                                 state_delta if latent else None)
        yb, b_new = _mlstm_mixer(proj, par_b, mlstm_norm_g, l, batch, seq,
                                 (state_mlstm_c, state_n, state_m) if latent else None)
        if latent:
            yc = _latent_attention(proj, cache_k, cache_v, cos, sin, sink, l, batch, seq)
        else:
            yc = _ctx_attention(proj, sink, l, batch, seq)
        x1, h2, aff, afft = _mix(x2d, ya, yb, yc, proj, mod_l, norm2_g, wa, wb, wc, wo, wr, l, seq, latent)
        xs, gate, rankc = _gather(aff, afft, h2, batch, seq, cap)
        ye_hi, ye_lo = _expert_ffn(xs.reshape(N_EXPERTS, batch * cap, D_MODEL),
                                   gate.reshape(N_EXPERTS, batch * cap, 1),
                                   w_expert_gate, w_expert_up, w_expert_down, l)
        slots = (N_EXPERTS, batch, cap, D_MODEL)
        x2 = _scatter(x1, rankc, ye_hi.reshape(slots), ye_lo.reshape(slots), mod_l, final_norm_g,
                      batch, seq, cap, latent, l == DEPTH - 1)
        return x2, proj, d_new, b_new

    xp = x_prompt.reshape(batch_p * seq_p, D_MODEL)
    ks, vs, ds, cs, ns, ms = [], [], [], [], [], []
    for l in range(DEPTH):
        xp, proj, d_new, (c_new, n_new, m_new) = layer(xp, l, batch_p, seq_p, False)
        ks.append(proj[:, T_KC * LANE:(T_KC + 1) * LANE].reshape(batch_p, seq_p, KV_HEADS_C, HEAD_DIM_C))
        vs.append(proj[:, T_VC * LANE:(T_VC + 1) * LANE].reshape(batch_p, seq_p, KV_HEADS_C, HEAD_DIM_C))
        ds.append(d_new)
        cs.append(c_new)
        ns.append(n_new)
        ms.append(m_new)

    xs = x_sample.reshape(batch_s * seq_s, D_MODEL)
    for l in range(DEPTH):
        xs, _, _, _ = layer(xs, l, batch_s, seq_s, True)

    stack = lambda parts: jnp.stack(parts, axis=1)
    return (xp.reshape(batch_p, seq_p, D_MODEL), xs.reshape(batch_s, seq_s, D_MODEL),
            stack(ks), stack(vs), stack(ds), stack(cs), stack(ns), stack(ms))
```

```python
import functools

import jax
import jax.numpy as jnp
from jax import lax
from jax.experimental import pallas as pl
from jax.experimental.pallas import tpu as pltpu

F32 = jnp.float32
BF16 = jnp.bfloat16

D_MODEL = 1024
DEPTH = 2
GRID_W = 64
EPS = 1e-6
HEADS_A = 4
DK_A = 128
DV_A = 128
CONV_K = 5
CHUNK = 64
HEADS_B = 4
DK_B = 64
DV_B = 128
HEADS_C = 8
KV_HEADS_C = 2
HEAD_DIM_C = 64
GROUP_C = HEADS_C // KV_HEADS_C
WINDOW = 128
Q_BLOCK = 128
ROPE_BASE = 10000.0
N_EXPERTS = 16
D_EXPERT = 512
EC_CAPACITY = 2
ADA_CHUNKS = 6

LANE = 128
MOD_ROWS = 8
COND_ROWS = 16

T_GA, T_GB, T_GC = 0, 8, 16
T_QA, T_KA, T_VA, T_ZA = 24, 28, 32, 36
T_QB, T_KB, T_VB, T_OB = 40, 42, 44, 48
T_QC, T_KC, T_VC = 52, 56, 57
T_GATES = 58
N_TILES = 60
N_PROJ = N_TILES * LANE
GATE_STRIDE = 8
GATE_B_OFF = HEADS_A * GATE_STRIDE

VMEM_LIMIT = 48 * 1024 * 1024


def _cparams(*sem):
    return pltpu.CompilerParams(dimension_semantics=sem, vmem_limit_bytes=VMEM_LIMIT)


def _bf(x):
    return x.astype(BF16)


def _mm(a, b):
    return jnp.dot(_bf(a), _bf(b), preferred_element_type=F32)


def _mm_nt(a, b):
    return lax.dot_general(_bf(a), _bf(b), (((1,), (1,)), ((), ())), preferred_element_type=F32)


def _mm_tn(a, b):
    return lax.dot_general(_bf(a), _bf(b), (((0,), (0,)), ((), ())), preferred_element_type=F32)


def _split2(x):
    hi = _bf(x)
    return hi, _bf(x - hi.astype(F32))


def _split3(x):
    hi = _bf(x)
    r = x - hi.astype(F32)
    mid = _bf(r)
    return hi, mid, _bf(r - mid.astype(F32))


def _mm_sel(sel, x):
    hi, mid, lo = _split3(x)
    d = functools.partial(jnp.dot, preferred_element_type=F32)
    return (d(sel, hi) + d(sel, mid)) + d(sel, lo)


def _mm_hi(a, b):
    ah, al = _split2(a)
    bh, bl = _split2(b)
    d = functools.partial(jnp.dot, preferred_element_type=F32)
    return d(ah, bh) + (d(ah, bl) + d(al, bh))


def _sigmoid(x):
    return 1.0 / (1.0 + jnp.exp(-x))


def _silu(x):
    return x * _sigmoid(x)


def _softplus(x):
    return jnp.maximum(x, 0.0) + jnp.log(1.0 + jnp.exp(-jnp.abs(x)))


def _rms(x, g):
    return x * lax.rsqrt(jnp.mean(x * x, axis=-1, keepdims=True) + EPS) * g


def _chunk_masks(backward):
    ri = lax.broadcasted_iota(jnp.int32, (CHUNK, CHUNK), 0)
    ci = lax.broadcasted_iota(jnp.int32, (CHUNK, CHUNK), 1)
    if backward:
        return ri <= ci, ri < ci, ri == ci
    return ri >= ci, ri > ci, ri == ci


def _mod_kernel(c_ref, w_ref, b_ref, o_ref):
    o_ref[0] = _mm(_silu(c_ref[...]), w_ref[0]) + b_ref[0]


def _modulation(cond, ada_w, ada_b):
    n_out = ADA_CHUNKS * D_MODEL
    tn = 512
    return pl.pallas_call(
        _mod_kernel,
        grid=(DEPTH, n_out // tn),
        in_specs=[
            pl.BlockSpec((COND_ROWS, D_MODEL), lambda l, j: (0, 0)),
            pl.BlockSpec((1, D_MODEL, tn), lambda l, j: (l, 0, j)),
            pl.BlockSpec((1, 1, tn), lambda l, j: (l, 0, j)),
        ],
        out_specs=pl.BlockSpec((1, COND_ROWS, tn), lambda l, j: (l, 0, j)),
        out_shape=jax.ShapeDtypeStruct((DEPTH, COND_ROWS, n_out), F32),
        compiler_params=_cparams("parallel", "parallel"),
    )(cond, ada_w, ada_b.reshape(DEPTH, 1, n_out))


def _in_proj_kernel(x_ref, mod_ref, g_ref, w_ref, o_ref, h_scr):
    @pl.when(pl.program_id(1) == 0)
    def _():
        m = mod_ref[0]
        h_scr[...] = _bf(_rms(x_ref[...], g_ref[0]) * (1.0 + m[1:2]) + m[0:1])

    o_ref[...] = jnp.dot(h_scr[...], w_ref[0], preferred_element_type=F32)


def _in_proj(x2d, mod, norm_g, w_in, layer, seq, per_request):
    m_rows = x2d.shape[0]
    tm = min(1024, m_rows)
    tn = 768
    if per_request:
        assert seq % tm == 0
    return pl.pallas_call(
        _in_proj_kernel,
        grid=(m_rows // tm, N_PROJ // tn),
        in_specs=[
            pl.BlockSpec((tm, D_MODEL), lambda i, j: (i, 0)),
            pl.BlockSpec((1, MOD_ROWS, D_MODEL), lambda i, j: ((i * tm) // seq if per_request else 0, 0, 0)),
            pl.BlockSpec((1, 1, D_MODEL), lambda i, j: (layer, 0, 0)),
            pl.BlockSpec((1, D_MODEL, tn), lambda i, j: (layer, 0, j)),
        ],
        out_specs=pl.BlockSpec((tm, tn), lambda i, j: (i, j)),
        out_shape=jax.ShapeDtypeStruct((m_rows, N_PROJ), F32),
        scratch_shapes=[pltpu.VMEM((tm, D_MODEL), BF16)],
        compiler_params=_cparams("parallel", "arbitrary"),
    )(x2d, mod, norm_g.reshape(DEPTH, 1, D_MODEL), w_in)


PREP_CHUNKS = 2
INV_GROUP = 16


def _gate_dense(gt_parts, lane_index):
    row = lax.broadcasted_iota(jnp.int32, (LANE, LANE), 0)
    sel = jnp.where(row == lane_index, 1.0, 0.0).astype(BF16)
    hi, mid, lo = gt_parts
    d = functools.partial(jnp.dot, preferred_element_type=F32)
    return (d(hi, sel) + d(mid, sel)) + d(lo, sel)


def _delta_kernel(q_ref, k_ref, v_ref, z_ref, gt_ref, cq_ref, ck_ref, cv_ref, par_ref, ng_ref, *rest,
                  seq, has_init, emit_state):
    rest = list(rest)
    s0_ref = rest.pop(0) if has_init else None
    y_ref = rest.pop(0)
    st_ref = rest.pop(0) if emit_state else None
    qs, ks, vs, gates, s_scr, o_scr, u_scr, wq_scr, akd_scr, dk_scr, pw_scr, inv_scr, rhs_scr = rest
    head = pl.program_id(1)
    n_chunks = seq // CHUNK
    rows_t = lax.broadcasted_iota(jnp.int32, (seq, 1), 0)

    def conv_silu(x_ref, w_ref):
        x = x_ref[...]
        w = w_ref[0]
        acc = x * w[CONV_K // 2:CONV_K // 2 + 1, :]
        for j in range(CONV_K):
            s = j - CONV_K // 2
            if s == 0:
                continue
            shifted = pltpu.roll(x, (-s) % seq, 0)
            ok = (rows_t + s >= 0) & (rows_t + s < seq)
            acc = acc + jnp.where(ok, shifted, 0.0) * w[j:j + 1, :]
        return _silu(acc)

    def l2n(x):
        return x * lax.rsqrt(jnp.sum(x * x, axis=-1, keepdims=True) + EPS)

    qs[...] = l2n(conv_silu(q_ref, cq_ref)) * (DK_A ** -0.5)
    ks[...] = l2n(conv_silu(k_ref, ck_ref))
    vs[...] = conv_silu(v_ref, cv_ref)

    gt_parts = _split3(gt_ref[...])
    par = par_ref[0]
    base = head * GATE_STRIDE
    for d in range(2):
        gates[d] = _sigmoid(_gate_dense(gt_parts, base + d))
        gates[2 + d] = -jnp.exp(par[d:d + 1, :]) * _softplus(_gate_dense(gt_parts, base + 2 + d) + par[2 + d:3 + d, :])
        s_scr[d] = s0_ref[0, 0, d, 0] if has_init else jnp.zeros((DK_A, DV_A), F32)

    def chunk_rows(c):
        return pl.ds(pl.multiple_of(c * CHUNK, CHUNK), CHUNK)

    n_prob = 2 * n_chunks
    group = min(INV_GROUP, n_prob)
    ri = lax.broadcasted_iota(jnp.int32, (CHUNK, CHUNK), 0)
    ci = lax.broadcasted_iota(jnp.int32, (CHUNK, CHUNK), 1)
    dot = functools.partial(jnp.dot, preferred_element_type=F32)

    def setup_body(i, carry):
        loaded = []
        for cc in range(PREP_CHUNKS):
            c = i * PREP_CHUNKS + cc
            rows = chunk_rows(c)
            loaded.append((c, qs[rows, :], ks[rows, :], vs[rows, :],
                           [(gates[d, rows, :], gates[2 + d, rows, :]) for d in range(2)]))
        products = []
        for c, q, k, v, gate_cols in loaded:
            per_dir = []
            for d in range(2):
                g = gate_cols[d][1]
                incl, strict, _ = _chunk_masks(d == 1)
                rhs = jnp.concatenate(
                    [jnp.where(strict, g[:, :CHUNK], 0.0), jnp.zeros((CHUNK, LANE - CHUNK), F32), g], axis=1)
                per_dir.append(_mm_sel(jnp.where(incl, 1.0, 0.0).astype(BF16), rhs))
            products.append((_mm_nt(k, k), _mm_nt(q, k), per_dir))
        results = []
        for (c, q, k, v, gate_cols), (kk, qk, per_dir) in zip(loaded, products):
            for d in range(2):
                bt, g = gate_cols[d]
                incl, strict, _ = _chunk_masks(d == 1)
                cs = per_dir[d]
                gc = cs[:, LANE:]
                g_last = gc[0:1, :] if d == 1 else gc[CHUNK - 1:CHUNK, :]
                egc = jnp.exp(gc)
                dec = jnp.where(incl, jnp.exp(cs[:, :CHUNK]), 0.0)
                low = jnp.where(strict, bt[:, :CHUNK] * kk * dec, 0.0)
                rhs2 = jnp.concatenate([v * bt, k * (bt * egc)], axis=1)
                akd = _bf(jnp.concatenate([qk * dec, (k * jnp.exp(g_last - gc)).T], axis=0))
                results.append((d * n_chunks + c, low, rhs2, akd, _bf(q * egc), jnp.exp(g_last)))
        for p, low, rhs2, akd, qd, decay in results:
            pw_scr[p] = low
            inv_scr[p] = jnp.where(ri == ci, 1.0, 0.0) - jnp.where((ri // 2) == (ci // 2), low, 0.0)
            rhs_scr[p] = rhs2
            akd_scr[p] = akd
            wq_scr[p, CHUNK:, :] = qd
            dk_scr[p] = decay
        return carry

    lax.fori_loop(0, n_chunks // PREP_CHUNKS, setup_body, 0)

    def doubling_pass(size):
        joins = ((ri // (2 * size)) == (ci // (2 * size))) & ((ri // size) != (ci // size))

        def body(i, carry):
            loaded = []
            for j in range(group):
                p = i * group + j
                loaded.append((p, pw_scr[p], inv_scr[p]))
            partial = [(p, inv, _mm_hi(inv, jnp.where(joins, low, 0.0))) for p, low, inv in loaded]
            results = [(p, inv - _mm_hi(t, inv)) for p, inv, t in partial]
            for p, new_inv in results:
                inv_scr[p] = new_inv
            return carry

        lax.fori_loop(0, n_prob // group, body, 0)

    size = 2
    while size < CHUNK:
        doubling_pass(size)
        size *= 2

    def solve_body(i, carry):
        loaded = []
        for j in range(group):
            p = i * group + j
            loaded.append((p, inv_scr[p], rhs_scr[p]))
        results = [(p, _mm_hi(inv, rhs2)) for p, inv, rhs2 in loaded]
        for p, sol in results:
            u_scr[p] = sol[:, :DV_A]
            wq_scr[p, :CHUNK, :] = _bf(sol[:, DV_A:])
        return carry

    lax.fori_loop(0, n_prob // group, solve_body, 0)

    def scan_body(i, carry):
        loaded = []
        for d in range(2):
            c = n_chunks - 1 - i if d == 1 else i
            p = d * n_chunks + c
            loaded.append((c, s_scr[d], u_scr[p], wq_scr[p], akd_scr[p], dk_scr[p]))
        results = []
        for c, state, u, wq, akd, decay in loaded:
            ws = dot(wq, _bf(state))
            av = dot(akd, _bf(u - ws[:CHUNK]))
            results.append((c, ws[CHUNK:] + av[:CHUNK], decay * state + av[CHUNK:]))
        for d, (c, o, state) in enumerate(results):
            o_scr[d, chunk_rows(c), :] = o
            s_scr[d] = state
        return carry

    lax.fori_loop(0, n_chunks, scan_body, 0)

    z = z_ref[...]
    y_ref[...] = _rms(o_scr[0] + o_scr[1], ng_ref[0]) * _silu(z)
    if emit_state:
        st_ref[0, 0, 0] = s_scr[0]
        st_ref[0, 1, 0] = s_scr[1]


def _delta_mixer(proj, conv_w, par, norm_g, layer, batch, seq, state0):
    has_init = state0 is not None
    emit_state = not has_init
    col = lambda tile: (lambda r, h: (r, tile + h))
    cw = lambda part: (lambda r, h: (layer, 0, part * HEADS_A + h))
    in_specs = [
        pl.BlockSpec((seq, LANE), col(T_QA)),
        pl.BlockSpec((seq, LANE), col(T_KA)),
        pl.BlockSpec((seq, LANE), col(T_VA)),
        pl.BlockSpec((seq, LANE), col(T_ZA)),
        pl.BlockSpec((seq, LANE), lambda r, h: (r, T_GATES)),
        pl.BlockSpec((1, CONV_K, LANE), cw(0)),
        pl.BlockSpec((1, CONV_K, LANE), cw(1)),
        pl.BlockSpec((1, CONV_K, LANE), cw(2)),
        pl.BlockSpec((1, 4, LANE), lambda r, h: (layer * HEADS_A + h, 0, 0)),
        pl.BlockSpec((1, 1, DV_A), lambda r, h: (layer, 0, 0)),
    ]
    args = [proj, proj, proj, proj, proj, conv_w, conv_w, conv_w, par, norm_g.reshape(DEPTH, 1, DV_A)]
    if has_init:
        in_specs.append(pl.BlockSpec((1, 1, 2, 1, DK_A, DV_A), lambda r, h: (r, layer, 0, h, 0, 0)))
        args.append(state0)
    out_specs = [pl.BlockSpec((seq, DV_A), lambda r, h: (r, h))]
    out_shape = [jax.ShapeDtypeStruct((batch * seq, HEADS_A * DV_A), F32)]
    if emit_state:
        out_specs.append(pl.BlockSpec((1, 2, 1, DK_A, DV_A), lambda r, h: (r, 0, h, 0, 0)))
        out_shape.append(jax.ShapeDtypeStruct((batch, 2, HEADS_A, DK_A, DV_A), F32))
    n_prob = 2 * (seq // CHUNK)
    outs = pl.pallas_call(
        functools.partial(_delta_kernel, seq=seq, has_init=has_init, emit_state=emit_state),
        grid=(batch, HEADS_A),
        in_specs=in_specs,
        out_specs=out_specs,
        out_shape=out_shape,
        scratch_shapes=[
            pltpu.VMEM((seq, DK_A), F32), pltpu.VMEM((seq, DK_A), F32), pltpu.VMEM((seq, DV_A), F32),
            pltpu.VMEM((4, seq, LANE), F32), pltpu.VMEM((2, DK_A, DV_A), F32), pltpu.VMEM((2, seq, DV_A), F32),
            pltpu.VMEM((n_prob, CHUNK, DV_A), F32), pltpu.VMEM((n_prob, 2 * CHUNK, DK_A), BF16),
            pltpu.VMEM((n_prob, CHUNK + DK_A, CHUNK), BF16), pltpu.VMEM((n_prob, 1, DV_A), F32),
            pltpu.VMEM((n_prob, CHUNK, CHUNK), F32), pltpu.VMEM((n_prob, CHUNK, CHUNK), F32),
            pltpu.VMEM((n_prob, CHUNK, DV_A + DK_A), F32),
        ],
        compiler_params=_cparams("parallel", "parallel"),
    )(*args)
    return (outs[0], outs[1]) if emit_state else (outs[0], None)


HEADS_PER_STEP_B = 4


def _mlstm_kernel(q_ref, k_ref, v_ref, og_ref, gt_ref, par_ref, ng_ref, *rest, seq, has_init, emit_state):
    rest = list(rest)
    if has_init:
        c0_ref, n0_ref, m0_ref = rest[:3]
        rest = rest[3:]
    y_ref = rest.pop(0)
    if emit_state:
        co_ref, no_ref, mo_ref = rest[:3]
        rest = rest[3:]
    gates, c_scr, n_scr, m_scr, h_scr = rest
    pair = pl.program_id(1)
    n_chunks = seq // CHUNK
    gt_parts = _split3(gt_ref[...])

    for j in range(HEADS_PER_STEP_B):
        par = par_ref[j]
        base = GATE_B_OFF + (pair * HEADS_PER_STEP_B + j) * GATE_STRIDE
        for d in range(2):
            gates[j * 4 + d] = _gate_dense(gt_parts, base + d) + par[d:d + 1, :]
            gates[j * 4 + 2 + d] = -_softplus(-(_gate_dense(gt_parts, base + 2 + d) + par[2 + d:3 + d, :]))
            idx = j * 2 + d
            if has_init:
                c_scr[idx] = c0_ref[0, 0, d, j]
                n_scr[idx] = n0_ref[0, 0, d, j]
                m_scr[idx] = m0_ref[0, 0, d, j]
            else:
                c_scr[idx] = jnp.zeros((DK_B, DV_B), F32)
                n_scr[idx] = jnp.zeros((1, DK_B), F32)
                m_scr[idx] = jnp.zeros((1, LANE), F32)

    n_chain = 2 * HEADS_PER_STEP_B

    def scan_body(i, carry):
        loaded = []
        for chain in range(n_chain):
            j, d = chain // 2, chain % 2
            c = n_chunks - 1 - i if d == 1 else i
            rows = pl.ds(pl.multiple_of(c * CHUNK, CHUNK), CHUNK)
            loaded.append((chain, rows, q_ref[rows, j * DK_B:(j + 1) * DK_B],
                           k_ref[rows, j * DK_B:(j + 1) * DK_B] * (DK_B ** -0.5),
                           v_ref[rows, j * DV_B:(j + 1) * DV_B],
                           gates[j * 4 + d, rows, :], gates[j * 4 + 2 + d, rows, :],
                           c_scr[chain], n_scr[chain], m_scr[chain]))
        stage1 = []
        for chain, rows, q, k, v, ig, lf, cmat, nvec, m_prev in loaded:
            incl, strict, diag = _chunk_masks(chain % 2 == 1)
            rhs = jnp.concatenate(
                [jnp.where(strict, lf[:, :CHUNK], 0.0) + jnp.where(diag, ig[:, :CHUNK], 0.0),
                 jnp.zeros((CHUNK, LANE - CHUNK), F32), lf], axis=1)
            stage1.append((_mm_sel(jnp.where(incl, 1.0, 0.0).astype(BF16), rhs), _mm_nt(q, k), _mm(q, cmat)))
        stage2 = []
        for (chain, rows, q, k, v, ig, lf, cmat, nvec, m_prev), (cs, qk, qc) in zip(loaded, stage1):
            incl, _, _ = _chunk_masks(chain % 2 == 1)
            bc = cs[:, LANE:]
            b_last = bc[0:1, :] if chain % 2 == 1 else bc[CHUNK - 1:CHUNK, :]
            d_log = jnp.where(incl, cs[:, :CHUNK], -jnp.inf)
            d_max = jnp.max(d_log, axis=1, keepdims=True)
            tok = b_last - bc + ig
            m_new = jnp.maximum(b_last + m_prev, jnp.max(tok, axis=0, keepdims=True))
            w_prev = jnp.exp(b_last + m_prev - m_new)
            kw = k * jnp.exp(tok - m_new)[:, :DK_B]
            stage2.append((bc, d_log, d_max, m_new, w_prev, kw))
        stage3 = []
        for (chain, rows, q, k, v, ig, lf, cmat, nvec, m_prev), (cs, qk, qc), (bc, d_log, d_max, m_new, w_prev, kw) \
                in zip(loaded, stage1, stage2):
            m_t = jnp.maximum(bc + m_prev, d_max)
            w_inter = jnp.exp(bc + m_prev - m_t)
            pm = jnp.exp(d_log - m_t[:, :CHUNK]) * qk
            den = jnp.sum(w_inter[:, :DK_B] * (q * nvec) + pm, axis=1, keepdims=True)
            stage3.append((m_t, w_inter, den, _mm(pm, v), _mm_tn(kw, v)))
        results = []
        for (chain, rows, q, k, v, ig, lf, cmat, nvec, m_prev), (cs, qk, qc), (bc, d_log, d_max, m_new, w_prev, kw), \
                (m_t, w_inter, den, pv, inc) in zip(loaded, stage1, stage2, stage3):
            results.append((chain, rows, (w_inter * qc + pv) / jnp.maximum(jnp.abs(den), jnp.exp(-m_t)),
                            w_prev * cmat + inc,
                            w_prev[:, :DK_B] * nvec + jnp.sum(kw, axis=0, keepdims=True), m_new))
        for chain, rows, h, cmat, nvec, m_new in results:
            h_scr[chain, rows, :] = h
            c_scr[chain] = cmat
            n_scr[chain] = nvec
            m_scr[chain] = m_new
        return carry

    lax.fori_loop(0, n_chunks, scan_body, 0)

    og = og_ref[...]
    for j in range(HEADS_PER_STEP_B):
        h = h_scr[j * 2] + h_scr[j * 2 + 1]
        y_ref[:, j * DV_B:(j + 1) * DV_B] = _rms(h, ng_ref[0]) * _sigmoid(og[:, j * DV_B:(j + 1) * DV_B])
        if emit_state:
            for d in range(2):
                co_ref[0, d, j] = c_scr[j * 2 + d]
                no_ref[0, d, j] = n_scr[j * 2 + d]
                mo_ref[0, d, j] = m_scr[j * 2 + d]


def _mlstm_mixer(proj, par, norm_g, layer, batch, seq, state0):
    has_init = state0 is not None
    emit_state = not has_init
    hp = HEADS_PER_STEP_B
    n_pairs = HEADS_B // hp
    in_specs = [
        pl.BlockSpec((seq, hp * DK_B), lambda r, p: (r, T_QB * LANE // (hp * DK_B) + p)),
        pl.BlockSpec((seq, hp * DK_B), lambda r, p: (r, T_KB * LANE // (hp * DK_B) + p)),
        pl.BlockSpec((seq, hp * DV_B), lambda r, p: (r, T_VB * LANE // (hp * DV_B) + p)),
        pl.BlockSpec((seq, hp * DV_B), lambda r, p: (r, T_OB * LANE // (hp * DV_B) + p)),
        pl.BlockSpec((seq, LANE), lambda r, p: (r, T_GATES)),
        pl.BlockSpec((hp, 4, LANE), lambda r, p: (layer * n_pairs + p, 0, 0)),
        pl.BlockSpec((1, 1, DV_B), lambda r, p: (layer, 0, 0)),
    ]
    args = [proj, proj, proj, proj, proj, par, norm_g.reshape(DEPTH, 1, DV_B)]
    if has_init:
        c0, n0, m0 = state0
        in_specs += [
            pl.BlockSpec((1, 1, 2, hp, DK_B, DV_B), lambda r, p: (r, layer, 0, p, 0, 0)),
            pl.BlockSpec((1, 1, 2, hp, 1, DK_B), lambda r, p: (r, layer, 0, p, 0, 0)),
            pl.BlockSpec((1, 1, 2, hp, 1, LANE), lambda r, p: (r, layer, 0, p, 0, 0)),
        ]
        args += [c0, n0, m0]
    out_specs = [pl.BlockSpec((seq, hp * DV_B), lambda r, p: (r, p))]
    out_shape = [jax.ShapeDtypeStruct((batch * seq, HEADS_B * DV_B), F32)]
    if emit_state:
        out_specs += [
            pl.BlockSpec((1, 2, hp, DK_B, DV_B), lambda r, p: (r, 0, p, 0, 0)),
            pl.BlockSpec((1, 2, hp, 1, DK_B), lambda r, p: (r, 0, p, 0, 0)),
            pl.BlockSpec((1, 2, hp, 1, LANE), lambda r, p: (r, 0, p, 0, 0)),
        ]
        out_shape += [
            jax.ShapeDtypeStruct((batch, 2, HEADS_B, DK_B, DV_B), F32),
            jax.ShapeDtypeStruct((batch, 2, HEADS_B, 1, DK_B), F32),
            jax.ShapeDtypeStruct((batch, 2, HEADS_B, 1, LANE), F32),
        ]
    outs = pl.pallas_call(
        functools.partial(_mlstm_kernel, seq=seq, has_init=has_init, emit_state=emit_state),
        grid=(batch, n_pairs),
        in_specs=in_specs,
        out_specs=out_specs,
        out_shape=out_shape,
        scratch_shapes=[
            pltpu.VMEM((4 * hp, seq, LANE), F32), pltpu.VMEM((2 * hp, DK_B, DV_B), F32),
            pltpu.VMEM((2 * hp, 1, DK_B), F32), pltpu.VMEM((2 * hp, 1, LANE), F32),
            pltpu.VMEM((2 * hp, seq, DV_B), F32),
        ],
        compiler_params=_cparams("parallel", "parallel"),
    )(*args)
    if emit_state:
        return outs[0], (outs[1], outs[2][:, :, :, 0, :], outs[3][:, :, :, 0, 0])
    return outs[0], None


def _stack_heads(q, kv_head):
    return jnp.concatenate(
        [q[:, (kv_head * GROUP_C + g) * HEAD_DIM_C:(kv_head * GROUP_C + g + 1) * HEAD_DIM_C] for g in range(GROUP_C)],
        axis=0)


def _sink_column(sink, kv_head, rows):
    return jnp.concatenate(
        [jnp.broadcast_to(sink[kv_head * GROUP_C + g:kv_head * GROUP_C + g + 1, 0:1], (rows, 1))
         for g in range(GROUP_C)], axis=0)


def _unstack_heads(per_kv, rows):
    return jnp.concatenate(
        [o[g * rows:(g + 1) * rows, :] for o in per_kv for g in range(GROUP_C)], axis=1)


def _ctx_attn_kernel(q_ref, k_ref, v_ref, sink_ref, o_ref, *, seq):
    q = q_ref[...]
    k = k_ref[...]
    v = v_ref[...]
    sink = sink_ref[0]
    scale = HEAD_DIM_C ** -0.5
    outs = []
    for kv in range(KV_HEADS_C):
        lanes = slice(kv * HEAD_DIM_C, (kv + 1) * HEAD_DIM_C)
        s = _mm_nt(_stack_heads(q, kv), k[:, lanes]) * scale
        sk = _sink_column(sink, kv, seq)
        m = jnp.maximum(jnp.max(s, axis=1, keepdims=True), sk)
        e = jnp.exp(s - m)
        den = jnp.sum(e, axis=1, keepdims=True) + jnp.exp(sk - m)
        outs.append(_mm(e / den, v[:, lanes]))
    o_ref[...] = _unstack_heads(outs, seq)


def _ctx_attention(proj, sink, layer, batch, seq):
    width = HEADS_C * HEAD_DIM_C
    return pl.pallas_call(
        functools.partial(_ctx_attn_kernel, seq=seq),
        grid=(batch,),
        in_specs=[
            pl.BlockSpec((seq, width), lambda r: (r, T_QC * LANE // width)),
            pl.BlockSpec((seq, LANE), lambda r: (r, T_KC)),
            pl.BlockSpec((seq, LANE), lambda r: (r, T_VC)),
            pl.BlockSpec((1, HEADS_C, LANE), lambda r: (layer, 0, 0)),
        ],
        out_specs=pl.BlockSpec((seq, width), lambda r: (r, 0)),
        out_shape=jax.ShapeDtypeStruct((batch * seq, width), F32),
        compiler_params=_cparams("parallel"),
    )(proj, proj, proj, sink)


def _rope(x, cos, sin):
    quarter = HEAD_DIM_C // 4
    lane = lax.broadcasted_iota(jnp.int32, (1, LANE), 1)
    first = (lane % (2 * quarter)) < quarter
    partner = jnp.where(first, -pltpu.roll(x, LANE - quarter, 1), pltpu.roll(x, quarter, 1))
    return x * cos + partner * sin


def _latent_attn_kernel(q_ref, k_ref, v_ref, ck_ref, cv_ref, cq_ref, sq_ref, cos_ref, sin_ref, sink_ref, o_ref, *, seq):
    blk = pl.program_id(1)
    span = Q_BLOCK + 2 * WINDOW
    start = blk * Q_BLOCK
    k_start = pl.multiple_of(jnp.clip(start - WINDOW, 0, seq - span), Q_BLOCK)
    win = pl.ds(k_start, span)
    cq = cq_ref[...]
    sq = sq_ref[...]
    q = jnp.concatenate(
        [_rope(q_ref[:, s * LANE:(s + 1) * LANE], cq, sq) for s in range(HEADS_C * HEAD_DIM_C // LANE)], axis=1)
    k = _rope(k_ref[win, :], cos_ref[win, :], sin_ref[win, :])
    v = v_ref[win, :]
    ck = ck_ref[0, 0]
    cv = cv_ref[0, 0]
    sink = sink_ref[0]
    scale = HEAD_DIM_C ** -0.5
    q_pos = start + lax.broadcasted_iota(jnp.int32, (GROUP_C * Q_BLOCK, 1), 0) % Q_BLOCK
    k_pos = k_start + lax.broadcasted_iota(jnp.int32, (1, span), 1)
    valid = jnp.abs(q_pos - k_pos) <= WINDOW
    outs = []
    for kv in range(KV_HEADS_C):
        lanes = slice(kv * HEAD_DIM_C, (kv + 1) * HEAD_DIM_C)
        qs = _stack_heads(q, kv)
        sl = jnp.where(valid, _mm_nt(qs, k[:, lanes]) * scale, -jnp.inf)
        sc = _mm_nt(qs, ck[:, lanes]) * scale
        sk = _sink_column(sink, kv, Q_BLOCK)
        m = jnp.maximum(jnp.maximum(jnp.max(sl, axis=1, keepdims=True), jnp.max(sc, axis=1, keepdims=True)), sk)
        el = jnp.exp(sl - m)
        ec = jnp.exp(sc - m)
        den = jnp.sum(el, axis=1, keepdims=True) + jnp.sum(ec, axis=1, keepdims=True) + jnp.exp(sk - m)
        outs.append(_mm(el / den, v[:, lanes]) + _mm(ec / den, cv[:, lanes]))
    o_ref[...] = _unstack_heads(outs, Q_BLOCK)


def _latent_attention(proj, cache_k, cache_v, cos, sin, sink, layer, batch, seq):
    width = HEADS_C * HEAD_DIM_C
    n_blk = seq // Q_BLOCK
    past = cache_k.shape[2]
    return pl.pallas_call(
        functools.partial(_latent_attn_kernel, seq=seq),
        grid=(batch, n_blk),
        in_specs=[
            pl.BlockSpec((Q_BLOCK, width), lambda r, i: (r * n_blk + i, T_QC * LANE // width)),
            pl.BlockSpec((seq, LANE), lambda r, i: (r, T_KC)),
            pl.BlockSpec((seq, LANE), lambda r, i: (r, T_VC)),
            pl.BlockSpec((1, 1, past, LANE), lambda r, i: (r, layer, 0, 0)),
            pl.BlockSpec((1, 1, past, LANE), lambda r, i: (r, layer, 0, 0)),
            pl.BlockSpec((Q_BLOCK, LANE), lambda r, i: (i, 0)),
            pl.BlockSpec((Q_BLOCK, LANE), lambda r, i: (i, 0)),
            pl.BlockSpec((seq, LANE), lambda r, i: (0, 0)),
            pl.BlockSpec((seq, LANE), lambda r, i: (0, 0)),
            pl.BlockSpec((1, HEADS_C, LANE), lambda r, i: (layer, 0, 0)),
        ],
        out_specs=pl.BlockSpec((Q_BLOCK, width), lambda r, i: (r * n_blk + i, 0)),
        out_shape=jax.ShapeDtypeStruct((batch * seq, width), F32),
        compiler_params=_cparams("parallel", "parallel"),
    )(proj, proj, proj, cache_k, cache_v, cos, sin, cos, sin, sink)


def _rope_tables(seq):
    quarter = HEAD_DIM_C // 4
    pos = jnp.arange(seq)
    row = (pos // GRID_W).astype(F32)
    col = (pos % GRID_W).astype(F32)
    inv = jnp.power(ROPE_BASE, -jnp.arange(quarter, dtype=F32) / quarter)
    ang_row = row[:, None] * inv[None, :]
    ang_col = col[:, None] * inv[None, :]
    ang = jnp.concatenate([ang_row, ang_row, ang_col, ang_col], axis=1)
    ang = jnp.concatenate([ang] * (LANE // HEAD_DIM_C), axis=1)
    return jnp.cos(ang), jnp.sin(ang)


def _mix_kernel(x_ref, ya_ref, yb_ref, yc_ref, ga_ref, gb_ref, gc_ref, mod_ref, n2_ref,
                wa_ref, wb_ref, wc_ref, wo_ref, wr_ref, xo_ref, h_ref, aff_ref, afft_ref):
    m = mod_ref[0]
    mixed = (_sigmoid(ga_ref[...]) * _mm(ya_ref[...], wa_ref[0])
             + _sigmoid(gb_ref[...]) * _mm(yb_ref[...], wb_ref[0])
             + _sigmoid(gc_ref[...]) * _mm(yc_ref[...], wc_ref[0]))
    x = x_ref[...] + m[2:3] * _mm(mixed, wo_ref[0])
    xo_ref[...] = x
    h = _bf(_rms(x, n2_ref[0]) * (1.0 + m[4:5]) + m[3:4])
    h_ref[...] = h
    logits = jnp.dot(h, wr_ref[0], preferred_element_type=F32)
    lane = lax.broadcasted_iota(jnp.int32, (1, LANE), 1)
    logits = jnp.where(lane < N_EXPERTS, logits, -jnp.inf)
    e = jnp.exp(logits - jnp.max(logits, axis=1, keepdims=True))
    aff = e / jnp.sum(e, axis=1, keepdims=True)
    aff_ref[...] = aff
    afft_ref[...] = aff.T[:N_EXPERTS, :]


def _mix(x2d, ya, yb, yc, proj, mod, norm2_g, wa, wb, wc, wo, wr, layer, seq, per_request):
    m_rows = x2d.shape[0]
    tm = 256
    gate_blk = lambda tile: (lambda i: (i, tile * LANE // D_MODEL))
    wspec = lambda w: pl.BlockSpec((1,) + w.shape[1:], lambda i: (layer, 0, 0))
    branch = pl.BlockSpec((tm, ya.shape[1]), lambda i: (i, 0))
    return pl.pallas_call(
        _mix_kernel,
        grid=(m_rows // tm,),
        in_specs=[
            pl.BlockSpec((tm, D_MODEL), lambda i: (i, 0)),
            branch, branch, branch,
            pl.BlockSpec((tm, D_MODEL), gate_blk(T_GA)),
            pl.BlockSpec((tm, D_MODEL), gate_blk(T_GB)),
            pl.BlockSpec((tm, D_MODEL), gate_blk(T_GC)),
            pl.BlockSpec((1, MOD_ROWS, D_MODEL), lambda i: ((i * tm) // seq if per_request else 0, 0, 0)),
            pl.BlockSpec((1, 1, D_MODEL), lambda i: (layer, 0, 0)),
            wspec(wa), wspec(wb), wspec(wc), wspec(wo), wspec(wr),
        ],
        out_specs=[
            pl.BlockSpec((tm, D_MODEL), lambda i: (i, 0)),
            pl.BlockSpec((tm, D_MODEL), lambda i: (i, 0)),
            pl.BlockSpec((tm, LANE), lambda i: (i, 0)),
            pl.BlockSpec((N_EXPERTS, tm), lambda i: (0, i)),
        ],
        out_shape=[
            jax.ShapeDtypeStruct((m_rows, D_MODEL), F32),
            jax.ShapeDtypeStruct((m_rows, D_MODEL), BF16),
            jax.ShapeDtypeStruct((m_rows, LANE), F32),
            jax.ShapeDtypeStruct((N_EXPERTS, m_rows), F32),
        ],
        compiler_params=_cparams("parallel"),
    )(x2d, ya, yb, yc, proj, proj, proj, mod, norm2_g.reshape(DEPTH, 1, D_MODEL), wa, wb, wc, wo, wr)


SCATTER_K = 512


def _gather_kernel(aff_ref, afft_ref, h_ref, xs_ref, gate_ref, rankc_ref, rank_scr, onehot_scr, col_scr, *,
                   seq, cap):
    n_blk = seq // LANE
    ri = lax.broadcasted_iota(jnp.int32, (LANE, LANE), 0)
    ci = lax.broadcasted_iota(jnp.int32, (LANE, LANE), 1)
    lane = lax.broadcasted_iota(jnp.int32, (1, LANE), 1)
    slot = lax.broadcasted_iota(jnp.int32, (cap, 1), 0).astype(F32)

    def expert_body(e, carry):
        a_row = afft_ref[pl.ds(e, 1), :]
        a_col = jnp.sum(jnp.where(lane == e, aff_ref[...], 0.0), axis=1, keepdims=True)
        col_scr[...] = jnp.broadcast_to(a_col, (seq, LANE))
        rank_parts = []
        for tb in range(n_blk):
            a_t = a_row[:, tb * LANE:(tb + 1) * LANE]
            count = jnp.zeros((LANE, LANE), F32)
            for sb in range(n_blk):
                a_s = col_scr[sb * LANE:(sb + 1) * LANE, :]
                if sb < tb:
                    beats = a_s >= a_t
                elif sb > tb:
                    beats = a_s > a_t
                else:
                    beats = (a_s > a_t) | ((a_s == a_t) & (ri < ci))
                count = count + jnp.where(beats, 1.0, 0.0)
            rank_parts.append(jnp.sum(count, axis=0, keepdims=True))
        rank = jnp.concatenate(rank_parts, axis=1)
        rank_scr[pl.ds(e, 1), :] = rank
        chosen = rank == slot
        onehot_scr[pl.ds(pl.multiple_of(e * cap, cap), cap), :] = jnp.where(chosen, 1.0, 0.0).astype(BF16)
        gate_ref[e, 0] = jnp.sum(jnp.where(chosen, a_row, 0.0), axis=1, keepdims=True)
        return carry

    lax.fori_loop(0, N_EXPERTS, expert_body, 0)
    ranks = jnp.concatenate([rank_scr[...], jnp.zeros((LANE - N_EXPERTS, seq), F32)], axis=0)
    rankc_ref[...] = ranks.T
    per_group = SCATTER_K // cap
    h = h_ref[...]
    for i in range(N_EXPERTS // per_group):
        rows = jnp.dot(onehot_scr[i * SCATTER_K:(i + 1) * SCATTER_K, :], h, preferred_element_type=F32)
        xs_ref[i * per_group:(i + 1) * per_group, 0] = _bf(rows).reshape(per_group, cap, D_MODEL)


def _gather(aff, afft, h2, batch, seq, cap):
    return pl.pallas_call(
        functools.partial(_gather_kernel, seq=seq, cap=cap),
        grid=(batch,),
        in_specs=[
            pl.BlockSpec((seq, LANE), lambda r: (r, 0)),
            pl.BlockSpec((N_EXPERTS, seq), lambda r: (0, r)),
            pl.BlockSpec((seq, D_MODEL), lambda r: (r, 0)),
        ],
        out_specs=[
            pl.BlockSpec((N_EXPERTS, 1, cap, D_MODEL), lambda r: (0, r, 0, 0)),
            pl.BlockSpec((N_EXPERTS, 1, cap, 1), lambda r: (0, r, 0, 0)),
            pl.BlockSpec((seq, LANE), lambda r: (r, 0)),
        ],
        out_shape=[
            jax.ShapeDtypeStruct((N_EXPERTS, batch, cap, D_MODEL), BF16),
            jax.ShapeDtypeStruct((N_EXPERTS, batch, cap, 1), F32),
            jax.ShapeDtypeStruct((batch * seq, LANE), F32),
        ],
        scratch_shapes=[pltpu.VMEM((N_EXPERTS, seq), F32), pltpu.VMEM((N_EXPERTS * cap, seq), BF16),
                        pltpu.VMEM((seq, LANE), F32)],
        compiler_params=_cparams("parallel"),
    )(aff, afft, h2)


def _ffn_kernel(x_ref, gate_ref, wg_ref, wu_ref, wd_ref, hi_ref, lo_ref):
    x = x_ref[0]
    hid = _silu(_mm(x, wg_ref[0, 0])) * _mm(x, wu_ref[0, 0])
    y = _mm(hid, wd_ref[0, 0]) * gate_ref[0]
    hi, lo = _split2(y)
    hi_ref[0] = hi
    lo_ref[0] = lo


def _expert_ffn(xs, gate, w_gate, w_up, w_down, layer):
    n_rows = xs.shape[1]
    rows = pl.BlockSpec((1, n_rows, D_MODEL), lambda e: (e, 0, 0))
    out = jax.ShapeDtypeStruct((N_EXPERTS, n_rows, D_MODEL), BF16)
    return pl.pallas_call(
        _ffn_kernel,
        grid=(N_EXPERTS,),
        in_specs=[
            rows,
            pl.BlockSpec((1, n_rows, 1), lambda e: (e, 0, 0)),
            pl.BlockSpec((1, 1, D_MODEL, D_EXPERT), lambda e: (layer, e, 0, 0)),
            pl.BlockSpec((1, 1, D_MODEL, D_EXPERT), lambda e: (layer, e, 0, 0)),
            pl.BlockSpec((1, 1, D_EXPERT, D_MODEL), lambda e: (layer, e, 0, 0)),
        ],
        out_specs=[rows, rows],
        out_shape=[out, out],
        compiler_params=_cparams("parallel"),
    )(xs, gate, w_gate, w_up, w_down)


def _scatter_kernel(x_ref, rankc_ref, hi_ref, lo_ref, mod_ref, fg_ref, o_ref, acc, *, cap, final):
    g = pl.program_id(1)
    n_groups = pl.num_programs(1)
    per_lane_tile = LANE // cap
    rank_parts = _split3(rankc_ref[...])
    row = lax.broadcasted_iota(jnp.int32, (LANE, LANE), 0)
    lane = lax.broadcasted_iota(jnp.int32, (LANE, LANE), 1)
    lane_slot = (lax.broadcasted_iota(jnp.int32, (1, LANE), 1) % cap).astype(F32)
    d = functools.partial(jnp.dot, preferred_element_type=F32)
    tiles = []
    for b in range(SCATTER_K // LANE):
        first = (g * (SCATTER_K // LANE) + b) * per_lane_tile
        sel = jnp.where(row == first + lane // cap, 1.0, 0.0).astype(BF16)
        hi, mid, lo = rank_parts
        token_rank = (d(hi, sel) + d(mid, sel)) + d(lo, sel)
        tiles.append(jnp.where(token_rank == lane_slot, 1.0, 0.0).astype(BF16))
    onehot = jnp.concatenate(tiles, axis=1)
    spread = (d(onehot, hi_ref[:, 0].reshape(SCATTER_K, D_MODEL))
              + d(onehot, lo_ref[:, 0].reshape(SCATTER_K, D_MODEL)))

    @pl.when(g == 0)
    def _():
        acc[...] = spread

    @pl.when(g > 0)
    def _():
        acc[...] += spread

    @pl.when(g == n_groups - 1)
    def _():
        x = x_ref[...] + mod_ref[0][5:6] * acc[...]
        o_ref[...] = _rms(x, fg_ref[...]) if final else x


def _scatter(x2d, rankc, ye_hi, ye_lo, mod, final_g, batch, seq, cap, per_request, final):
    experts_per_group = SCATTER_K // cap
    slots = pl.BlockSpec((experts_per_group, 1, cap, D_MODEL), lambda r, g: (g, r, 0, 0))
    return pl.pallas_call(
        functools.partial(_scatter_kernel, cap=cap, final=final),
        grid=(batch, N_EXPERTS // experts_per_group),
        in_specs=[
            pl.BlockSpec((seq, D_MODEL), lambda r, g: (r, 0)),
            pl.BlockSpec((seq, LANE), lambda r, g: (r, 0)),
            slots, slots,
            pl.BlockSpec((1, MOD_ROWS, D_MODEL), lambda r, g: (r if per_request else 0, 0, 0)),
            pl.BlockSpec((1, D_MODEL), lambda r, g: (0, 0)),
        ],
        out_specs=pl.BlockSpec((seq, D_MODEL), lambda r, g: (r, 0)),
        out_shape=jax.ShapeDtypeStruct((batch * seq, D_MODEL), F32),
        scratch_shapes=[pltpu.VMEM((seq, D_MODEL), F32)],
        compiler_params=_cparams("parallel", "arbitrary"),
    )(x2d, rankc, ye_hi, ye_lo, mod, final_g.reshape(1, D_MODEL))


def _reorder_w_in(w_in):
    a_main = w_in[:, :, 0:2048]
    a_gate = w_in[:, :, 2048:2064]
    b_main = w_in[:, :, 2064:3600]
    b_gate = w_in[:, :, 3600:3616]
    c_main = w_in[:, :, 3616:4384]
    merge = w_in[:, :, 4384:7456]

    def per_head(g, heads):
        g = g.reshape(DEPTH, D_MODEL, 4, heads).transpose(0, 1, 3, 2)
        g = jnp.pad(g, ((0, 0), (0, 0), (0, 0), (0, GATE_STRIDE - 4)))
        return g.reshape(DEPTH, D_MODEL, heads * GATE_STRIDE)

    gates = jnp.concatenate([per_head(a_gate, HEADS_A), per_head(b_gate, HEADS_B)], axis=-1)
    pad = jnp.zeros((DEPTH, D_MODEL, N_PROJ - T_GATES * LANE - gates.shape[-1]), w_in.dtype)
    return _bf(jnp.concatenate([merge, a_main, b_main, c_main, gates, pad], axis=-1))


def _lane_rows(rows):
    p = jnp.stack(rows, axis=-1)
    p = p.reshape(-1, len(rows))
    return jnp.broadcast_to(p[:, :, None], p.shape + (LANE,)).astype(F32)


def kernel(x_prompt, x_sample, cache_attn_k, cache_attn_v, state_delta, state_mlstm_c, state_mlstm_n, state_mlstm_m, c, c_ctx, ada_w, ada_b, norm1_g, norm2_g, w_in, conv_qkv_a, delta_a_log, delta_dt_bias, delta_norm_g, mlstm_i_bias, mlstm_f_bias, mlstm_norm_g, attn_sink, w_branch_a, w_branch_b, w_branch_c, w_out, w_router, w_expert_gate, w_expert_up, w_expert_down, final_norm_g):
    batch_p, seq_p, _ = x_prompt.shape
    batch_s, seq_s, _ = x_sample.shape
    past = cache_attn_k.shape[2]

    cond = jnp.concatenate([c_ctx[None, :], c, jnp.zeros((COND_ROWS - 1 - batch_s, D_MODEL), F32)], axis=0)
    mod = _modulation(cond, ada_w, ada_b).reshape(DEPTH, COND_ROWS, ADA_CHUNKS, D_MODEL)
    mod = jnp.pad(mod, ((0, 0), (0, 0), (0, MOD_ROWS - ADA_CHUNKS), (0, 0)))

    w_in_r = _reorder_w_in(w_in)
    wa, wb, wc, wo = _bf(w_branch_a), _bf(w_branch_b), _bf(w_branch_c), _bf(w_out)
    wr = _bf(jnp.pad(w_router, ((0, 0), (0, 0), (0, LANE - N_EXPERTS))))
    par_a = _lane_rows([delta_a_log[:, 0], delta_a_log[:, 1], delta_dt_bias[:, 0], delta_dt_bias[:, 1]])
    par_b = _lane_rows([mlstm_i_bias[:, 0], mlstm_i_bias[:, 1], mlstm_f_bias[:, 0], mlstm_f_bias[:, 1]])
    sink = jnp.broadcast_to(attn_sink[:, :, None], (DEPTH, HEADS_C, LANE)).astype(F32)
    cache_k = cache_attn_k.reshape(batch_s, DEPTH, past, KV_HEADS_C * HEAD_DIM_C)
    cache_v = cache_attn_v.reshape(batch_s, DEPTH, past, KV_HEADS_C * HEAD_DIM_C)
    state_n = state_mlstm_n.reshape(batch_s, DEPTH, 2, HEADS_B, 1, DK_B)
    state_m = jnp.broadcast_to(state_mlstm_m[..., None, None], (batch_s, DEPTH, 2, HEADS_B, 1, LANE)).astype(F32)
    cos, sin = _rope_tables(seq_s)

    def layer(x2d, l, batch, seq, latent):
        mod_l = mod[l, 1:1 + batch] if latent else mod[l, 0:1]
        cap = EC_CAPACITY * seq // N_EXPERTS
        proj = _in_proj(x2d, mod_l, norm1_g, w_in_r, l, seq, latent)
        ya, d_new = _delta_mixer(proj, conv_qkv_a, par_a, delta_norm_g, l, batch, seq,
                                 state_delta if latent else None)
        yb, b_new = _mlstm_mixer(proj, par_b, mlstm_norm_g, l, batch, seq,
                                 (state_mlstm_c, state_n, state_m) if latent else None)
        if latent:
            yc = _latent_attention(proj, cache_k, cache_v, cos, sin, sink, l, batch, seq)
        else:
            yc = _ctx_attention(proj, sink, l, batch, seq)
        x1, h2, aff, afft = _mix(x2d, ya, yb, yc, proj, mod_l, norm2_g, wa, wb, wc, wo, wr, l, seq, latent)
        xs, gate, rankc = _gather(aff, afft, h2, batch, seq, cap)
        ye_hi, ye_lo = _expert_ffn(xs.reshape(N_EXPERTS, batch * cap, D_MODEL),
                                   gate.reshape(N_EXPERTS, batch * cap, 1),
                                   w_expert_gate, w_expert_up, w_expert_down, l)
        slots = (N_EXPERTS, batch, cap, D_MODEL)
        x2 = _scatter(x1, rankc, ye_hi.reshape(slots), ye_lo.reshape(slots), mod_l, final_norm_g,
                      batch, seq, cap, latent, l == DEPTH - 1)
        return x2, proj, d_new, b_new

    xp = x_prompt.reshape(batch_p * seq_p, D_MODEL)
    ks, vs, ds, cs, ns, ms = [], [], [], [], [], []
    for l in range(DEPTH):
        xp, proj, d_new, (c_new, n_new, m_new) = layer(xp, l, batch_p, seq_p, False)
        ks.append(proj[:, T_KC * LANE:(T_KC + 1) * LANE].reshape(batch_p, seq_p, KV_HEADS_C, HEAD_DIM_C))
        vs.append(proj[:, T_VC * LANE:(T_VC + 1) * LANE].reshape(batch_p, seq_p, KV_HEADS_C, HEAD_DIM_C))
        ds.append(d_new)
        cs.append(c_new)
        ns.append(n_new)
        ms.append(m_new)

    xs = x_sample.reshape(batch_s * seq_s, D_MODEL)
    for l in range(DEPTH):
        xs, _, _, _ = layer(xs, l, batch_s, seq_s, True)

    stack = lambda parts: jnp.stack(parts, axis=1)
    return (xp.reshape(batch_p, seq_p, D_MODEL), xs.reshape(batch_s, seq_s, D_MODEL),
            stack(ks), stack(vs), stack(ds), stack(cs), stack(ns), stack(ms))
```

```python
import functools

import jax
import jax.numpy as jnp
from jax import lax
from jax.experimental import pallas as pl
from jax.experimental.pallas import tpu as pltpu

F32 = jnp.float32
BF16 = jnp.bfloat16

D_MODEL = 1024
DEPTH = 2
GRID_W = 64
EPS = 1e-6
HEADS_A = 4
DK_A = 128
DV_A = 128
CONV_K = 5
CHUNK = 64
HEADS_B = 4
DK_B = 64
DV_B = 128
HEADS_C = 8
KV_HEADS_C = 2
HEAD_DIM_C = 64
GROUP_C = HEADS_C // KV_HEADS_C
WINDOW = 128
Q_BLOCK = 128
ROPE_BASE = 10000.0
N_EXPERTS = 16
D_EXPERT = 512
EC_CAPACITY = 2
ADA_CHUNKS = 6

LANE = 128
MOD_ROWS = 8
COND_ROWS = 16

T_GA, T_GB, T_GC = 0, 8, 16
T_QA, T_KA, T_VA, T_ZA = 24, 28, 32, 36
T_QB, T_KB, T_VB, T_OB = 40, 42, 44, 48
T_QC, T_KC, T_VC = 52, 56, 57
T_GATES = 58
N_TILES = 60
N_PROJ = N_TILES * LANE
GATE_STRIDE = 8
GATE_B_OFF = HEADS_A * GATE_STRIDE

VMEM_LIMIT = 48 * 1024 * 1024


def _cparams(*sem):
    return pltpu.CompilerParams(dimension_semantics=sem, vmem_limit_bytes=VMEM_LIMIT)


def _bf(x):
    return x.astype(BF16)


def _mm(a, b):
    return jnp.dot(_bf(a), _bf(b), preferred_element_type=F32)


def _mm_nt(a, b):
    return lax.dot_general(_bf(a), _bf(b), (((1,), (1,)), ((), ())), preferred_element_type=F32)


def _mm_tn(a, b):
    return lax.dot_general(_bf(a), _bf(b), (((0,), (0,)), ((), ())), preferred_element_type=F32)


def _split2(x):
    hi = _bf(x)
    return hi, _bf(x - hi.astype(F32))


def _split3(x):
    hi = _bf(x)
    r = x - hi.astype(F32)
    mid = _bf(r)
    return hi, mid, _bf(r - mid.astype(F32))


def _mm_sel(sel, x):
    hi, mid, lo = _split3(x)
    d = functools.partial(jnp.dot, preferred_element_type=F32)
    return (d(sel, hi) + d(sel, mid)) + d(sel, lo)


def _mm_hi(a, b):
    ah, al = _split2(a)
    bh, bl = _split2(b)
    d = functools.partial(jnp.dot, preferred_element_type=F32)
    return d(ah, bh) + (d(ah, bl) + d(al, bh))


def _sigmoid(x):
    return 1.0 / (1.0 + jnp.exp(-x))


def _silu(x):
    return x * _sigmoid(x)


def _softplus(x):
    return jnp.maximum(x, 0.0) + jnp.log(1.0 + jnp.exp(-jnp.abs(x)))


def _rms(x, g):
    return x * lax.rsqrt(jnp.mean(x * x, axis=-1, keepdims=True) + EPS) * g


def _chunk_masks(backward):
    ri = lax.broadcasted_iota(jnp.int32, (CHUNK, CHUNK), 0)
    ci = lax.broadcasted_iota(jnp.int32, (CHUNK, CHUNK), 1)
    if backward:
        return ri <= ci, ri < ci, ri == ci
    return ri >= ci, ri > ci, ri == ci


def _mod_kernel(c_ref, w_ref, b_ref, o_ref):
    o_ref[0] = _mm(_silu(c_ref[...]), w_ref[0]) + b_ref[0]


def _modulation(cond, ada_w, ada_b):
    n_out = ADA_CHUNKS * D_MODEL
    tn = 512
    return pl.pallas_call(
        _mod_kernel,
        grid=(DEPTH, n_out // tn),
        in_specs=[
            pl.BlockSpec((COND_ROWS, D_MODEL), lambda l, j: (0, 0)),
            pl.BlockSpec((1, D_MODEL, tn), lambda l, j: (l, 0, j)),
            pl.BlockSpec((1, 1, tn), lambda l, j: (l, 0, j)),
        ],
        out_specs=pl.BlockSpec((1, COND_ROWS, tn), lambda l, j: (l, 0, j)),
        out_shape=jax.ShapeDtypeStruct((DEPTH, COND_ROWS, n_out), F32),
        compiler_params=_cparams("parallel", "parallel"),
    )(cond, ada_w, ada_b.reshape(DEPTH, 1, n_out))


def _in_proj_kernel(x_ref, mod_ref, g_ref, w_ref, o_ref, h_scr):
    @pl.when(pl.program_id(1) == 0)
    def _():
        m = mod_ref[0]
        h_scr[...] = _bf(_rms(x_ref[...], g_ref[0]) * (1.0 + m[1:2]) + m[0:1])

    o_ref[...] = jnp.dot(h_scr[...], w_ref[0], preferred_element_type=F32)


def _in_proj(x2d, mod, norm_g, w_in, layer, seq, per_request):
    m_rows = x2d.shape[0]
    tm = min(1024, m_rows)
    tn = 768
    if per_request:
        assert seq % tm == 0
    return pl.pallas_call(
        _in_proj_kernel,
        grid=(m_rows // tm, N_PROJ // tn),
        in_specs=[
            pl.BlockSpec((tm, D_MODEL), lambda i, j: (i, 0)),
            pl.BlockSpec((1, MOD_ROWS, D_MODEL), lambda i, j: ((i * tm) // seq if per_request else 0, 0, 0)),
            pl.BlockSpec((1, 1, D_MODEL), lambda i, j: (layer, 0, 0)),
            pl.BlockSpec((1, D_MODEL, tn), lambda i, j: (layer, 0, j)),
        ],
        out_specs=pl.BlockSpec((tm, tn), lambda i, j: (i, j)),
        out_shape=jax.ShapeDtypeStruct((m_rows, N_PROJ), F32),
        scratch_shapes=[pltpu.VMEM((tm, D_MODEL), BF16)],
        compiler_params=_cparams("parallel", "arbitrary"),
    )(x2d, mod, norm_g.reshape(DEPTH, 1, D_MODEL), w_in)


PREP_PROBLEMS = 8
DELTA_HEAD_TOKENS = 2048
INV_GROUP = 16


def _gate_dense(gt_parts, lane_index):
    row = lax.broadcasted_iota(jnp.int32, (LANE, LANE), 0)
    sel = jnp.where(row == lane_index, 1.0, 0.0).astype(BF16)
    hi, mid, lo = gt_parts
    d = functools.partial(jnp.dot, preferred_element_type=F32)
    return (d(hi, sel) + d(mid, sel)) + d(lo, sel)


def _delta_kernel(q_ref, k_ref, v_ref, z_ref, gt_ref, cq_ref, ck_ref, cv_ref, par_ref, ng_ref, *rest,
                  seq, heads, has_init, emit_state):
    rest = list(rest)
    s0_ref = rest.pop(0) if has_init else None
    y_ref = rest.pop(0)
    st_ref = rest.pop(0) if emit_state else None
    qs, ks, vs, gates, s_scr, o_scr, u_scr, wq_scr, akd_scr, dk_scr, pw_scr, inv_scr, rhs_scr = rest
    head_group = pl.program_id(1)
    n_chunks = seq // CHUNK
    n_chain = 2 * heads
    rows_t = lax.broadcasted_iota(jnp.int32, (seq, 1), 0)
    head_lanes = lambda hd: slice(hd * DK_A, (hd + 1) * DK_A)

    def conv_silu(x_ref, w_ref):
        x = x_ref[...]
        w = w_ref[0]
        acc = x * w[CONV_K // 2:CONV_K // 2 + 1, :]
        for j in range(CONV_K):
            s = j - CONV_K // 2
            if s == 0:
                continue
            shifted = pltpu.roll(x, (-s) % seq, 0)
            ok = (rows_t + s >= 0) & (rows_t + s < seq)
            acc = acc + jnp.where(ok, shifted, 0.0) * w[j:j + 1, :]
        return _silu(acc)

    def l2n(x):
        parts = [x[:, head_lanes(hd)] for hd in range(heads)]
        return jnp.concatenate(
            [p * lax.rsqrt(jnp.sum(p * p, axis=-1, keepdims=True) + EPS) for p in parts], axis=1)

    qs[...] = l2n(conv_silu(q_ref, cq_ref)) * (DK_A ** -0.5)
    ks[...] = l2n(conv_silu(k_ref, ck_ref))
    vs[...] = conv_silu(v_ref, cv_ref)

    gt_parts = _split3(gt_ref[...])
    for hd in range(heads):
        par = par_ref[hd]
        base = (head_group * heads + hd) * GATE_STRIDE
        for d in range(2):
            gates[hd * 4 + d] = _sigmoid(_gate_dense(gt_parts, base + d))
            gates[hd * 4 + 2 + d] = (-jnp.exp(par[d:d + 1, :])
                                     * _softplus(_gate_dense(gt_parts, base + 2 + d) + par[2 + d:3 + d, :]))
            s_scr[hd * 2 + d] = s0_ref[0, 0, d, hd] if has_init else jnp.zeros((DK_A, DV_A), F32)

    def chunk_rows(c):
        return pl.ds(pl.multiple_of(c * CHUNK, CHUNK), CHUNK)

    n_prob = n_chain * n_chunks
    group = min(INV_GROUP, n_prob)
    prep_chunks = max(1, PREP_PROBLEMS // n_chain)
    ri = lax.broadcasted_iota(jnp.int32, (CHUNK, CHUNK), 0)
    ci = lax.broadcasted_iota(jnp.int32, (CHUNK, CHUNK), 1)
    dot = functools.partial(jnp.dot, preferred_element_type=F32)

    def setup_body(i, carry):
        loaded = []
        for cc in range(prep_chunks):
            c = i * prep_chunks + cc
            rows = chunk_rows(c)
            for hd in range(heads):
                loaded.append((c, hd, qs[rows, head_lanes(hd)], ks[rows, head_lanes(hd)], vs[rows, head_lanes(hd)],
                               [(gates[hd * 4 + d, rows, :], gates[hd * 4 + 2 + d, rows, :]) for d in range(2)]))
        products = []
        for c, hd, q, k, v, gate_cols in loaded:
            per_dir = []
            for d in range(2):
                g = gate_cols[d][1]
                incl, strict, _ = _chunk_masks(d == 1)
                rhs = jnp.concatenate(
                    [jnp.where(strict, g[:, :CHUNK], 0.0), jnp.zeros((CHUNK, LANE - CHUNK), F32), g], axis=1)
                per_dir.append(_mm_sel(jnp.where(incl, 1.0, 0.0).astype(BF16), rhs))
            products.append((_mm_nt(k, k), _mm_nt(q, k), per_dir))
        results = []
        for (c, hd, q, k, v, gate_cols), (kk, qk, per_dir) in zip(loaded, products):
            for d in range(2):
                bt, g = gate_cols[d]
                incl, strict, _ = _chunk_masks(d == 1)
                cs = per_dir[d]
                gc = cs[:, LANE:]
                g_last = gc[0:1, :] if d == 1 else gc[CHUNK - 1:CHUNK, :]
                egc = jnp.exp(gc)
                dec = jnp.where(incl, jnp.exp(cs[:, :CHUNK]), 0.0)
                low = jnp.where(strict, bt[:, :CHUNK] * kk * dec, 0.0)
                rhs2 = jnp.concatenate([v * bt, k * (bt * egc)], axis=1)
                akd = _bf(jnp.concatenate([qk * dec, (k * jnp.exp(g_last - gc)).T], axis=0))
                results.append(((hd * 2 + d) * n_chunks + c, low, rhs2, akd, _bf(q * egc), jnp.exp(g_last)))
        for p, low, rhs2, akd, qd, decay in results:
            pw_scr[p] = low
            inv_scr[p] = jnp.where(ri == ci, 1.0, 0.0) - jnp.where((ri // 2) == (ci // 2), low, 0.0)
            rhs_scr[p] = rhs2
            akd_scr[p] = akd
            wq_scr[p, CHUNK:, :] = qd
            dk_scr[p] = decay
        return carry

    lax.fori_loop(0, n_chunks // prep_chunks, setup_body, 0)

    def doubling_pass(size):
        joins = ((ri // (2 * size)) == (ci // (2 * size))) & ((ri // size) != (ci // size))

        def body(i, carry):
            loaded = []
            for j in range(group):
                p = i * group + j
                loaded.append((p, pw_scr[p], inv_scr[p]))
            partial = [(p, inv, _mm_hi(inv, jnp.where(joins, low, 0.0))) for p, low, inv in loaded]
            results = [(p, inv - _mm_hi(t, inv)) for p, inv, t in partial]
            for p, new_inv in results:
                inv_scr[p] = new_inv
            return carry

        lax.fori_loop(0, n_prob // group, body, 0)

    size = 2
    while size < CHUNK:
        doubling_pass(size)
        size *= 2

    def solve_body(i, carry):
        loaded = []
        for j in range(group):
            p = i * group + j
            loaded.append((p, inv_scr[p], rhs_scr[p]))
        results = [(p, _mm_hi(inv, rhs2)) for p, inv, rhs2 in loaded]
        for p, sol in results:
            u_scr[p] = sol[:, :DV_A]
            wq_scr[p, :CHUNK, :] = _bf(sol[:, DV_A:])
        return carry

    lax.fori_loop(0, n_prob // group, solve_body, 0)

    def scan_body(i, carry):
        loaded = []
        for chain in range(n_chain):
            c = n_chunks - 1 - i if chain % 2 == 1 else i
            p = chain * n_chunks + c
            loaded.append((c, s_scr[chain], u_scr[p], wq_scr[p], akd_scr[p], dk_scr[p]))
        first = [dot(wq, _bf(state)) for c, state, u, wq, akd, decay in loaded]
        second = [dot(akd, _bf(u - ws[:CHUNK])) for (c, state, u, wq, akd, decay), ws in zip(loaded, first)]
        results = [(c, ws[CHUNK:] + av[:CHUNK], decay * state + av[CHUNK:])
                   for (c, state, u, wq, akd, decay), ws, av in zip(loaded, first, second)]
        for chain, (c, o, state) in enumerate(results):
            o_scr[chain, chunk_rows(c), :] = o
            s_scr[chain] = state
        return carry

    lax.fori_loop(0, n_chunks, scan_body, 0)

    z = z_ref[...]
    for hd in range(heads):
        y_ref[:, head_lanes(hd)] = (_rms(o_scr[hd * 2] + o_scr[hd * 2 + 1], ng_ref[0])
                                    * _silu(z[:, head_lanes(hd)]))
        if emit_state:
            for d in range(2):
                st_ref[0, d, hd] = s_scr[hd * 2 + d]


def _delta_mixer(proj, conv_w, par, norm_g, layer, batch, seq, state0):
    has_init = state0 is not None
    emit_state = not has_init
    heads = next(h for h in (4, 2, 1) if HEADS_A % h == 0 and h * seq <= max(DELTA_HEAD_TOKENS, seq))
    n_groups = HEADS_A // heads
    width = heads * DK_A
    col = lambda tile: (lambda r, g: (r, tile // heads + g))
    cw = lambda part: (lambda r, g: (layer, 0, part * n_groups + g))
    in_specs = [
        pl.BlockSpec((seq, width), col(T_QA)),
        pl.BlockSpec((seq, width), col(T_KA)),
        pl.BlockSpec((seq, width), col(T_VA)),
        pl.BlockSpec((seq, width), col(T_ZA)),
        pl.BlockSpec((seq, LANE), lambda r, g: (r, T_GATES)),
        pl.BlockSpec((1, CONV_K, width), cw(0)),
        pl.BlockSpec((1, CONV_K, width), cw(1)),
        pl.BlockSpec((1, CONV_K, width), cw(2)),
        pl.BlockSpec((heads, 4, LANE), lambda r, g: (layer * n_groups + g, 0, 0)),
        pl.BlockSpec((1, 1, DV_A), lambda r, g: (layer, 0, 0)),
    ]
    args = [proj, proj, proj, proj, proj, conv_w, conv_w, conv_w, par, norm_g.reshape(DEPTH, 1, DV_A)]
    if has_init:
        in_specs.append(pl.BlockSpec((1, 1, 2, heads, DK_A, DV_A), lambda r, g: (r, layer, 0, g, 0, 0)))
        args.append(state0)
    out_specs = [pl.BlockSpec((seq, width), lambda r, g: (r, g))]
    out_shape = [jax.ShapeDtypeStruct((batch * seq, HEADS_A * DV_A), F32)]
    if emit_state:
        out_specs.append(pl.BlockSpec((1, 2, heads, DK_A, DV_A), lambda r, g: (r, 0, g, 0, 0)))
        out_shape.append(jax.ShapeDtypeStruct((batch, 2, HEADS_A, DK_A, DV_A), F32))
    n_chain = 2 * heads
    n_prob = n_chain * (seq // CHUNK)
    outs = pl.pallas_call(
        functools.partial(_delta_kernel, seq=seq, heads=heads, has_init=has_init, emit_state=emit_state),
        grid=(batch, n_groups),
        in_specs=in_specs,
        out_specs=out_specs,
        out_shape=out_shape,
        scratch_shapes=[
            pltpu.VMEM((seq, width), F32), pltpu.VMEM((seq, width), F32), pltpu.VMEM((seq, width), F32),
            pltpu.VMEM((2 * n_chain, seq, LANE), F32), pltpu.VMEM((n_chain, DK_A, DV_A), F32),
            pltpu.VMEM((n_chain, seq, DV_A), F32),
            pltpu.VMEM((n_prob, CHUNK, DV_A), F32), pltpu.VMEM((n_prob, 2 * CHUNK, DK_A), BF16),
            pltpu.VMEM((n_prob, CHUNK + DK_A, CHUNK), BF16), pltpu.VMEM((n_prob, 1, DV_A), F32),
            pltpu.VMEM((n_prob, CHUNK, CHUNK), F32), pltpu.VMEM((n_prob, CHUNK, CHUNK), F32),
            pltpu.VMEM((n_prob, CHUNK, DV_A + DK_A), F32),
        ],
        compiler_params=_cparams("parallel", "parallel"),
    )(*args)
    return (outs[0], outs[1]) if emit_state else (outs[0], None)


HEADS_PER_STEP_B = 4


def _mlstm_kernel(q_ref, k_ref, v_ref, og_ref, gt_ref, par_ref, ng_ref, *rest, seq, has_init, emit_state):
    rest = list(rest)
    if has_init:
        c0_ref, n0_ref, m0_ref = rest[:3]
        rest = rest[3:]
    y_ref = rest.pop(0)
    if emit_state:
        co_ref, no_ref, mo_ref = rest[:3]
        rest = rest[3:]
    gates, c_scr, n_scr, m_scr, h_scr = rest
    pair = pl.program_id(1)
    n_chunks = seq // CHUNK
    gt_parts = _split3(gt_ref[...])

    for j in range(HEADS_PER_STEP_B):
        par = par_ref[j]
        base = GATE_B_OFF + (pair * HEADS_PER_STEP_B + j) * GATE_STRIDE
        for d in range(2):
            gates[j * 4 + d] = _gate_dense(gt_parts, base + d) + par[d:d + 1, :]
            gates[j * 4 + 2 + d] = -_softplus(-(_gate_dense(gt_parts, base + 2 + d) + par[2 + d:3 + d, :]))
            idx = j * 2 + d
            if has_init:
                c_scr[idx] = c0_ref[0, 0, d, j]
                n_scr[idx] = n0_ref[0, 0, d, j]
                m_scr[idx] = m0_ref[0, 0, d, j]
            else:
                c_scr[idx] = jnp.zeros((DK_B, DV_B), F32)
                n_scr[idx] = jnp.zeros((1, DK_B), F32)
                m_scr[idx] = jnp.zeros((1, LANE), F32)

    n_chain = 2 * HEADS_PER_STEP_B

    def scan_body(i, carry):
        loaded = []
        for chain in range(n_chain):
            j, d = chain // 2, chain % 2
            c = n_chunks - 1 - i if d == 1 else i
            rows = pl.ds(pl.multiple_of(c * CHUNK, CHUNK), CHUNK)
            loaded.append((chain, rows, q_ref[rows, j * DK_B:(j + 1) * DK_B],
                           k_ref[rows, j * DK_B:(j + 1) * DK_B] * (DK_B ** -0.5),
                           v_ref[rows, j * DV_B:(j + 1) * DV_B],
                           gates[j * 4 + d, rows, :], gates[j * 4 + 2 + d, rows, :],
                           c_scr[chain], n_scr[chain], m_scr[chain]))
        stage1 = []
        for chain, rows, q, k, v, ig, lf, cmat, nvec, m_prev in loaded:
            incl, strict, diag = _chunk_masks(chain % 2 == 1)
            rhs = jnp.concatenate(
                [jnp.where(strict, lf[:, :CHUNK], 0.0) + jnp.where(diag, ig[:, :CHUNK], 0.0),
                 jnp.zeros((CHUNK, LANE - CHUNK), F32), lf], axis=1)
            stage1.append((_mm_sel(jnp.where(incl, 1.0, 0.0).astype(BF16), rhs), _mm_nt(q, k), _mm(q, cmat)))
        stage2 = []
        for (chain, rows, q, k, v, ig, lf, cmat, nvec, m_prev), (cs, qk, qc) in zip(loaded, stage1):
            incl, _, _ = _chunk_masks(chain % 2 == 1)
            bc = cs[:, LANE:]
            b_last = bc[0:1, :] if chain % 2 == 1 else bc[CHUNK - 1:CHUNK, :]
            d_log = jnp.where(incl, cs[:, :CHUNK], -jnp.inf)
            d_max = jnp.max(d_log, axis=1, keepdims=True)
            tok = b_last - bc + ig
            m_new = jnp.maximum(b_last + m_prev, jnp.max(tok, axis=0, keepdims=True))
            w_prev = jnp.exp(b_last + m_prev - m_new)
            kw = k * jnp.exp(tok - m_new)[:, :DK_B]
            stage2.append((bc, d_log, d_max, m_new, w_prev, kw))
        stage3 = []
        for (chain, rows, q, k, v, ig, lf, cmat, nvec, m_prev), (cs, qk, qc), (bc, d_log, d_max, m_new, w_prev, kw) \
                in zip(loaded, stage1, stage2):
            m_t = jnp.maximum(bc + m_prev, d_max)
            w_inter = jnp.exp(bc + m_prev - m_t)
            pm = jnp.exp(d_log - m_t[:, :CHUNK]) * qk
            den = jnp.sum(w_inter[:, :DK_B] * (q * nvec) + pm, axis=1, keepdims=True)
            stage3.append((m_t, w_inter, den, _mm(pm, v), _mm_tn(kw, v)))
        results = []
        for (chain, rows, q, k, v, ig, lf, cmat, nvec, m_prev), (cs, qk, qc), (bc, d_log, d_max, m_new, w_prev, kw), \
                (m_t, w_inter, den, pv, inc) in zip(loaded, stage1, stage2, stage3):
            results.append((chain, rows, (w_inter * qc + pv) / jnp.maximum(jnp.abs(den), jnp.exp(-m_t)),
                            w_prev * cmat + inc,
                            w_prev[:, :DK_B] * nvec + jnp.sum(kw, axis=0, keepdims=True), m_new))
        for chain, rows, h, cmat, nvec, m_new in results:
            h_scr[chain, rows, :] = h
            c_scr[chain] = cmat
            n_scr[chain] = nvec
            m_scr[chain] = m_new
        return carry

    lax.fori_loop(0, n_chunks, scan_body, 0)

    og = og_ref[...]
    for j in range(HEADS_PER_STEP_B):
        h = h_scr[j * 2] + h_scr[j * 2 + 1]
        y_ref[:, j * DV_B:(j + 1) * DV_B] = _rms(h, ng_ref[0]) * _sigmoid(og[:, j * DV_B:(j + 1) * DV_B])
        if emit_state:
            for d in range(2):
                co_ref[0, d, j] = c_scr[j * 2 + d]
                no_ref[0, d, j] = n_scr[j * 2 + d]
                mo_ref[0, d, j] = m_scr[j * 2 + d]


def _mlstm_mixer(proj, par, norm_g, layer, batch, seq, state0):
    has_init = state0 is not None
    emit_state = not has_init
    hp = HEADS_PER_STEP_B
    n_pairs = HEADS_B // hp
    in_specs = [
        pl.BlockSpec((seq, hp * DK_B), lambda r, p: (r, T_QB * LANE // (hp * DK_B) + p)),
        pl.BlockSpec((seq, hp * DK_B), lambda r, p: (r, T_KB * LANE // (hp * DK_B) + p)),
        pl.BlockSpec((seq, hp * DV_B), lambda r, p: (r, T_VB * LANE // (hp * DV_B) + p)),
        pl.BlockSpec((seq, hp * DV_B), lambda r, p: (r, T_OB * LANE // (hp * DV_B) + p)),
        pl.BlockSpec((seq, LANE), lambda r, p: (r, T_GATES)),
        pl.BlockSpec((hp, 4, LANE), lambda r, p: (layer * n_pairs + p, 0, 0)),
        pl.BlockSpec((1, 1, DV_B), lambda r, p: (layer, 0, 0)),
    ]
    args = [proj, proj, proj, proj, proj, par, norm_g.reshape(DEPTH, 1, DV_B)]
    if has_init:
        c0, n0, m0 = state0
        in_specs += [
            pl.BlockSpec((1, 1, 2, hp, DK_B, DV_B), lambda r, p: (r, layer, 0, p, 0, 0)),
            pl.BlockSpec((1, 1, 2, hp, 1, DK_B), lambda r, p: (r, layer, 0, p, 0, 0)),
            pl.BlockSpec((1, 1, 2, hp, 1, LANE), lambda r, p: (r, layer, 0, p, 0, 0)),
        ]
        args += [c0, n0, m0]
    out_specs = [pl.BlockSpec((seq, hp * DV_B), lambda r, p: (r, p))]
    out_shape = [jax.ShapeDtypeStruct((batch * seq, HEADS_B * DV_B), F32)]
    if emit_state:
        out_specs += [
            pl.BlockSpec((1, 2, hp, DK_B, DV_B), lambda r, p: (r, 0, p, 0, 0)),
            pl.BlockSpec((1, 2, hp, 1, DK_B), lambda r, p: (r, 0, p, 0, 0)),
            pl.BlockSpec((1, 2, hp, 1, LANE), lambda r, p: (r, 0, p, 0, 0)),
        ]
        out_shape += [
            jax.ShapeDtypeStruct((batch, 2, HEADS_B, DK_B, DV_B), F32),
            jax.ShapeDtypeStruct((batch, 2, HEADS_B, 1, DK_B), F32),
            jax.ShapeDtypeStruct((batch, 2, HEADS_B, 1, LANE), F32),
        ]
    outs = pl.pallas_call(
        functools.partial(_mlstm_kernel, seq=seq, has_init=has_init, emit_state=emit_state),
        grid=(batch, n_pairs),
        in_specs=in_specs,
        out_specs=out_specs,
        out_shape=out_shape,
        scratch_shapes=[
            pltpu.VMEM((4 * hp, seq, LANE), F32), pltpu.VMEM((2 * hp, DK_B, DV_B), F32),
            pltpu.VMEM((2 * hp, 1, DK_B), F32), pltpu.VMEM((2 * hp, 1, LANE), F32),
            pltpu.VMEM((2 * hp, seq, DV_B), F32),
        ],
        compiler_params=_cparams("parallel", "parallel"),
    )(*args)
    if emit_state:
        return outs[0], (outs[1], outs[2][:, :, :, 0, :], outs[3][:, :, :, 0, 0])
    return outs[0], None


Q_SLABS_C = HEADS_C * HEAD_DIM_C // LANE
HEAD_ORDER_C = [h for s in range(Q_SLABS_C) for h in (s, s + GROUP_C)]


def _attend(q_slabs, segments, sink):
    lane = lax.broadcasted_iota(jnp.int32, (1, LANE), 1)
    scale = HEAD_DIM_C ** -0.5
    nt = (((1,), (1,)), ((), ()))
    operands = []
    for kv in range(KV_HEADS_C):
        mine = (lane >= kv * HEAD_DIM_C) & (lane < (kv + 1) * HEAD_DIM_C)
        operands.append([(_bf(jnp.where(mine, k, 0.0)), _bf(jnp.where(mine, v, 0.0)), valid)
                         for k, v, valid in segments])
    problems = [(s, kv) for s in range(len(q_slabs)) for kv in range(KV_HEADS_C)]
    q_bf = [_bf(q) for q in q_slabs]
    scores = [[lax.dot_general(q_bf[s], kb, nt, preferred_element_type=F32) * scale for kb, _, _ in operands[kv]]
              for s, kv in problems]
    weights = []
    for (s, kv), per_seg in zip(problems, scores):
        per_seg = [x if valid is None else jnp.where(valid, x, -jnp.inf)
                   for x, (_, _, valid) in zip(per_seg, operands[kv])]
        head = kv * GROUP_C + s
        tiles = [x[:, t * LANE:(t + 1) * LANE] for x in per_seg for t in range(x.shape[1] // LANE)]
        m = jnp.maximum(jnp.max(functools.reduce(jnp.maximum, tiles), axis=1, keepdims=True),
                        sink[head:head + 1, 0:1])
        es = [jnp.exp(x - m) for x in per_seg]
        e_tiles = [e[:, t * LANE:(t + 1) * LANE] for e in es for t in range(e.shape[1] // LANE)]
        den = (jnp.sum(functools.reduce(jnp.add, e_tiles), axis=1, keepdims=True)
               + jnp.exp(sink[head:head + 1, 0:1] - m))
        weights.append((es, den))
    outs = []
    for (s, kv), (es, den) in zip(problems, weights):
        acc = None
        for e, (_, vb, _) in zip(es, operands[kv]):
            part = jnp.dot(_bf(e), vb, preferred_element_type=F32)
            acc = part if acc is None else acc + part
        outs.append(acc / den)
    return [sum(outs[s * KV_HEADS_C + 1:(s + 1) * KV_HEADS_C], outs[s * KV_HEADS_C]) for s in range(len(q_slabs))]


def _ctx_attn_kernel(q_ref, k_ref, v_ref, sink_ref, o_ref, *, seq):
    q_slabs = [q_ref[:, s * LANE:(s + 1) * LANE] for s in range(Q_SLABS_C)]
    outs = _attend(q_slabs, [(k_ref[...], v_ref[...], None)], sink_ref[0])
    for s, o in enumerate(outs):
        o_ref[:, s * LANE:(s + 1) * LANE] = o


def _ctx_attention(proj, sink, layer, batch, seq):
    width = HEADS_C * HEAD_DIM_C
    return pl.pallas_call(
        functools.partial(_ctx_attn_kernel, seq=seq),
        grid=(batch,),
        in_specs=[
            pl.BlockSpec((seq, width), lambda r: (r, T_QC * LANE // width)),
            pl.BlockSpec((seq, LANE), lambda r: (r, T_KC)),
            pl.BlockSpec((seq, LANE), lambda r: (r, T_VC)),
            pl.BlockSpec((1, HEADS_C, LANE), lambda r: (layer, 0, 0)),
        ],
        out_specs=pl.BlockSpec((seq, width), lambda r: (r, 0)),
        out_shape=jax.ShapeDtypeStruct((batch * seq, width), F32),
        compiler_params=_cparams("parallel"),
    )(proj, proj, proj, sink)


def _rope(x, cos, sin):
    quarter = HEAD_DIM_C // 4
    lane = lax.broadcasted_iota(jnp.int32, (1, LANE), 1)
    first = (lane % (2 * quarter)) < quarter
    partner = jnp.where(first, -pltpu.roll(x, LANE - quarter, 1), pltpu.roll(x, quarter, 1))
    return x * cos + partner * sin


def _latent_attn_kernel(q_ref, k_ref, v_ref, ck_ref, cv_ref, cq_ref, sq_ref, cos_ref, sin_ref, sink_ref, o_ref, *, seq):
    blk = pl.program_id(1)
    span = Q_BLOCK + 2 * WINDOW
    start = blk * Q_BLOCK
    k_start = pl.multiple_of(jnp.clip(start - WINDOW, 0, seq - span), Q_BLOCK)
    win = pl.ds(k_start, span)
    cq = cq_ref[...]
    sq = sq_ref[...]
    q_slabs = [_rope(q_ref[:, s * LANE:(s + 1) * LANE], cq, sq) for s in range(Q_SLABS_C)]
    k = _rope(k_ref[win, :], cos_ref[win, :], sin_ref[win, :])
    q_pos = start + lax.broadcasted_iota(jnp.int32, (Q_BLOCK, 1), 0)
    k_pos = k_start + lax.broadcasted_iota(jnp.int32, (1, span), 1)
    valid = jnp.abs(q_pos - k_pos) <= WINDOW
    outs = _attend(q_slabs, [(k, v_ref[win, :], valid), (ck_ref[0, 0], cv_ref[0, 0], None)], sink_ref[0])
    for s, o in enumerate(outs):
        o_ref[:, s * LANE:(s + 1) * LANE] = o


def _latent_attention(proj, cache_k, cache_v, cos, sin, sink, layer, batch, seq):
    width = HEADS_C * HEAD_DIM_C
    n_blk = seq // Q_BLOCK
    past = cache_k.shape[2]
    return pl.pallas_call(
        functools.partial(_latent_attn_kernel, seq=seq),
        grid=(batch, n_blk),
        in_specs=[
            pl.BlockSpec((Q_BLOCK, width), lambda r, i: (r * n_blk + i, T_QC * LANE // width)),
            pl.BlockSpec((seq, LANE), lambda r, i: (r, T_KC)),
            pl.BlockSpec((seq, LANE), lambda r, i: (r, T_VC)),
            pl.BlockSpec((1, 1, past, LANE), lambda r, i: (r, layer, 0, 0)),
            pl.BlockSpec((1, 1, past, LANE), lambda r, i: (r, layer, 0, 0)),
            pl.BlockSpec((Q_BLOCK, LANE), lambda r, i: (i, 0)),
            pl.BlockSpec((Q_BLOCK, LANE), lambda r, i: (i, 0)),
            pl.BlockSpec((seq, LANE), lambda r, i: (0, 0)),
            pl.BlockSpec((seq, LANE), lambda r, i: (0, 0)),
            pl.BlockSpec((1, HEADS_C, LANE), lambda r, i: (layer, 0, 0)),
        ],
        out_specs=pl.BlockSpec((Q_BLOCK, width), lambda r, i: (r * n_blk + i, 0)),
        out_shape=jax.ShapeDtypeStruct((batch * seq, width), F32),
        compiler_params=_cparams("parallel", "parallel"),
    )(proj, proj, proj, cache_k, cache_v, cos, sin, cos, sin, sink)


def _rope_tables(seq):
    quarter = HEAD_DIM_C // 4
    pos = jnp.arange(seq)
    row = (pos // GRID_W).astype(F32)
    col = (pos % GRID_W).astype(F32)
    inv = jnp.power(ROPE_BASE, -jnp.arange(quarter, dtype=F32) / quarter)
    ang_row = row[:, None] * inv[None, :]
    ang_col = col[:, None] * inv[None, :]
    ang = jnp.concatenate([ang_row, ang_row, ang_col, ang_col], axis=1)
    ang = jnp.concatenate([ang] * (LANE // HEAD_DIM_C), axis=1)
    return jnp.cos(ang), jnp.sin(ang)


def _mix_kernel(x_ref, ya_ref, yb_ref, yc_ref, ga_ref, gb_ref, gc_ref, mod_ref, n2_ref,
                wa_ref, wb_ref, wc_ref, wo_ref, wr_ref, xo_ref, h_ref, aff_ref, afft_ref):
    m = mod_ref[0]
    mixed = (_sigmoid(ga_ref[...]) * _mm(ya_ref[...], wa_ref[0])
             + _sigmoid(gb_ref[...]) * _mm(yb_ref[...], wb_ref[0])
             + _sigmoid(gc_ref[...]) * _mm(yc_ref[...], wc_ref[0]))
    x = x_ref[...] + m[2:3] * _mm(mixed, wo_ref[0])
    xo_ref[...] = x
    h = _bf(_rms(x, n2_ref[0]) * (1.0 + m[4:5]) + m[3:4])
    h_ref[...] = h
    logits = jnp.dot(h, wr_ref[0], preferred_element_type=F32)
    lane = lax.broadcasted_iota(jnp.int32, (1, LANE), 1)
    logits = jnp.where(lane < N_EXPERTS, logits, -jnp.inf)
    e = jnp.exp(logits - jnp.max(logits, axis=1, keepdims=True))
    aff = e / jnp.sum(e, axis=1, keepdims=True)
    aff_ref[...] = aff
    afft_ref[...] = aff.T[:N_EXPERTS, :]


def _mix(x2d, ya, yb, yc, proj, mod, norm2_g, wa, wb, wc, wo, wr, layer, seq, per_request):
    m_rows = x2d.shape[0]
    tm = 256
    gate_blk = lambda tile: (lambda i: (i, tile * LANE // D_MODEL))
    wspec = lambda w: pl.BlockSpec((1,) + w.shape[1:], lambda i: (layer, 0, 0))
    branch = pl.BlockSpec((tm, ya.shape[1]), lambda i: (i, 0))
    return pl.pallas_call(
        _mix_kernel,
        grid=(m_rows // tm,),
        in_specs=[
            pl.BlockSpec((tm, D_MODEL), lambda i: (i, 0)),
            branch, branch, branch,
            pl.BlockSpec((tm, D_MODEL), gate_blk(T_GA)),
            pl.BlockSpec((tm, D_MODEL), gate_blk(T_GB)),
            pl.BlockSpec((tm, D_MODEL), gate_blk(T_GC)),
            pl.BlockSpec((1, MOD_ROWS, D_MODEL), lambda i: ((i * tm) // seq if per_request else 0, 0, 0)),
            pl.BlockSpec((1, 1, D_MODEL), lambda i: (layer, 0, 0)),
            wspec(wa), wspec(wb), wspec(wc), wspec(wo), wspec(wr),
        ],
        out_specs=[
            pl.BlockSpec((tm, D_MODEL), lambda i: (i, 0)),
            pl.BlockSpec((tm, D_MODEL), lambda i: (i, 0)),
            pl.BlockSpec((tm, LANE), lambda i: (i, 0)),
            pl.BlockSpec((N_EXPERTS, tm), lambda i: (0, i)),
        ],
        out_shape=[
            jax.ShapeDtypeStruct((m_rows, D_MODEL), F32),
            jax.ShapeDtypeStruct((m_rows, D_MODEL), BF16),
            jax.ShapeDtypeStruct((m_rows, LANE), F32),
            jax.ShapeDtypeStruct((N_EXPERTS, m_rows), F32),
        ],
        compiler_params=_cparams("parallel"),
    )(x2d, ya, yb, yc, proj, proj, proj, mod, norm2_g.reshape(DEPTH, 1, D_MODEL), wa, wb, wc, wo, wr)


SCATTER_K = 512


def _gather_kernel(aff_ref, afft_ref, h_ref, xs_ref, gate_ref, rankc_ref, rank_scr, onehot_scr, col_scr, *,
                   seq, cap):
    n_blk = seq // LANE
    ri = lax.broadcasted_iota(jnp.int32, (LANE, LANE), 0)
    ci = lax.broadcasted_iota(jnp.int32, (LANE, LANE), 1)
    lane = lax.broadcasted_iota(jnp.int32, (1, LANE), 1)
    slot = lax.broadcasted_iota(jnp.int32, (cap, 1), 0).astype(F32)

    def expert_body(e, carry):
        a_row = afft_ref[pl.ds(e, 1), :]
        a_col = jnp.sum(jnp.where(lane == e, aff_ref[...], 0.0), axis=1, keepdims=True)
        col_scr[...] = jnp.broadcast_to(a_col, (seq, LANE))
        rank_parts = []
        for tb in range(n_blk):
            a_t = a_row[:, tb * LANE:(tb + 1) * LANE]
            count = jnp.zeros((LANE, LANE), F32)
            for sb in range(n_blk):
                a_s = col_scr[sb * LANE:(sb + 1) * LANE, :]
                if sb < tb:
                    beats = a_s >= a_t
                elif sb > tb:
                    beats = a_s > a_t
                else:
                    beats = (a_s > a_t) | ((a_s == a_t) & (ri < ci))
                count = count + jnp.where(beats, 1.0, 0.0)
            rank_parts.append(jnp.sum(count, axis=0, keepdims=True))
        rank = jnp.concatenate(rank_parts, axis=1)
        rank_scr[pl.ds(e, 1), :] = rank
        chosen = rank == slot
        onehot_scr[pl.ds(pl.multiple_of(e * cap, cap), cap), :] = jnp.where(chosen, 1.0, 0.0).astype(BF16)
        gate_ref[e, 0] = jnp.sum(jnp.where(chosen, a_row, 0.0), axis=1, keepdims=True)
        return carry

    lax.fori_loop(0, N_EXPERTS, expert_body, 0)
    ranks = jnp.concatenate([rank_scr[...], jnp.zeros((LANE - N_EXPERTS, seq), F32)], axis=0)
    rankc_ref[...] = ranks.T
    per_group = SCATTER_K // cap
    h = h_ref[...]
    for i in range(N_EXPERTS // per_group):
        rows = jnp.dot(onehot_scr[i * SCATTER_K:(i + 1) * SCATTER_K, :], h, preferred_element_type=F32)
        xs_ref[i * per_group:(i + 1) * per_group, 0] = _bf(rows).reshape(per_group, cap, D_MODEL)


def _gather(aff, afft, h2, batch, seq, cap):
    return pl.pallas_call(
        functools.partial(_gather_kernel, seq=seq, cap=cap),
        grid=(batch,),
        in_specs=[
            pl.BlockSpec((seq, LANE), lambda r: (r, 0)),
            pl.BlockSpec((N_EXPERTS, seq), lambda r: (0, r)),
            pl.BlockSpec((seq, D_MODEL), lambda r: (r, 0)),
        ],
        out_specs=[
            pl.BlockSpec((N_EXPERTS, 1, cap, D_MODEL), lambda r: (0, r, 0, 0)),
            pl.BlockSpec((N_EXPERTS, 1, cap, 1), lambda r: (0, r, 0, 0)),
            pl.BlockSpec((seq, LANE), lambda r: (r, 0)),
        ],
        out_shape=[
            jax.ShapeDtypeStruct((N_EXPERTS, batch, cap, D_MODEL), BF16),
            jax.ShapeDtypeStruct((N_EXPERTS, batch, cap, 1), F32),
            jax.ShapeDtypeStruct((batch * seq, LANE), F32),
        ],
        scratch_shapes=[pltpu.VMEM((N_EXPERTS, seq), F32), pltpu.VMEM((N_EXPERTS * cap, seq), BF16),
                        pltpu.VMEM((seq, LANE), F32)],
        compiler_params=_cparams("parallel"),
    )(aff, afft, h2)


def _ffn_kernel(x_ref, gate_ref, wg_ref, wu_ref, wd_ref, hi_ref, lo_ref):
    x = x_ref[0]
    hid = _silu(_mm(x, wg_ref[0, 0])) * _mm(x, wu_ref[0, 0])
    y = _mm(hid, wd_ref[0, 0]) * gate_ref[0]
    hi, lo = _split2(y)
    hi_ref[0] = hi
    lo_ref[0] = lo


def _expert_ffn(xs, gate, w_gate, w_up, w_down, layer):
    n_rows = xs.shape[1]
    rows = pl.BlockSpec((1, n_rows, D_MODEL), lambda e: (e, 0, 0))
    out = jax.ShapeDtypeStruct((N_EXPERTS, n_rows, D_MODEL), BF16)
    return pl.pallas_call(
        _ffn_kernel,
        grid=(N_EXPERTS,),
        in_specs=[
            rows,
            pl.BlockSpec((1, n_rows, 1), lambda e: (e, 0, 0)),
            pl.BlockSpec((1, 1, D_MODEL, D_EXPERT), lambda e: (layer, e, 0, 0)),
            pl.BlockSpec((1, 1, D_MODEL, D_EXPERT), lambda e: (layer, e, 0, 0)),
            pl.BlockSpec((1, 1, D_EXPERT, D_MODEL), lambda e: (layer, e, 0, 0)),
        ],
        out_specs=[rows, rows],
        out_shape=[out, out],
        compiler_params=_cparams("parallel"),
    )(xs, gate, w_gate, w_up, w_down)


def _scatter_kernel(x_ref, rankc_ref, hi_ref, lo_ref, mod_ref, fg_ref, o_ref, acc, *, cap, final):
    g = pl.program_id(1)
    n_groups = pl.num_programs(1)
    per_lane_tile = LANE // cap
    rank_parts = _split3(rankc_ref[...])
    row = lax.broadcasted_iota(jnp.int32, (LANE, LANE), 0)
    lane = lax.broadcasted_iota(jnp.int32, (LANE, LANE), 1)
    lane_slot = (lax.broadcasted_iota(jnp.int32, (1, LANE), 1) % cap).astype(F32)
    d = functools.partial(jnp.dot, preferred_element_type=F32)
    tiles = []
    for b in range(SCATTER_K // LANE):
        first = (g * (SCATTER_K // LANE) + b) * per_lane_tile
        sel = jnp.where(row == first + lane // cap, 1.0, 0.0).astype(BF16)
        hi, mid, lo = rank_parts
        token_rank = (d(hi, sel) + d(mid, sel)) + d(lo, sel)
        tiles.append(jnp.where(token_rank == lane_slot, 1.0, 0.0).astype(BF16))
    onehot = jnp.concatenate(tiles, axis=1)
    spread = (d(onehot, hi_ref[:, 0].reshape(SCATTER_K, D_MODEL))
              + d(onehot, lo_ref[:, 0].reshape(SCATTER_K, D_MODEL)))

    @pl.when(g == 0)
    def _():
        acc[...] = spread

    @pl.when(g > 0)
    def _():
        acc[...] += spread

    @pl.when(g == n_groups - 1)
    def _():
        x = x_ref[...] + mod_ref[0][5:6] * acc[...]
        o_ref[...] = _rms(x, fg_ref[...]) if final else x


def _scatter(x2d, rankc, ye_hi, ye_lo, mod, final_g, batch, seq, cap, per_request, final):
    experts_per_group = SCATTER_K // cap
    slots = pl.BlockSpec((experts_per_group, 1, cap, D_MODEL), lambda r, g: (g, r, 0, 0))
    return pl.pallas_call(
        functools.partial(_scatter_kernel, cap=cap, final=final),
        grid=(batch, N_EXPERTS // experts_per_group),
        in_specs=[
            pl.BlockSpec((seq, D_MODEL), lambda r, g: (r, 0)),
            pl.BlockSpec((seq, LANE), lambda r, g: (r, 0)),
            slots, slots,
            pl.BlockSpec((1, MOD_ROWS, D_MODEL), lambda r, g: (r if per_request else 0, 0, 0)),
            pl.BlockSpec((1, D_MODEL), lambda r, g: (0, 0)),
        ],
        out_specs=pl.BlockSpec((seq, D_MODEL), lambda r, g: (r, 0)),
        out_shape=jax.ShapeDtypeStruct((batch * seq, D_MODEL), F32),
        scratch_shapes=[pltpu.VMEM((seq, D_MODEL), F32)],
        compiler_params=_cparams("parallel", "arbitrary"),
    )(x2d, rankc, ye_hi, ye_lo, mod, final_g.reshape(1, D_MODEL))


def _reorder_w_in(w_in):
    a_main = w_in[:, :, 0:2048]
    a_gate = w_in[:, :, 2048:2064]
    b_main = w_in[:, :, 2064:3600]
    b_gate = w_in[:, :, 3600:3616]
    c_main = w_in[:, :, 3616:4384]
    merge = w_in[:, :, 4384:7456]

    def per_head(g, heads):
        g = g.reshape(DEPTH, D_MODEL, 4, heads).transpose(0, 1, 3, 2)
        g = jnp.pad(g, ((0, 0), (0, 0), (0, 0), (0, GATE_STRIDE - 4)))
        return g.reshape(DEPTH, D_MODEL, heads * GATE_STRIDE)

    width_q = HEADS_C * HEAD_DIM_C
    qc = c_main[:, :, :width_q].reshape(DEPTH, D_MODEL, HEADS_C, HEAD_DIM_C)
    qc = jnp.stack([qc[:, :, h] for h in HEAD_ORDER_C], axis=2).reshape(DEPTH, D_MODEL, width_q)
    c_main = jnp.concatenate([qc, c_main[:, :, width_q:]], axis=-1)
    gates = jnp.concatenate([per_head(a_gate, HEADS_A), per_head(b_gate, HEADS_B)], axis=-1)
    pad = jnp.zeros((DEPTH, D_MODEL, N_PROJ - T_GATES * LANE - gates.shape[-1]), w_in.dtype)
    return _bf(jnp.concatenate([merge, a_main, b_main, c_main, gates, pad], axis=-1))


def _lane_rows(rows):
    p = jnp.stack(rows, axis=-1)
    p = p.reshape(-1, len(rows))
    return jnp.broadcast_to(p[:, :, None], p.shape + (LANE,)).astype(F32)


def kernel(x_prompt, x_sample, cache_attn_k, cache_attn_v, state_delta, state_mlstm_c, state_mlstm_n, state_mlstm_m, c, c_ctx, ada_w, ada_b, norm1_g, norm2_g, w_in, conv_qkv_a, delta_a_log, delta_dt_bias, delta_norm_g, mlstm_i_bias, mlstm_f_bias, mlstm_norm_g, attn_sink, w_branch_a, w_branch_b, w_branch_c, w_out, w_router, w_expert_gate, w_expert_up, w_expert_down, final_norm_g):
    batch_p, seq_p, _ = x_prompt.shape
    batch_s, seq_s, _ = x_sample.shape
    past = cache_attn_k.shape[2]

    cond = jnp.concatenate([c_ctx[None, :], c, jnp.zeros((COND_ROWS - 1 - batch_s, D_MODEL), F32)], axis=0)
    mod = _modulation(cond, ada_w, ada_b).reshape(DEPTH, COND_ROWS, ADA_CHUNKS, D_MODEL)
    mod = jnp.pad(mod, ((0, 0), (0, 0), (0, MOD_ROWS - ADA_CHUNKS), (0, 0)))

    w_in_r = _reorder_w_in(w_in)
    wa, wb, wo = _bf(w_branch_a), _bf(w_branch_b), _bf(w_out)
    wc = w_branch_c.reshape(DEPTH, HEADS_C, HEAD_DIM_C, D_MODEL)
    wc = _bf(jnp.stack([wc[:, h] for h in HEAD_ORDER_C], axis=1).reshape(DEPTH, HEADS_C * HEAD_DIM_C, D_MODEL))
    wr = _bf(jnp.pad(w_router, ((0, 0), (0, 0), (0, LANE - N_EXPERTS))))
    par_a = _lane_rows([delta_a_log[:, 0], delta_a_log[:, 1], delta_dt_bias[:, 0], delta_dt_bias[:, 1]])
    par_b = _lane_rows([mlstm_i_bias[:, 0], mlstm_i_bias[:, 1], mlstm_f_bias[:, 0], mlstm_f_bias[:, 1]])
    sink = jnp.broadcast_to(attn_sink[:, :, None], (DEPTH, HEADS_C, LANE)).astype(F32)
    cache_k = cache_attn_k.reshape(batch_s, DEPTH, past, KV_HEADS_C * HEAD_DIM_C)
    cache_v = cache_attn_v.reshape(batch_s, DEPTH, past, KV_HEADS_C * HEAD_DIM_C)
    state_n = state_mlstm_n.reshape(batch_s, DEPTH, 2, HEADS_B, 1, DK_B)
    state_m = jnp.broadcast_to(state_mlstm_m[..., None, None], (batch_s, DEPTH, 2, HEADS_B, 1, LANE)).astype(F32)
    cos, sin = _rope_tables(seq_s)

    def layer(x2d, l, batch, seq, latent):
        mod_l = mod[l, 1:1 + batch] if latent else mod[l, 0:1]
        cap = EC_CAPACITY * seq // N_EXPERTS
        proj = _in_proj(x2d, mod_l, norm1_g, w_in_r, l, seq, latent)
        ya, d_new = _delta_mixer(proj, conv_qkv_a, par_a, delta_norm_g, l, batch, seq,
                                 state_delta if latent else None)
        yb, b_new = _mlstm_mixer(proj, par_b, mlstm_norm_g, l, batch, seq,
                                 (state_mlstm_c, state_n, state_m) if latent else None)
        if latent:
            yc = _latent_attention(proj, cache_k, cache_v, cos, sin, sink, l, batch, seq)
        else:
            yc = _ctx_attention(proj, sink, l, batch, seq)
        x1, h2, aff, afft = _mix(x2d, ya, yb, yc, proj, mod_l, norm2_g, wa, wb, wc, wo, wr, l, seq, latent)
        xs, gate, rankc = _gather(aff, afft, h2, batch, seq, cap)
        ye_hi, ye_lo = _expert_ffn(xs.reshape(N_EXPERTS, batch * cap, D_MODEL),
                                   gate.reshape(N_EXPERTS, batch * cap, 1),
                                   w_expert_gate, w_expert_up, w_expert_down, l)
        slots = (N_EXPERTS, batch, cap, D_MODEL)
        x2 = _scatter(x1, rankc, ye_hi.reshape(slots), ye_lo.reshape(slots), mod_l, final_norm_g,
                      batch, seq, cap, latent, l == DEPTH - 1)
        return x2, proj, d_new, b_new

    xp = x_prompt.reshape(batch_p * seq_p, D_MODEL)
    ks, vs, ds, cs, ns, ms = [], [], [], [], [], []
    for l in range(DEPTH):
        xp, proj, d_new, (c_new, n_new, m_new) = layer(xp, l, batch_p, seq_p, False)
        ks.append(proj[:, T_KC * LANE:(T_KC + 1) * LANE].reshape(batch_p, seq_p, KV_HEADS_C, HEAD_DIM_C))
        vs.append(proj[:, T_VC * LANE:(T_VC + 1) * LANE].reshape(batch_p, seq_p, KV_HEADS_C, HEAD_DIM_C))
        ds.append(d_new)
        cs.append(c_new)
        ns.append(n_new)
        ms.append(m_new)

    xs = x_sample.reshape(batch_s * seq_s, D_MODEL)
    for l in range(DEPTH):
        xs, _, _, _ = layer(xs, l, batch_s, seq_s, True)

    stack = lambda parts: jnp.stack(parts, axis=1)
    return (xp.reshape(batch_p, seq_p, D_MODEL), xs.reshape(batch_s, seq_s, D_MODEL),
            stack(ks), stack(vs), stack(ds), stack(cs), stack(ns), stack(ms))
```

```python
import functools

import jax
import jax.numpy as jnp
from jax import lax
from jax.experimental import pallas as pl
from jax.experimental.pallas import tpu as pltpu

F32 = jnp.float32
BF16 = jnp.bfloat16

D_MODEL = 1024
DEPTH = 2
GRID_W = 64
EPS = 1e-6
HEADS_A = 4
DK_A = 128
DV_A = 128
CONV_K = 5
CHUNK = 64
HEADS_B = 4
DK_B = 64
DV_B = 128
HEADS_C = 8
KV_HEADS_C = 2
HEAD_DIM_C = 64
GROUP_C = HEADS_C // KV_HEADS_C
WINDOW = 128
Q_BLOCK = 128
ROPE_BASE = 10000.0
N_EXPERTS = 16
D_EXPERT = 512
EC_CAPACITY = 2
ADA_CHUNKS = 6

LANE = 128
MOD_ROWS = 8
COND_ROWS = 16

T_GA, T_GB, T_GC = 0, 8, 16
T_QA, T_KA, T_VA, T_ZA = 24, 28, 32, 36
T_QB, T_KB, T_VB, T_OB = 40, 42, 44, 48
T_QC, T_KC, T_VC = 52, 56, 57
T_GATES = 58
N_TILES = 60
N_PROJ = N_TILES * LANE
GATE_STRIDE = 8
GATE_B_OFF = HEADS_A * GATE_STRIDE

VMEM_LIMIT = 48 * 1024 * 1024


def _cparams(*sem):
    return pltpu.CompilerParams(dimension_semantics=sem, vmem_limit_bytes=VMEM_LIMIT)


def _bf(x):
    return x.astype(BF16)


def _mm(a, b):
    return jnp.dot(_bf(a), _bf(b), preferred_element_type=F32)


def _mm_nt(a, b):
    return lax.dot_general(_bf(a), _bf(b), (((1,), (1,)), ((), ())), preferred_element_type=F32)


def _mm_tn(a, b):
    return lax.dot_general(_bf(a), _bf(b), (((0,), (0,)), ((), ())), preferred_element_type=F32)


def _split2(x):
    hi = _bf(x)
    return hi, _bf(x - hi.astype(F32))


def _split3(x):
    hi = _bf(x)
    r = x - hi.astype(F32)
    mid = _bf(r)
    return hi, mid, _bf(r - mid.astype(F32))


def _mm_sel(sel, x):
    hi, mid, lo = _split3(x)
    d = functools.partial(jnp.dot, preferred_element_type=F32)
    return (d(sel, hi) + d(sel, mid)) + d(sel, lo)


def _mm_hi(a, b):
    ah, al = _split2(a)
    bh, bl = _split2(b)
    d = functools.partial(jnp.dot, preferred_element_type=F32)
    return d(ah, bh) + (d(ah, bl) + d(al, bh))


def _sigmoid(x):
    return 1.0 / (1.0 + jnp.exp(-x))


def _silu(x):
    return x * _sigmoid(x)


def _softplus(x):
    return jnp.maximum(x, 0.0) + jnp.log(1.0 + jnp.exp(-jnp.abs(x)))


def _rms(x, g):
    return x * lax.rsqrt(jnp.mean(x * x, axis=-1, keepdims=True) + EPS) * g


def _chunk_masks(backward):
    ri = lax.broadcasted_iota(jnp.int32, (CHUNK, CHUNK), 0)
    ci = lax.broadcasted_iota(jnp.int32, (CHUNK, CHUNK), 1)
    if backward:
        return ri <= ci, ri < ci, ri == ci
    return ri >= ci, ri > ci, ri == ci


def _mod_kernel(c_ref, w_ref, b_ref, o_ref):
    o_ref[0] = _mm(_silu(c_ref[...]), w_ref[0]) + b_ref[0]


def _modulation(cond, ada_w, ada_b):
    n_out = ADA_CHUNKS * D_MODEL
    tn = 512
    return pl.pallas_call(
        _mod_kernel,
        grid=(DEPTH, n_out // tn),
        in_specs=[
            pl.BlockSpec((COND_ROWS, D_MODEL), lambda l, j: (0, 0)),
            pl.BlockSpec((1, D_MODEL, tn), lambda l, j: (l, 0, j)),
            pl.BlockSpec((1, 1, tn), lambda l, j: (l, 0, j)),
        ],
        out_specs=pl.BlockSpec((1, COND_ROWS, tn), lambda l, j: (l, 0, j)),
        out_shape=jax.ShapeDtypeStruct((DEPTH, COND_ROWS, n_out), F32),
        compiler_params=_cparams("parallel", "parallel"),
    )(cond, ada_w, ada_b.reshape(DEPTH, 1, n_out))


def _in_proj_kernel(x_ref, mod_ref, g_ref, w_ref, o_ref, h_scr):
    @pl.when(pl.program_id(1) == 0)
    def _():
        m = mod_ref[0]
        h_scr[...] = _bf(_rms(x_ref[...], g_ref[0]) * (1.0 + m[1:2]) + m[0:1])

    o_ref[...] = jnp.dot(h_scr[...], w_ref[0], preferred_element_type=F32)


def _in_proj(x2d, mod, norm_g, w_in, layer, seq, per_request):
    m_rows = x2d.shape[0]
    tm = min(1024, m_rows)
    tn = 768
    if per_request:
        assert seq % tm == 0
    return pl.pallas_call(
        _in_proj_kernel,
        grid=(m_rows // tm, N_PROJ // tn),
        in_specs=[
            pl.BlockSpec((tm, D_MODEL), lambda i, j: (i, 0)),
            pl.BlockSpec((1, MOD_ROWS, D_MODEL), lambda i, j: ((i * tm) // seq if per_request else 0, 0, 0)),
            pl.BlockSpec((1, 1, D_MODEL), lambda i, j: (layer, 0, 0)),
            pl.BlockSpec((1, D_MODEL, tn), lambda i, j: (layer, 0, j)),
        ],
        out_specs=pl.BlockSpec((tm, tn), lambda i, j: (i, j)),
        out_shape=jax.ShapeDtypeStruct((m_rows, N_PROJ), F32),
        scratch_shapes=[pltpu.VMEM((tm, D_MODEL), BF16)],
        compiler_params=_cparams("parallel", "arbitrary"),
    )(x2d, mod, norm_g.reshape(DEPTH, 1, D_MODEL), w_in)


PREP_PROBLEMS = 8
DELTA_HEAD_TOKENS = 2048
INV_GROUP = 32


def _gate_dense(gt_parts, lane_index):
    row = lax.broadcasted_iota(jnp.int32, (LANE, LANE), 0)
    sel = jnp.where(row == lane_index, 1.0, 0.0).astype(BF16)
    hi, mid, lo = gt_parts
    d = functools.partial(jnp.dot, preferred_element_type=F32)
    return (d(hi, sel) + d(mid, sel)) + d(lo, sel)


def _delta_kernel(q_ref, k_ref, v_ref, z_ref, gt_ref, cq_ref, ck_ref, cv_ref, par_ref, ng_ref, *rest,
                  seq, heads, has_init, emit_state):
    rest = list(rest)
    s0_ref = rest.pop(0) if has_init else None
    y_ref = rest.pop(0)
    st_ref = rest.pop(0) if emit_state else None
    qs, ks, vs, gates, s_scr, o_scr, u_scr, wq_scr, akd_scr, dk_scr, pw_scr, inv_scr, rhs_scr = rest
    head_group = pl.program_id(1)
    n_chunks = seq // CHUNK
    n_chain = 2 * heads
    rows_t = lax.broadcasted_iota(jnp.int32, (seq, 1), 0)
    head_lanes = lambda hd: slice(hd * DK_A, (hd + 1) * DK_A)

    def conv_silu(x_ref, w_ref):
        x = x_ref[...]
        w = w_ref[0]
        acc = x * w[CONV_K // 2:CONV_K // 2 + 1, :]
        for j in range(CONV_K):
            s = j - CONV_K // 2
            if s == 0:
                continue
            shifted = pltpu.roll(x, (-s) % seq, 0)
            ok = (rows_t + s >= 0) & (rows_t + s < seq)
            acc = acc + jnp.where(ok, shifted, 0.0) * w[j:j + 1, :]
        return _silu(acc)

    def l2n(x):
        parts = [x[:, head_lanes(hd)] for hd in range(heads)]
        return jnp.concatenate(
            [p * lax.rsqrt(jnp.sum(p * p, axis=-1, keepdims=True) + EPS) for p in parts], axis=1)

    qs[...] = l2n(conv_silu(q_ref, cq_ref)) * (DK_A ** -0.5)
    ks[...] = l2n(conv_silu(k_ref, ck_ref))
    vs[...] = conv_silu(v_ref, cv_ref)

    gt_parts = _split3(gt_ref[...])
    for hd in range(heads):
        par = par_ref[hd]
        base = (head_group * heads + hd) * GATE_STRIDE
        for d in range(2):
            gates[hd * 4 + d] = _sigmoid(_gate_dense(gt_parts, base + d))
            gates[hd * 4 + 2 + d] = (-jnp.exp(par[d:d + 1, :])
                                     * _softplus(_gate_dense(gt_parts, base + 2 + d) + par[2 + d:3 + d, :]))
            s_scr[hd * 2 + d] = s0_ref[0, 0, d, hd] if has_init else jnp.zeros((DK_A, DV_A), F32)

    def chunk_rows(c):
        return pl.ds(pl.multiple_of(c * CHUNK, CHUNK), CHUNK)

    n_prob = n_chain * n_chunks
    group = min(INV_GROUP, n_prob)
    prep_chunks = max(1, PREP_PROBLEMS // n_chain)
    ri = lax.broadcasted_iota(jnp.int32, (CHUNK, 2 * CHUNK), 0)
    ci = lax.broadcasted_iota(jnp.int32, (CHUNK, 2 * CHUNK), 1) % CHUNK
    dot = functools.partial(jnp.dot, preferred_element_type=F32)

    def setup_body(i, carry):
        loaded = []
        for cc in range(prep_chunks):
            c = i * prep_chunks + cc
            rows = chunk_rows(c)
            for hd in range(heads):
                loaded.append((c, hd, qs[rows, head_lanes(hd)], ks[rows, head_lanes(hd)], vs[rows, head_lanes(hd)],
                               [(gates[hd * 4 + d, rows, :], gates[hd * 4 + 2 + d, rows, :]) for d in range(2)]))
        products = []
        for c, hd, q, k, v, gate_cols in loaded:
            per_dir = []
            for d in range(2):
                g = gate_cols[d][1]
                incl, strict, _ = _chunk_masks(d == 1)
                rhs = jnp.concatenate(
                    [jnp.where(strict, g[:, :CHUNK], 0.0), jnp.zeros((CHUNK, LANE - CHUNK), F32), g], axis=1)
                per_dir.append(_mm_sel(jnp.where(incl, 1.0, 0.0).astype(BF16), rhs))
            gram = _mm_nt(jnp.concatenate([k, q], axis=0), k)
            products.append((gram[:CHUNK], gram[CHUNK:], per_dir))
        results = []
        pairs = []
        for (c, hd, q, k, v, gate_cols), (kk, qk, per_dir) in zip(loaded, products):
            lows = []
            for d in range(2):
                bt, g = gate_cols[d]
                incl, strict, _ = _chunk_masks(d == 1)
                cs = per_dir[d]
                gc = cs[:, LANE:]
                g_last = gc[0:1, :] if d == 1 else gc[CHUNK - 1:CHUNK, :]
                egc = jnp.exp(gc)
                dec = jnp.where(incl, jnp.exp(cs[:, :CHUNK]), 0.0)
                lows.append(jnp.where(strict, bt[:, :CHUNK] * kk * dec, 0.0))
                rhs2 = jnp.concatenate([v * bt, k * (bt * egc)], axis=1)
                akd = _bf(jnp.concatenate([qk * dec, (k * jnp.exp(g_last - gc)).T], axis=0))
                results.append(((hd * n_chunks + c) * 2 + d, rhs2, akd, _bf(q * egc), jnp.exp(g_last)))
            pairs.append((hd * n_chunks + c, jnp.concatenate(lows, axis=1)))
        for p, rhs2, akd, qd, decay in results:
            rhs_scr[p] = rhs2
            akd_scr[p] = akd
            wq_scr[p, CHUNK:, :] = qd
            dk_scr[p] = decay
        for pp, low in pairs:
            pw_scr[pp] = low
            inv_scr[pp] = jnp.where(ri == ci, 1.0, 0.0) - jnp.where((ri // 2) == (ci // 2), low, 0.0)
        return carry

    lax.fori_loop(0, n_chunks // prep_chunks, setup_body, 0)

    lane_pair = lax.broadcasted_iota(jnp.int32, (CHUNK, 2 * CHUNK), 1)

    def block_diagonal(y):
        zero = jnp.zeros_like(y)
        return jnp.concatenate(
            [jnp.where(lane_pair < CHUNK, y, zero), jnp.where(lane_pair < CHUNK, zero, y)], axis=0)

    def pair_product(x, y):
        xh, xl = _split2(x)
        both = dot(jnp.concatenate([xh, xl], axis=0), block_diagonal(_bf(y)))
        return both[:CHUNK] + both[CHUNK:]

    n_pairs = n_prob // 2
    pair_group = min(INV_GROUP // 2, n_pairs)

    def doubling_pass(size):
        joins = ((ri // (2 * size)) == (ci // (2 * size))) & ((ri // size) != (ci // size))

        def body(i, carry):
            loaded = []
            for j in range(pair_group):
                pp = i * pair_group + j
                loaded.append((pp, pw_scr[pp], inv_scr[pp]))
            partial = [(pp, inv, pair_product(inv, jnp.where(joins, low, 0.0))) for pp, low, inv in loaded]
            results = [(pp, inv - pair_product(t, inv)) for pp, inv, t in partial]
            for pp, new_inv in results:
                inv_scr[pp] = new_inv
            return carry

        lax.fori_loop(0, n_pairs // pair_group, body, 0)

    size = 2
    while size < CHUNK:
        doubling_pass(size)
        size *= 2

    def solve_body(i, carry):
        loaded = []
        for j in range(pair_group):
            pp = i * pair_group + j
            inv = inv_scr[pp]
            for d in range(2):
                loaded.append((pp * 2 + d, inv[:, d * CHUNK:(d + 1) * CHUNK], rhs_scr[pp * 2 + d]))
        results = []
        for p, inv, rhs2 in loaded:
            ih, il = _split2(inv)
            rh, rl = _split2(rhs2)
            by_hi = dot(jnp.concatenate([ih, il], axis=0), rh)
            results.append((p, by_hi[:CHUNK] + (dot(ih, rl) + by_hi[CHUNK:])))
        for p, sol in results:
            u_scr[p] = sol[:, :DV_A]
            wq_scr[p, :CHUNK, :] = _bf(sol[:, DV_A:])
        return carry

    lax.fori_loop(0, n_pairs // pair_group, solve_body, 0)

    def scan_body(i, carry):
        loaded = []
        for chain in range(n_chain):
            c = n_chunks - 1 - i if chain % 2 == 1 else i
            p = ((chain // 2) * n_chunks + c) * 2 + chain % 2
            loaded.append((c, s_scr[chain], u_scr[p], wq_scr[p], akd_scr[p], dk_scr[p]))
        first = [dot(wq, _bf(state)) for c, state, u, wq, akd, decay in loaded]
        second = [dot(akd, _bf(u - ws[:CHUNK])) for (c, state, u, wq, akd, decay), ws in zip(loaded, first)]
        results = [(c, ws[CHUNK:] + av[:CHUNK], decay * state + av[CHUNK:])
                   for (c, state, u, wq, akd, decay), ws, av in zip(loaded, first, second)]
        for chain, (c, o, state) in enumerate(results):
            o_scr[chain, chunk_rows(c), :] = o
            s_scr[chain] = state
        return carry

    lax.fori_loop(0, n_chunks, scan_body, 0)

    z = z_ref[...]
    for hd in range(heads):
        y_ref[:, head_lanes(hd)] = (_rms(o_scr[hd * 2] + o_scr[hd * 2 + 1], ng_ref[0])
                                    * _silu(z[:, head_lanes(hd)]))
        if emit_state:
            for d in range(2):
                st_ref[0, d, hd] = s_scr[hd * 2 + d]


def _delta_mixer(proj, conv_w, par, norm_g, layer, batch, seq, state0):
    has_init = state0 is not None
    emit_state = not has_init
    heads = next(h for h in (4, 2, 1) if HEADS_A % h == 0 and h * seq <= max(DELTA_HEAD_TOKENS, seq))
    n_groups = HEADS_A // heads
    width = heads * DK_A
    col = lambda tile: (lambda r, g: (r, tile // heads + g))
    cw = lambda part: (lambda r, g: (layer, 0, part * n_groups + g))
    in_specs = [
        pl.BlockSpec((seq, width), col(T_QA)),
        pl.BlockSpec((seq, width), col(T_KA)),
        pl.BlockSpec((seq, width), col(T_VA)),
        pl.BlockSpec((seq, width), col(T_ZA)),
        pl.BlockSpec((seq, LANE), lambda r, g: (r, T_GATES)),
        pl.BlockSpec((1, CONV_K, width), cw(0)),
        pl.BlockSpec((1, CONV_K, width), cw(1)),
        pl.BlockSpec((1, CONV_K, width), cw(2)),
        pl.BlockSpec((heads, 4, LANE), lambda r, g: (layer * n_groups + g, 0, 0)),
        pl.BlockSpec((1, 1, DV_A), lambda r, g: (layer, 0, 0)),
    ]
    args = [proj, proj, proj, proj, proj, conv_w, conv_w, conv_w, par, norm_g.reshape(DEPTH, 1, DV_A)]
    if has_init:
        in_specs.append(pl.BlockSpec((1, 1, 2, heads, DK_A, DV_A), lambda r, g: (r, layer, 0, g, 0, 0)))
        args.append(state0)
    out_specs = [pl.BlockSpec((seq, width), lambda r, g: (r, g))]
    out_shape = [jax.ShapeDtypeStruct((batch * seq, HEADS_A * DV_A), F32)]
    if emit_state:
        out_specs.append(pl.BlockSpec((1, 2, heads, DK_A, DV_A), lambda r, g: (r, 0, g, 0, 0)))
        out_shape.append(jax.ShapeDtypeStruct((batch, 2, HEADS_A, DK_A, DV_A), F32))
    n_chain = 2 * heads
    n_prob = n_chain * (seq // CHUNK)
    outs = pl.pallas_call(
        functools.partial(_delta_kernel, seq=seq, heads=heads, has_init=has_init, emit_state=emit_state),
        grid=(batch, n_groups),
        in_specs=in_specs,
        out_specs=out_specs,
        out_shape=out_shape,
        scratch_shapes=[
            pltpu.VMEM((seq, width), F32), pltpu.VMEM((seq, width), F32), pltpu.VMEM((seq, width), F32),
            pltpu.VMEM((2 * n_chain, seq, LANE), F32), pltpu.VMEM((n_chain, DK_A, DV_A), F32),
            pltpu.VMEM((n_chain, seq, DV_A), F32),
            pltpu.VMEM((n_prob, CHUNK, DV_A), F32), pltpu.VMEM((n_prob, 2 * CHUNK, DK_A), BF16),
            pltpu.VMEM((n_prob, CHUNK + DK_A, CHUNK), BF16), pltpu.VMEM((n_prob, 1, DV_A), F32),
            pltpu.VMEM((n_prob // 2, CHUNK, 2 * CHUNK), F32), pltpu.VMEM((n_prob // 2, CHUNK, 2 * CHUNK), F32),
            pltpu.VMEM((n_prob, CHUNK, DV_A + DK_A), F32),
        ],
        compiler_params=_cparams("parallel", "parallel"),
    )(*args)
    return (outs[0], outs[1]) if emit_state else (outs[0], None)


HEADS_PER_STEP_B = 4


def _mlstm_kernel(q_ref, k_ref, v_ref, og_ref, gt_ref, par_ref, ng_ref, *rest, seq, has_init, emit_state):
    rest = list(rest)
    if has_init:
        c0_ref, n0_ref, m0_ref = rest[:3]
        rest = rest[3:]
    y_ref = rest.pop(0)
    if emit_state:
        co_ref, no_ref, mo_ref = rest[:3]
        rest = rest[3:]
    gates, c_scr, n_scr, m_scr, h_scr = rest
    pair = pl.program_id(1)
    n_chunks = seq // CHUNK
    gt_parts = _split3(gt_ref[...])

    for j in range(HEADS_PER_STEP_B):
        par = par_ref[j]
        base = GATE_B_OFF + (pair * HEADS_PER_STEP_B + j) * GATE_STRIDE
        for d in range(2):
            gates[j * 4 + d] = _gate_dense(gt_parts, base + d) + par[d:d + 1, :]
            gates[j * 4 + 2 + d] = -_softplus(-(_gate_dense(gt_parts, base + 2 + d) + par[2 + d:3 + d, :]))
            idx = j * 2 + d
            if has_init:
                c_scr[idx] = c0_ref[0, 0, d, j]
                n_scr[idx] = n0_ref[0, 0, d, j]
                m_scr[idx] = m0_ref[0, 0, d, j]
            else:
                c_scr[idx] = jnp.zeros((DK_B, DV_B), F32)
                n_scr[idx] = jnp.zeros((1, DK_B), F32)
                m_scr[idx] = jnp.zeros((1, LANE), F32)

    n_chain = 2 * HEADS_PER_STEP_B

    def scan_body(i, carry):
        loaded = []
        for chain in range(n_chain):
            j, d = chain // 2, chain % 2
            c = n_chunks - 1 - i if d == 1 else i
            rows = pl.ds(pl.multiple_of(c * CHUNK, CHUNK), CHUNK)
            loaded.append((chain, rows, q_ref[rows, j * DK_B:(j + 1) * DK_B],
                           k_ref[rows, j * DK_B:(j + 1) * DK_B] * (DK_B ** -0.5),
                           v_ref[rows, j * DV_B:(j + 1) * DV_B],
                           gates[j * 4 + d, rows, :], gates[j * 4 + 2 + d, rows, :],
                           c_scr[chain], n_scr[chain], m_scr[chain]))
        stage1 = []
        for chain, rows, q, k, v, ig, lf, cmat, nvec, m_prev in loaded:
            incl, strict, diag = _chunk_masks(chain % 2 == 1)
            rhs = jnp.concatenate(
                [jnp.where(strict, lf[:, :CHUNK], 0.0) + jnp.where(diag, ig[:, :CHUNK], 0.0),
                 jnp.zeros((CHUNK, LANE - CHUNK), F32), lf], axis=1)
            stage1.append((_mm_sel(jnp.where(incl, 1.0, 0.0).astype(BF16), rhs), _mm_nt(q, k), _mm(q, cmat)))
        stage2 = []
        for (chain, rows, q, k, v, ig, lf, cmat, nvec, m_prev), (cs, qk, qc) in zip(loaded, stage1):
            incl, _, _ = _chunk_masks(chain % 2 == 1)
            bc = cs[:, LANE:]
            b_last = bc[0:1, :] if chain % 2 == 1 else bc[CHUNK - 1:CHUNK, :]
            d_log = jnp.where(incl, cs[:, :CHUNK], -jnp.inf)
            d_max = jnp.max(d_log, axis=1, keepdims=True)
            tok = b_last - bc + ig
            m_new = jnp.maximum(b_last + m_prev, jnp.max(tok, axis=0, keepdims=True))
            w_prev = jnp.exp(b_last + m_prev - m_new)
            kw = k * jnp.exp(tok - m_new)[:, :DK_B]
            stage2.append((bc, d_log, d_max, m_new, w_prev, kw))
        stage3 = []
        for (chain, rows, q, k, v, ig, lf, cmat, nvec, m_prev), (cs, qk, qc), (bc, d_log, d_max, m_new, w_prev, kw) \
                in zip(loaded, stage1, stage2):
            m_t = jnp.maximum(bc + m_prev, d_max)
            w_inter = jnp.exp(bc + m_prev - m_t)
            pm = jnp.exp(d_log - m_t[:, :CHUNK]) * qk
            den = jnp.sum(w_inter[:, :DK_B] * (q * nvec) + pm, axis=1, keepdims=True)
            stage3.append((m_t, w_inter, den, _mm(pm, v), _mm_tn(kw, v)))
        results = []
        for (chain, rows, q, k, v, ig, lf, cmat, nvec, m_prev), (cs, qk, qc), (bc, d_log, d_max, m_new, w_prev, kw), \
                (m_t, w_inter, den, pv, inc) in zip(loaded, stage1, stage2, stage3):
            results.append((chain, rows, (w_inter * qc + pv) / jnp.maximum(jnp.abs(den), jnp.exp(-m_t)),
                            w_prev * cmat + inc,
                            w_prev[:, :DK_B] * nvec + jnp.sum(kw, axis=0, keepdims=True), m_new))
        for chain, rows, h, cmat, nvec, m_new in results:
            h_scr[chain, rows, :] = h
            c_scr[chain] = cmat
            n_scr[chain] = nvec
            m_scr[chain] = m_new
        return carry

    lax.fori_loop(0, n_chunks, scan_body, 0)

    og = og_ref[...]
    for j in range(HEADS_PER_STEP_B):
        h = h_scr[j * 2] + h_scr[j * 2 + 1]
        y_ref[:, j * DV_B:(j + 1) * DV_B] = _rms(h, ng_ref[0]) * _sigmoid(og[:, j * DV_B:(j + 1) * DV_B])
        if emit_state:
            for d in range(2):
                co_ref[0, d, j] = c_scr[j * 2 + d]
                no_ref[0, d, j] = n_scr[j * 2 + d]
                mo_ref[0, d, j] = m_scr[j * 2 + d]


def _mlstm_mixer(proj, par, norm_g, layer, batch, seq, state0):
    has_init = state0 is not None
    emit_state = not has_init
    hp = HEADS_PER_STEP_B
    n_pairs = HEADS_B // hp
    in_specs = [
        pl.BlockSpec((seq, hp * DK_B), lambda r, p: (r, T_QB * LANE // (hp * DK_B) + p)),
        pl.BlockSpec((seq, hp * DK_B), lambda r, p: (r, T_KB * LANE // (hp * DK_B) + p)),
        pl.BlockSpec((seq, hp * DV_B), lambda r, p: (r, T_VB * LANE // (hp * DV_B) + p)),
        pl.BlockSpec((seq, hp * DV_B), lambda r, p: (r, T_OB * LANE // (hp * DV_B) + p)),
        pl.BlockSpec((seq, LANE), lambda r, p: (r, T_GATES)),
        pl.BlockSpec((hp, 4, LANE), lambda r, p: (layer * n_pairs + p, 0, 0)),
        pl.BlockSpec((1, 1, DV_B), lambda r, p: (layer, 0, 0)),
    ]
    args = [proj, proj, proj, proj, proj, par, norm_g.reshape(DEPTH, 1, DV_B)]
    if has_init:
        c0, n0, m0 = state0
        in_specs += [
            pl.BlockSpec((1, 1, 2, hp, DK_B, DV_B), lambda r, p: (r, layer, 0, p, 0, 0)),
            pl.BlockSpec((1, 1, 2, hp, 1, DK_B), lambda r, p: (r, layer, 0, p, 0, 0)),
            pl.BlockSpec((1, 1, 2, hp, 1, LANE), lambda r, p: (r, layer, 0, p, 0, 0)),
        ]
        args += [c0, n0, m0]
    out_specs = [pl.BlockSpec((seq, hp * DV_B), lambda r, p: (r, p))]
    out_shape = [jax.ShapeDtypeStruct((batch * seq, HEADS_B * DV_B), F32)]
    if emit_state:
        out_specs += [
            pl.BlockSpec((1, 2, hp, DK_B, DV_B), lambda r, p: (r, 0, p, 0, 0)),
            pl.BlockSpec((1, 2, hp, 1, DK_B), lambda r, p: (r, 0, p, 0, 0)),
            pl.BlockSpec((1, 2, hp, 1, LANE), lambda r, p: (r, 0, p, 0, 0)),
        ]
        out_shape += [
            jax.ShapeDtypeStruct((batch, 2, HEADS_B, DK_B, DV_B), F32),
            jax.ShapeDtypeStruct((batch, 2, HEADS_B, 1, DK_B), F32),
            jax.ShapeDtypeStruct((batch, 2, HEADS_B, 1, LANE), F32),
        ]
    outs = pl.pallas_call(
        functools.partial(_mlstm_kernel, seq=seq, has_init=has_init, emit_state=emit_state),
        grid=(batch, n_pairs),
        in_specs=in_specs,
        out_specs=out_specs,
        out_shape=out_shape,
        scratch_shapes=[
            pltpu.VMEM((4 * hp, seq, LANE), F32), pltpu.VMEM((2 * hp, DK_B, DV_B), F32),
            pltpu.VMEM((2 * hp, 1, DK_B), F32), pltpu.VMEM((2 * hp, 1, LANE), F32),
            pltpu.VMEM((2 * hp, seq, DV_B), F32),
        ],
        compiler_params=_cparams("parallel", "parallel"),
    )(*args)
    if emit_state:
        return outs[0], (outs[1], outs[2][:, :, :, 0, :], outs[3][:, :, :, 0, 0])
    return outs[0], None


Q_SLABS_C = HEADS_C * HEAD_DIM_C // LANE
HEAD_ORDER_C = [h for s in range(Q_SLABS_C) for h in (s, s + GROUP_C)]


def _attend(q_slabs, segments, sink):
    lane = lax.broadcasted_iota(jnp.int32, (1, LANE), 1)
    scale = HEAD_DIM_C ** -0.5
    nt = (((1,), (1,)), ((), ()))
    operands = []
    for kv in range(KV_HEADS_C):
        mine = (lane >= kv * HEAD_DIM_C) & (lane < (kv + 1) * HEAD_DIM_C)
        operands.append([(_bf(jnp.where(mine, k, 0.0)), _bf(jnp.where(mine, v, 0.0)), valid)
                         for k, v, valid in segments])
    problems = [(s, kv) for s in range(len(q_slabs)) for kv in range(KV_HEADS_C)]
    q_bf = [_bf(q) for q in q_slabs]
    scores = [[lax.dot_general(q_bf[s], kb, nt, preferred_element_type=F32) * scale for kb, _, _ in operands[kv]]
              for s, kv in problems]
    weights = []
    for (s, kv), per_seg in zip(problems, scores):
        per_seg = [x if valid is None else jnp.where(valid, x, -jnp.inf)
                   for x, (_, _, valid) in zip(per_seg, operands[kv])]
        head = kv * GROUP_C + s
        tiles = [x[:, t * LANE:(t + 1) * LANE] for x in per_seg for t in range(x.shape[1] // LANE)]
        m = jnp.maximum(jnp.max(functools.reduce(jnp.maximum, tiles), axis=1, keepdims=True),
                        sink[head:head + 1, 0:1])
        es = [jnp.exp(x - m) for x in per_seg]
        e_tiles = [e[:, t * LANE:(t + 1) * LANE] for e in es for t in range(e.shape[1] // LANE)]
        den = (jnp.sum(functools.reduce(jnp.add, e_tiles), axis=1, keepdims=True)
               + jnp.exp(sink[head:head + 1, 0:1] - m))
        weights.append((es, den))
    outs = []
    for (s, kv), (es, den) in zip(problems, weights):
        acc = None
        for e, (_, vb, _) in zip(es, operands[kv]):
            part = jnp.dot(_bf(e), vb, preferred_element_type=F32)
            acc = part if acc is None else acc + part
        outs.append(acc / den)
    return [sum(outs[s * KV_HEADS_C + 1:(s + 1) * KV_HEADS_C], outs[s * KV_HEADS_C]) for s in range(len(q_slabs))]


def _ctx_attn_kernel(q_ref, k_ref, v_ref, sink_ref, o_ref, *, seq):
    q_slabs = [q_ref[:, s * LANE:(s + 1) * LANE] for s in range(Q_SLABS_C)]
    outs = _attend(q_slabs, [(k_ref[...], v_ref[...], None)], sink_ref[0])
    for s, o in enumerate(outs):
        o_ref[:, s * LANE:(s + 1) * LANE] = o


def _ctx_attention(proj, sink, layer, batch, seq):
    width = HEADS_C * HEAD_DIM_C
    return pl.pallas_call(
        functools.partial(_ctx_attn_kernel, seq=seq),
        grid=(batch,),
        in_specs=[
            pl.BlockSpec((seq, width), lambda r: (r, T_QC * LANE // width)),
            pl.BlockSpec((seq, LANE), lambda r: (r, T_KC)),
            pl.BlockSpec((seq, LANE), lambda r: (r, T_VC)),
            pl.BlockSpec((1, HEADS_C, LANE), lambda r: (layer, 0, 0)),
        ],
        out_specs=pl.BlockSpec((seq, width), lambda r: (r, 0)),
        out_shape=jax.ShapeDtypeStruct((batch * seq, width), F32),
        compiler_params=_cparams("parallel"),
    )(proj, proj, proj, sink)


def _rope(x, cos, sin):
    quarter = HEAD_DIM_C // 4
    lane = lax.broadcasted_iota(jnp.int32, (1, LANE), 1)
    first = (lane % (2 * quarter)) < quarter
    partner = jnp.where(first, -pltpu.roll(x, LANE - quarter, 1), pltpu.roll(x, quarter, 1))
    return x * cos + partner * sin


def _latent_attn_kernel(q_ref, k_ref, v_ref, ck_ref, cv_ref, cq_ref, sq_ref, cos_ref, sin_ref, sink_ref, o_ref, *, seq):
    blk = pl.program_id(1)
    span = Q_BLOCK + 2 * WINDOW
    start = blk * Q_BLOCK
    k_start = pl.multiple_of(jnp.clip(start - WINDOW, 0, seq - span), Q_BLOCK)
    win = pl.ds(k_start, span)
    cq = cq_ref[...]
    sq = sq_ref[...]
    q_slabs = [_rope(q_ref[:, s * LANE:(s + 1) * LANE], cq, sq) for s in range(Q_SLABS_C)]
    k = _rope(k_ref[win, :], cos_ref[win, :], sin_ref[win, :])
    q_pos = start + lax.broadcasted_iota(jnp.int32, (Q_BLOCK, 1), 0)
    k_pos = k_start + lax.broadcasted_iota(jnp.int32, (1, span), 1)
    valid = jnp.abs(q_pos - k_pos) <= WINDOW
    outs = _attend(q_slabs, [(k, v_ref[win, :], valid), (ck_ref[0, 0], cv_ref[0, 0], None)], sink_ref[0])
    for s, o in enumerate(outs):
        o_ref[:, s * LANE:(s + 1) * LANE] = o


def _latent_attention(proj, cache_k, cache_v, cos, sin, sink, layer, batch, seq):
    width = HEADS_C * HEAD_DIM_C
    n_blk = seq // Q_BLOCK
    past = cache_k.shape[2]
    return pl.pallas_call(
        functools.partial(_latent_attn_kernel, seq=seq),
        grid=(batch, n_blk),
        in_specs=[
            pl.BlockSpec((Q_BLOCK, width), lambda r, i: (r * n_blk + i, T_QC * LANE // width)),
            pl.BlockSpec((seq, LANE), lambda r, i: (r, T_KC)),
            pl.BlockSpec((seq, LANE), lambda r, i: (r, T_VC)),
            pl.BlockSpec((1, 1, past, LANE), lambda r, i: (r, layer, 0, 0)),
            pl.BlockSpec((1, 1, past, LANE), lambda r, i: (r, layer, 0, 0)),
            pl.BlockSpec((Q_BLOCK, LANE), lambda r, i: (i, 0)),
            pl.BlockSpec((Q_BLOCK, LANE), lambda r, i: (i, 0)),
            pl.BlockSpec((seq, LANE), lambda r, i: (0, 0)),
            pl.BlockSpec((seq, LANE), lambda r, i: (0, 0)),
            pl.BlockSpec((1, HEADS_C, LANE), lambda r, i: (layer, 0, 0)),
        ],
        out_specs=pl.BlockSpec((Q_BLOCK, width), lambda r, i: (r * n_blk + i, 0)),
        out_shape=jax.ShapeDtypeStruct((batch * seq, width), F32),
        compiler_params=_cparams("parallel", "parallel"),
    )(proj, proj, proj, cache_k, cache_v, cos, sin, cos, sin, sink)


def _rope_tables(seq):
    quarter = HEAD_DIM_C // 4
    pos = jnp.arange(seq)
    row = (pos // GRID_W).astype(F32)
    col = (pos % GRID_W).astype(F32)
    inv = jnp.power(ROPE_BASE, -jnp.arange(quarter, dtype=F32) / quarter)
    ang_row = row[:, None] * inv[None, :]
    ang_col = col[:, None] * inv[None, :]
    ang = jnp.concatenate([ang_row, ang_row, ang_col, ang_col], axis=1)
    ang = jnp.concatenate([ang] * (LANE // HEAD_DIM_C), axis=1)
    return jnp.cos(ang), jnp.sin(ang)


def _mix_kernel(x_ref, ya_ref, yb_ref, yc_ref, ga_ref, gb_ref, gc_ref, mod_ref, n2_ref,
                wa_ref, wb_ref, wc_ref, wo_ref, wr_ref, xo_ref, h_ref, aff_ref, afft_ref):
    m = mod_ref[0]
    mixed = (_sigmoid(ga_ref[...]) * _mm(ya_ref[...], wa_ref[0])
             + _sigmoid(gb_ref[...]) * _mm(yb_ref[...], wb_ref[0])
             + _sigmoid(gc_ref[...]) * _mm(yc_ref[...], wc_ref[0]))
    x = x_ref[...] + m[2:3] * _mm(mixed, wo_ref[0])
    xo_ref[...] = x
    h = _bf(_rms(x, n2_ref[0]) * (1.0 + m[4:5]) + m[3:4])
    h_ref[...] = h
    logits = jnp.dot(h, wr_ref[0], preferred_element_type=F32)
    lane = lax.broadcasted_iota(jnp.int32, (1, LANE), 1)
    logits = jnp.where(lane < N_EXPERTS, logits, -jnp.inf)
    e = jnp.exp(logits - jnp.max(logits, axis=1, keepdims=True))
    aff = e / jnp.sum(e, axis=1, keepdims=True)
    aff_ref[...] = aff
    afft_ref[...] = aff.T[:N_EXPERTS, :]


def _mix(x2d, ya, yb, yc, proj, mod, norm2_g, wa, wb, wc, wo, wr, layer, seq, per_request):
    m_rows = x2d.shape[0]
    tm = 256
    gate_blk = lambda tile: (lambda i: (i, tile * LANE // D_MODEL))
    wspec = lambda w: pl.BlockSpec((1,) + w.shape[1:], lambda i: (layer, 0, 0))
    branch = pl.BlockSpec((tm, ya.shape[1]), lambda i: (i, 0))
    return pl.pallas_call(
        _mix_kernel,
        grid=(m_rows // tm,),
        in_specs=[
            pl.BlockSpec((tm, D_MODEL), lambda i: (i, 0)),
            branch, branch, branch,
            pl.BlockSpec((tm, D_MODEL), gate_blk(T_GA)),
            pl.BlockSpec((tm, D_MODEL), gate_blk(T_GB)),
            pl.BlockSpec((tm, D_MODEL), gate_blk(T_GC)),
            pl.BlockSpec((1, MOD_ROWS, D_MODEL), lambda i: ((i * tm) // seq if per_request else 0, 0, 0)),
            pl.BlockSpec((1, 1, D_MODEL), lambda i: (layer, 0, 0)),
            wspec(wa), wspec(wb), wspec(wc), wspec(wo), wspec(wr),
        ],
        out_specs=[
            pl.BlockSpec((tm, D_MODEL), lambda i: (i, 0)),
            pl.BlockSpec((tm, D_MODEL), lambda i: (i, 0)),
            pl.BlockSpec((tm, LANE), lambda i: (i, 0)),
            pl.BlockSpec((N_EXPERTS, tm), lambda i: (0, i)),
        ],
        out_shape=[
            jax.ShapeDtypeStruct((m_rows, D_MODEL), F32),
            jax.ShapeDtypeStruct((m_rows, D_MODEL), BF16),
            jax.ShapeDtypeStruct((m_rows, LANE), F32),
            jax.ShapeDtypeStruct((N_EXPERTS, m_rows), F32),
        ],
        compiler_params=_cparams("parallel"),
    )(x2d, ya, yb, yc, proj, proj, proj, mod, norm2_g.reshape(DEPTH, 1, D_MODEL), wa, wb, wc, wo, wr)


SCATTER_K = 512


def _gather_kernel(aff_ref, afft_ref, h_ref, xs_ref, gate_ref, rankc_ref, rank_scr, onehot_scr, col_scr, *,
                   seq, cap):
    n_blk = seq // LANE
    ri = lax.broadcasted_iota(jnp.int32, (LANE, LANE), 0)
    ci = lax.broadcasted_iota(jnp.int32, (LANE, LANE), 1)
    lane = lax.broadcasted_iota(jnp.int32, (1, LANE), 1)
    slot = lax.broadcasted_iota(jnp.int32, (cap, 1), 0).astype(F32)

    def expert_body(e, carry):
        a_row = afft_ref[pl.ds(e, 1), :]
        a_col = jnp.sum(jnp.where(lane == e, aff_ref[...], 0.0), axis=1, keepdims=True)
        col_scr[...] = jnp.broadcast_to(a_col, (seq, LANE))
        rank_parts = []
        for tb in range(n_blk):
            a_t = a_row[:, tb * LANE:(tb + 1) * LANE]
            count = jnp.zeros((LANE, LANE), F32)
            for sb in range(n_blk):
                a_s = col_scr[sb * LANE:(sb + 1) * LANE, :]
                if sb < tb:
                    beats = a_s >= a_t
                elif sb > tb:
                    beats = a_s > a_t
                else:
                    beats = (a_s > a_t) | ((a_s == a_t) & (ri < ci))
                count = count + jnp.where(beats, 1.0, 0.0)
            rank_parts.append(jnp.sum(count, axis=0, keepdims=True))
        rank = jnp.concatenate(rank_parts, axis=1)
        rank_scr[pl.ds(e, 1), :] = rank
        chosen = rank == slot
        onehot_scr[pl.ds(pl.multiple_of(e * cap, cap), cap), :] = jnp.where(chosen, 1.0, 0.0).astype(BF16)
        gate_ref[e, 0] = jnp.sum(jnp.where(chosen, a_row, 0.0), axis=1, keepdims=True)
        return carry

    lax.fori_loop(0, N_EXPERTS, expert_body, 0)
    ranks = jnp.concatenate([rank_scr[...], jnp.zeros((LANE - N_EXPERTS, seq), F32)], axis=0)
    rankc_ref[...] = ranks.T
    per_group = SCATTER_K // cap
    h = h_ref[...]
    for i in range(N_EXPERTS // per_group):
        rows = jnp.dot(onehot_scr[i * SCATTER_K:(i + 1) * SCATTER_K, :], h, preferred_element_type=F32)
        xs_ref[i * per_group:(i + 1) * per_group, 0] = _bf(rows).reshape(per_group, cap, D_MODEL)


def _gather(aff, afft, h2, batch, seq, cap):
    return pl.pallas_call(
        functools.partial(_gather_kernel, seq=seq, cap=cap),
        grid=(batch,),
        in_specs=[
            pl.BlockSpec((seq, LANE), lambda r: (r, 0)),
            pl.BlockSpec((N_EXPERTS, seq), lambda r: (0, r)),
            pl.BlockSpec((seq, D_MODEL), lambda r: (r, 0)),
        ],
        out_specs=[
            pl.BlockSpec((N_EXPERTS, 1, cap, D_MODEL), lambda r: (0, r, 0, 0)),
            pl.BlockSpec((N_EXPERTS, 1, cap, 1), lambda r: (0, r, 0, 0)),
            pl.BlockSpec((seq, LANE), lambda r: (r, 0)),
        ],
        out_shape=[
            jax.ShapeDtypeStruct((N_EXPERTS, batch, cap, D_MODEL), BF16),
            jax.ShapeDtypeStruct((N_EXPERTS, batch, cap, 1), F32),
            jax.ShapeDtypeStruct((batch * seq, LANE), F32),
        ],
        scratch_shapes=[pltpu.VMEM((N_EXPERTS, seq), F32), pltpu.VMEM((N_EXPERTS * cap, seq), BF16),
                        pltpu.VMEM((seq, LANE), F32)],
        compiler_params=_cparams("parallel"),
    )(aff, afft, h2)


def _ffn_kernel(x_ref, gate_ref, wg_ref, wu_ref, wd_ref, hi_ref, lo_ref):
    x = x_ref[0]
    hid = _silu(_mm(x, wg_ref[0, 0])) * _mm(x, wu_ref[0, 0])
    y = _mm(hid, wd_ref[0, 0]) * gate_ref[0]
    hi, lo = _split2(y)
    hi_ref[0] = hi
    lo_ref[0] = lo


def _expert_ffn(xs, gate, w_gate, w_up, w_down, layer):
    n_rows = xs.shape[1]
    rows = pl.BlockSpec((1, n_rows, D_MODEL), lambda e: (e, 0, 0))
    out = jax.ShapeDtypeStruct((N_EXPERTS, n_rows, D_MODEL), BF16)
    return pl.pallas_call(
        _ffn_kernel,
        grid=(N_EXPERTS,),
        in_specs=[
            rows,
            pl.BlockSpec((1, n_rows, 1), lambda e: (e, 0, 0)),
            pl.BlockSpec((1, 1, D_MODEL, D_EXPERT), lambda e: (layer, e, 0, 0)),
            pl.BlockSpec((1, 1, D_MODEL, D_EXPERT), lambda e: (layer, e, 0, 0)),
            pl.BlockSpec((1, 1, D_EXPERT, D_MODEL), lambda e: (layer, e, 0, 0)),
        ],
        out_specs=[rows, rows],
        out_shape=[out, out],
        compiler_params=_cparams("parallel"),
    )(xs, gate, w_gate, w_up, w_down)


def _scatter_kernel(x_ref, rankc_ref, hi_ref, lo_ref, mod_ref, fg_ref, o_ref, acc, *, cap, final):
    g = pl.program_id(1)
    n_groups = pl.num_programs(1)
    per_lane_tile = LANE // cap
    rank_parts = _split3(rankc_ref[...])
    row = lax.broadcasted_iota(jnp.int32, (LANE, LANE), 0)
    lane = lax.broadcasted_iota(jnp.int32, (LANE, LANE), 1)
    lane_slot = (lax.broadcasted_iota(jnp.int32, (1, LANE), 1) % cap).astype(F32)
    d = functools.partial(jnp.dot, preferred_element_type=F32)
    tiles = []
    for b in range(SCATTER_K // LANE):
        first = (g * (SCATTER_K // LANE) + b) * per_lane_tile
        sel = jnp.where(row == first + lane // cap, 1.0, 0.0).astype(BF16)
        hi, mid, lo = rank_parts
        token_rank = (d(hi, sel) + d(mid, sel)) + d(lo, sel)
        tiles.append(jnp.where(token_rank == lane_slot, 1.0, 0.0).astype(BF16))
    onehot = jnp.concatenate(tiles, axis=1)
    spread = (d(onehot, hi_ref[:, 0].reshape(SCATTER_K, D_MODEL))
              + d(onehot, lo_ref[:, 0].reshape(SCATTER_K, D_MODEL)))

    @pl.when(g == 0)
    def _():
        acc[...] = spread

    @pl.when(g > 0)
    def _():
        acc[...] += spread

    @pl.when(g == n_groups - 1)
    def _():
        x = x_ref[...] + mod_ref[0][5:6] * acc[...]
        o_ref[...] = _rms(x, fg_ref[...]) if final else x


def _scatter(x2d, rankc, ye_hi, ye_lo, mod, final_g, batch, seq, cap, per_request, final):
    experts_per_group = SCATTER_K // cap
    slots = pl.BlockSpec((experts_per_group, 1, cap, D_MODEL), lambda r, g: (g, r, 0, 0))
    return pl.pallas_call(
        functools.partial(_scatter_kernel, cap=cap, final=final),
        grid=(batch, N_EXPERTS // experts_per_group),
        in_specs=[
            pl.BlockSpec((seq, D_MODEL), lambda r, g: (r, 0)),
            pl.BlockSpec((seq, LANE), lambda r, g: (r, 0)),
            slots, slots,
            pl.BlockSpec((1, MOD_ROWS, D_MODEL), lambda r, g: (r if per_request else 0, 0, 0)),
            pl.BlockSpec((1, D_MODEL), lambda r, g: (0, 0)),
        ],
        out_specs=pl.BlockSpec((seq, D_MODEL), lambda r, g: (r, 0)),
        out_shape=jax.ShapeDtypeStruct((batch * seq, D_MODEL), F32),
        scratch_shapes=[pltpu.VMEM((seq, D_MODEL), F32)],
        compiler_params=_cparams("parallel", "arbitrary"),
    )(x2d, rankc, ye_hi, ye_lo, mod, final_g.reshape(1, D_MODEL))


def _reorder_w_in(w_in):
    a_main = w_in[:, :, 0:2048]
    a_gate = w_in[:, :, 2048:2064]
    b_main = w_in[:, :, 2064:3600]
    b_gate = w_in[:, :, 3600:3616]
    c_main = w_in[:, :, 3616:4384]
    merge = w_in[:, :, 4384:7456]

    def per_head(g, heads):
        g = g.reshape(DEPTH, D_MODEL, 4, heads).transpose(0, 1, 3, 2)
        g = jnp.pad(g, ((0, 0), (0, 0), (0, 0), (0, GATE_STRIDE - 4)))
        return g.reshape(DEPTH, D_MODEL, heads * GATE_STRIDE)

    width_q = HEADS_C * HEAD_DIM_C
    qc = c_main[:, :, :width_q].reshape(DEPTH, D_MODEL, HEADS_C, HEAD_DIM_C)
    qc = jnp.stack([qc[:, :, h] for h in HEAD_ORDER_C], axis=2).reshape(DEPTH, D_MODEL, width_q)
    c_main = jnp.concatenate([qc, c_main[:, :, width_q:]], axis=-1)
    gates = jnp.concatenate([per_head(a_gate, HEADS_A), per_head(b_gate, HEADS_B)], axis=-1)
    pad = jnp.zeros((DEPTH, D_MODEL, N_PROJ - T_GATES * LANE - gates.shape[-1]), w_in.dtype)
    return _bf(jnp.concatenate([merge, a_main, b_main, c_main, gates, pad], axis=-1))


def _lane_rows(rows):
    p = jnp.stack(rows, axis=-1)
    p = p.reshape(-1, len(rows))
    return jnp.broadcast_to(p[:, :, None], p.shape + (LANE,)).astype(F32)


def kernel(x_prompt, x_sample, cache_attn_k, cache_attn_v, state_delta, state_mlstm_c, state_mlstm_n, state_mlstm_m, c, c_ctx, ada_w, ada_b, norm1_g, norm2_g, w_in, conv_qkv_a, delta_a_log, delta_dt_bias, delta_norm_g, mlstm_i_bias, mlstm_f_bias, mlstm_norm_g, attn_sink, w_branch_a, w_branch_b, w_branch_c, w_out, w_router, w_expert_gate, w_expert_up, w_expert_down, final_norm_g):
    batch_p, seq_p, _ = x_prompt.shape
    batch_s, seq_s, _ = x_sample.shape
    past = cache_attn_k.shape[2]

    cond = jnp.concatenate([c_ctx[None, :], c, jnp.zeros((COND_ROWS - 1 - batch_s, D_MODEL), F32)], axis=0)
    mod = _modulation(cond, ada_w, ada_b).reshape(DEPTH, COND_ROWS, ADA_CHUNKS, D_MODEL)
    mod = jnp.pad(mod, ((0, 0), (0, 0), (0, MOD_ROWS - ADA_CHUNKS), (0, 0)))

    w_in_r = _reorder_w_in(w_in)
    wa, wb, wo = _bf(w_branch_a), _bf(w_branch_b), _bf(w_out)
    wc = w_branch_c.reshape(DEPTH, HEADS_C, HEAD_DIM_C, D_MODEL)
    wc = _bf(jnp.stack([wc[:, h] for h in HEAD_ORDER_C], axis=1).reshape(DEPTH, HEADS_C * HEAD_DIM_C, D_MODEL))
    wr = _bf(jnp.pad(w_router, ((0, 0), (0, 0), (0, LANE - N_EXPERTS))))
    par_a = _lane_rows([delta_a_log[:, 0], delta_a_log[:, 1], delta_dt_bias[:, 0], delta_dt_bias[:, 1]])
    par_b = _lane_rows([mlstm_i_bias[:, 0], mlstm_i_bias[:, 1], mlstm_f_bias[:, 0], mlstm_f_bias[:, 1]])
    sink = jnp.broadcast_to(attn_sink[:, :, None], (DEPTH, HEADS_C, LANE)).astype(F32)
    cache_k = cache_attn_k.reshape(batch_s, DEPTH, past, KV_HEADS_C * HEAD_DIM_C)
    cache_v = cache_attn_v.reshape(batch_s, DEPTH, past, KV_HEADS_C * HEAD_DIM_C)
    state_n = state_mlstm_n.reshape(batch_s, DEPTH, 2, HEADS_B, 1, DK_B)
    state_m = jnp.broadcast_to(state_mlstm_m[..., None, None], (batch_s, DEPTH, 2, HEADS_B, 1, LANE)).astype(F32)
    cos, sin = _rope_tables(seq_s)

    def layer(x2d, l, batch, seq, latent):
        mod_l = mod[l, 1:1 + batch] if latent else mod[l, 0:1]
        cap = EC_CAPACITY * seq // N_EXPERTS
        proj = _in_proj(x2d, mod_l, norm1_g, w_in_r, l, seq, latent)
        ya, d_new = _delta_mixer(proj, conv_qkv_a, par_a, delta_norm_g, l, batch, seq,
                                 state_delta if latent else None)
        yb, b_new = _mlstm_mixer(proj, par_b, mlstm_norm_g, l, batch, seq,
                                 (state_mlstm_c, state_n, state_m) if latent else None)
        if latent:
            yc = _latent_attention(proj, cache_k, cache_v, cos, sin, sink, l, batch, seq)
        else:
            yc = _ctx_attention(proj, sink, l, batch, seq)
        x1, h2, aff, afft = _mix(x2d, ya, yb, yc, proj, mod_l, norm2_g, wa, wb, wc, wo, wr, l, seq, latent)
        xs, gate, rankc = _gather(aff, afft, h2, batch, seq, cap)
        ye_hi, ye_lo = _expert_ffn(xs.reshape(N_EXPERTS, batch * cap, D_MODEL),
                                   gate.reshape(N_EXPERTS, batch * cap, 1),
                                   w_expert_gate, w_expert_up, w_expert_down, l)
        slots = (N_EXPERTS, batch, cap, D_MODEL)
        x2 = _scatter(x1, rankc, ye_hi.reshape(slots), ye_lo.reshape(slots), mod_l, final_norm_g,
                      batch, seq, cap, latent, l == DEPTH - 1)
        return x2, proj, d_new, b_new

    xp = x_prompt.reshape(batch_p * seq_p, D_MODEL)
    ks, vs, ds, cs, ns, ms = [], [], [], [], [], []
    for l in range(DEPTH):
        xp, proj, d_new, (c_new, n_new, m_new) = layer(xp, l, batch_p, seq_p, False)
        ks.append(proj[:, T_KC * LANE:(T_KC + 1) * LANE].reshape(batch_p, seq_p, KV_HEADS_C, HEAD_DIM_C))
        vs.append(proj[:, T_VC * LANE:(T_VC + 1) * LANE].reshape(batch_p, seq_p, KV_HEADS_C, HEAD_DIM_C))
        ds.append(d_new)
        cs.append(c_new)
        ns.append(n_new)
        ms.append(m_new)

    xs = x_sample.reshape(batch_s * seq_s, D_MODEL)
    for l in range(DEPTH):
        xs, _, _, _ = layer(xs, l, batch_s, seq_s, True)

    stack = lambda parts: jnp.stack(parts, axis=1)
    return (xp.reshape(batch_p, seq_p, D_MODEL), xs.reshape(batch_s, seq_s, D_MODEL),
            stack(ks), stack(vs), stack(ds), stack(cs), stack(ns), stack(ms))
```

```python
import functools

import jax
import jax.numpy as jnp
from jax import lax
from jax.experimental import pallas as pl
from jax.experimental.pallas import tpu as pltpu

F32 = jnp.float32
BF16 = jnp.bfloat16

D_MODEL = 1024
DEPTH = 2
GRID_W = 64
EPS = 1e-6
HEADS_A = 4
DK_A = 128
DV_A = 128
CONV_K = 5
CHUNK = 64
HEADS_B = 4
DK_B = 64
DV_B = 128
HEADS_C = 8
KV_HEADS_C = 2
HEAD_DIM_C = 64
GROUP_C = HEADS_C // KV_HEADS_C
WINDOW = 128
Q_BLOCK = 128
ROPE_BASE = 10000.0
N_EXPERTS = 16
D_EXPERT = 512
EC_CAPACITY = 2
ADA_CHUNKS = 6

LANE = 128
MOD_ROWS = 8
COND_ROWS = 16

T_GA, T_GB, T_GC = 0, 8, 16
T_QA, T_KA, T_VA, T_ZA = 24, 28, 32, 36
T_QB, T_KB, T_VB, T_OB = 40, 42, 44, 48
T_QC, T_KC, T_VC = 52, 56, 57
T_GATES = 58
N_TILES = 60
N_PROJ = N_TILES * LANE
GATE_STRIDE = 8
GATE_B_OFF = HEADS_A * GATE_STRIDE

VMEM_LIMIT = 48 * 1024 * 1024


def _cparams(*sem):
    return pltpu.CompilerParams(dimension_semantics=sem, vmem_limit_bytes=VMEM_LIMIT)


def _bf(x):
    return x.astype(BF16)


def _mm(a, b):
    return jnp.dot(_bf(a), _bf(b), preferred_element_type=F32)


def _mm_nt(a, b):
    return lax.dot_general(_bf(a), _bf(b), (((1,), (1,)), ((), ())), preferred_element_type=F32)


def _mm_tn(a, b):
    return lax.dot_general(_bf(a), _bf(b), (((0,), (0,)), ((), ())), preferred_element_type=F32)


def _split2(x):
    hi = _bf(x)
    return hi, _bf(x - hi.astype(F32))


def _split3(x):
    hi = _bf(x)
    r = x - hi.astype(F32)
    mid = _bf(r)
    return hi, mid, _bf(r - mid.astype(F32))


def _mm_sel(sel, x):
    hi, mid, lo = _split3(x)
    d = functools.partial(jnp.dot, preferred_element_type=F32)
    return (d(sel, hi) + d(sel, mid)) + d(sel, lo)


def _mm_hi(a, b):
    ah, al = _split2(a)
    bh, bl = _split2(b)
    d = functools.partial(jnp.dot, preferred_element_type=F32)
    return d(ah, bh) + (d(ah, bl) + d(al, bh))


def _sigmoid(x):
    return 1.0 / (1.0 + jnp.exp(-x))


def _silu(x):
    return x * _sigmoid(x)


def _softplus(x):
    return jnp.maximum(x, 0.0) + jnp.log(1.0 + jnp.exp(-jnp.abs(x)))


def _rms(x, g):
    return x * lax.rsqrt(jnp.mean(x * x, axis=-1, keepdims=True) + EPS) * g


def _chunk_masks(backward):
    ri = lax.broadcasted_iota(jnp.int32, (CHUNK, CHUNK), 0)
    ci = lax.broadcasted_iota(jnp.int32, (CHUNK, CHUNK), 1)
    if backward:
        return ri <= ci, ri < ci, ri == ci
    return ri >= ci, ri > ci, ri == ci


def _mod_kernel(c_ref, w_ref, b_ref, o_ref):
    o_ref[0] = _mm(_silu(c_ref[...]), w_ref[0]) + b_ref[0]


def _modulation(cond, ada_w, ada_b):
    n_out = ADA_CHUNKS * D_MODEL
    tn = 512
    return pl.pallas_call(
        _mod_kernel,
        grid=(DEPTH, n_out // tn),
        in_specs=[
            pl.BlockSpec((COND_ROWS, D_MODEL), lambda l, j: (0, 0)),
            pl.BlockSpec((1, D_MODEL, tn), lambda l, j: (l, 0, j)),
            pl.BlockSpec((1, 1, tn), lambda l, j: (l, 0, j)),
        ],
        out_specs=pl.BlockSpec((1, COND_ROWS, tn), lambda l, j: (l, 0, j)),
        out_shape=jax.ShapeDtypeStruct((DEPTH, COND_ROWS, n_out), F32),
        compiler_params=_cparams("parallel", "parallel"),
    )(cond, ada_w, ada_b.reshape(DEPTH, 1, n_out))


def _activate_gates(x, bias, a_log):
    lane = lax.broadcasted_iota(jnp.int32, (1, LANE), 1)
    kind = lane % GATE_STRIDE
    is_delta = lane < GATE_B_OFF
    is_mlstm = (lane >= GATE_B_OFF) & (lane < GATE_B_OFF + HEADS_B * GATE_STRIDE)
    y = x + bias
    log_term = jnp.log(1.0 + jnp.exp(-jnp.abs(y)))
    out = jnp.where(is_delta & (kind < 2), _sigmoid(y), x)
    out = jnp.where(is_delta & (kind >= 2) & (kind < 4), -jnp.exp(a_log) * (jnp.maximum(y, 0.0) + log_term), out)
    out = jnp.where(is_mlstm & (kind < 2), y, out)
    return jnp.where(is_mlstm & (kind >= 2) & (kind < 4), -(jnp.maximum(-y, 0.0) + log_term), out)


def _in_proj_kernel(x_ref, mod_ref, g_ref, w_ref, gb_ref, ga_ref, o_ref, h_scr, *, gate_off):
    @pl.when(pl.program_id(1) == 0)
    def _():
        m = mod_ref[0]
        h_scr[...] = _bf(_rms(x_ref[...], g_ref[0]) * (1.0 + m[1:2]) + m[0:1])

    o_ref[...] = jnp.dot(h_scr[...], w_ref[0], preferred_element_type=F32)

    @pl.when(pl.program_id(1) == pl.num_programs(1) - 1)
    def _():
        lanes = slice(gate_off, gate_off + LANE)
        o_ref[:, lanes] = _activate_gates(o_ref[:, lanes], gb_ref[0], ga_ref[0])


def _in_proj(x2d, mod, norm_g, w_in, gate_bias, gate_a_log, layer, seq, per_request):
    m_rows = x2d.shape[0]
    tm = min(1024, m_rows)
    tn = 768
    if per_request:
        assert seq % tm == 0
    gate_off = T_GATES * LANE - (N_PROJ // tn - 1) * tn
    assert 0 <= gate_off and gate_off + LANE <= tn
    lane_row = pl.BlockSpec((1, 1, LANE), lambda i, j: (layer, 0, 0))
    return pl.pallas_call(
        functools.partial(_in_proj_kernel, gate_off=gate_off),
        grid=(m_rows // tm, N_PROJ // tn),
        in_specs=[
            pl.BlockSpec((tm, D_MODEL), lambda i, j: (i, 0)),
            pl.BlockSpec((1, MOD_ROWS, D_MODEL), lambda i, j: ((i * tm) // seq if per_request else 0, 0, 0)),
            pl.BlockSpec((1, 1, D_MODEL), lambda i, j: (layer, 0, 0)),
            pl.BlockSpec((1, D_MODEL, tn), lambda i, j: (layer, 0, j)),
            lane_row, lane_row,
        ],
        out_specs=pl.BlockSpec((tm, tn), lambda i, j: (i, j)),
        out_shape=jax.ShapeDtypeStruct((m_rows, N_PROJ), F32),
        scratch_shapes=[pltpu.VMEM((tm, D_MODEL), BF16)],
        compiler_params=_cparams("parallel", "arbitrary"),
    )(x2d, mod, norm_g.reshape(DEPTH, 1, D_MODEL), w_in, gate_bias, gate_a_log)


PREP_PROBLEMS = 8
DELTA_HEAD_TOKENS = 2048
INV_GROUP = 32


def _gate_dense(gt_parts, lane_index):
    row = lax.broadcasted_iota(jnp.int32, (LANE, LANE), 0)
    sel = jnp.where(row == lane_index, 1.0, 0.0).astype(BF16)
    hi, mid, lo = gt_parts
    d = functools.partial(jnp.dot, preferred_element_type=F32)
    return (d(hi, sel) + d(mid, sel)) + d(lo, sel)


def _delta_kernel(q_ref, k_ref, v_ref, z_ref, gt_ref, cq_ref, ck_ref, cv_ref, ng_ref, *rest,
                  seq, heads, has_init, emit_state):
    rest = list(rest)
    s0_ref = rest.pop(0) if has_init else None
    y_ref = rest.pop(0)
    st_ref = rest.pop(0) if emit_state else None
    qs, ks, vs, gates, s_scr, o_scr, u_scr, wq_scr, akd_scr, dk_scr, pw_scr, inv_scr, rhs_scr = rest
    head_group = pl.program_id(1)
    n_chunks = seq // CHUNK
    n_chain = 2 * heads
    rows_t = lax.broadcasted_iota(jnp.int32, (seq, 1), 0)
    head_lanes = lambda hd: slice(hd * DK_A, (hd + 1) * DK_A)

    def conv_silu(x_ref, w_ref):
        x = x_ref[...]
        w = w_ref[0]
        acc = x * w[CONV_K // 2:CONV_K // 2 + 1, :]
        for j in range(CONV_K):
            s = j - CONV_K // 2
            if s == 0:
                continue
            shifted = pltpu.roll(x, (-s) % seq, 0)
            ok = (rows_t + s >= 0) & (rows_t + s < seq)
            acc = acc + jnp.where(ok, shifted, 0.0) * w[j:j + 1, :]
        return _silu(acc)

    def l2n(x):
        parts = [x[:, head_lanes(hd)] for hd in range(heads)]
        return jnp.concatenate(
            [p * lax.rsqrt(jnp.sum(p * p, axis=-1, keepdims=True) + EPS) for p in parts], axis=1)

    qs[...] = l2n(conv_silu(q_ref, cq_ref)) * (DK_A ** -0.5)
    ks[...] = l2n(conv_silu(k_ref, ck_ref))
    vs[...] = conv_silu(v_ref, cv_ref)

    gt_parts = _split3(gt_ref[...])
    for hd in range(heads):
        base = (head_group * heads + hd) * GATE_STRIDE
        for kind in range(4):
            gates[hd * 4 + kind] = _gate_dense(gt_parts, base + kind)
        for d in range(2):
            s_scr[hd * 2 + d] = s0_ref[0, 0, d, hd] if has_init else jnp.zeros((DK_A, DV_A), F32)

    def chunk_rows(c):
        return pl.ds(pl.multiple_of(c * CHUNK, CHUNK), CHUNK)

    n_prob = n_chain * n_chunks
    group = min(INV_GROUP, n_prob)
    prep_chunks = max(1, PREP_PROBLEMS // n_chain)
    ri = lax.broadcasted_iota(jnp.int32, (CHUNK, 2 * CHUNK), 0)
    ci = lax.broadcasted_iota(jnp.int32, (CHUNK, 2 * CHUNK), 1) % CHUNK
    dot = functools.partial(jnp.dot, preferred_element_type=F32)

    def setup_body(i, carry):
        loaded = []
        for cc in range(prep_chunks):
            c = i * prep_chunks + cc
            rows = chunk_rows(c)
            for hd in range(heads):
                loaded.append((c, hd, qs[rows, head_lanes(hd)], ks[rows, head_lanes(hd)], vs[rows, head_lanes(hd)],
                               [(gates[hd * 4 + d, rows, :], gates[hd * 4 + 2 + d, rows, :]) for d in range(2)]))
        products = []
        for c, hd, q, k, v, gate_cols in loaded:
            per_dir = []
            for d in range(2):
                g = gate_cols[d][1]
                incl, strict, _ = _chunk_masks(d == 1)
                rhs = jnp.concatenate(
                    [jnp.where(strict, g[:, :CHUNK], 0.0), jnp.zeros((CHUNK, LANE - CHUNK), F32), g], axis=1)
                per_dir.append(_mm_sel(jnp.where(incl, 1.0, 0.0).astype(BF16), rhs))
            gram = _mm_nt(jnp.concatenate([k, q], axis=0), k)
            products.append((gram[:CHUNK], gram[CHUNK:], per_dir))
        results = []
        pairs = []
        for (c, hd, q, k, v, gate_cols), (kk, qk, per_dir) in zip(loaded, products):
            lows = []
            for d in range(2):
                bt, g = gate_cols[d]
                incl, strict, _ = _chunk_masks(d == 1)
                cs = per_dir[d]
                gc = cs[:, LANE:]
                g_last = gc[0:1, :] if d == 1 else gc[CHUNK - 1:CHUNK, :]
                egc = jnp.exp(gc)
                dec = jnp.where(incl, jnp.exp(cs[:, :CHUNK]), 0.0)
                lows.append(jnp.where(strict, bt[:, :CHUNK] * kk * dec, 0.0))
                rhs2 = jnp.concatenate([v * bt, k * (bt * egc)], axis=1)
                akd = _bf(jnp.concatenate([qk * dec, (k * jnp.exp(g_last - gc)).T], axis=0))
                results.append(((hd * n_chunks + c) * 2 + d, rhs2, akd, _bf(q * egc), jnp.exp(g_last)))
            pairs.append((hd * n_chunks + c, jnp.concatenate(lows, axis=1)))
        for p, rhs2, akd, qd, decay in results:
            rhs_scr[p] = rhs2
            akd_scr[p] = akd
            wq_scr[p, CHUNK:, :] = qd
            dk_scr[p] = decay
        for pp, low in pairs:
            pw_scr[pp] = low
            inv_scr[pp] = jnp.where(ri == ci, 1.0, 0.0) - jnp.where((ri // 2) == (ci // 2), low, 0.0)
        return carry

    lax.fori_loop(0, n_chunks // prep_chunks, setup_body, 0)

    lane_pair = lax.broadcasted_iota(jnp.int32, (CHUNK, 2 * CHUNK), 1)

    def block_diagonal(y):
        zero = jnp.zeros_like(y)
        return jnp.concatenate(
            [jnp.where(lane_pair < CHUNK, y, zero), jnp.where(lane_pair < CHUNK, zero, y)], axis=0)

    def pair_product(x, y):
        xh, xl = _split2(x)
        both = dot(jnp.concatenate([xh, xl], axis=0), block_diagonal(_bf(y)))
        return both[:CHUNK] + both[CHUNK:]

    n_pairs = n_prob // 2
    pair_group = min(INV_GROUP // 2, n_pairs)

    def doubling_pass(size):
        joins = ((ri // (2 * size)) == (ci // (2 * size))) & ((ri // size) != (ci // size))

        def body(i, carry):
            loaded = []
            for j in range(pair_group):
                pp = i * pair_group + j
                loaded.append((pp, pw_scr[pp], inv_scr[pp]))
            partial = [(pp, inv, pair_product(inv, jnp.where(joins, low, 0.0))) for pp, low, inv in loaded]
            results = [(pp, inv - pair_product(t, inv)) for pp, inv, t in partial]
            for pp, new_inv in results:
                inv_scr[pp] = new_inv
            return carry

        lax.fori_loop(0, n_pairs // pair_group, body, 0)

    size = 2
    while size < CHUNK:
        doubling_pass(size)
        size *= 2

    def solve_body(i, carry):
        loaded = []
        for j in range(pair_group):
            pp = i * pair_group + j
            inv = inv_scr[pp]
            for d in range(2):
                loaded.append((pp * 2 + d, inv[:, d * CHUNK:(d + 1) * CHUNK], rhs_scr[pp * 2 + d]))
        results = []
        for p, inv, rhs2 in loaded:
            ih, il = _split2(inv)
            rh, rl = _split2(rhs2)
            by_hi = dot(jnp.concatenate([ih, il], axis=0), rh)
            results.append((p, by_hi[:CHUNK] + (dot(ih, rl) + by_hi[CHUNK:])))
        for p, sol in results:
            u_scr[p] = sol[:, :DV_A]
            wq_scr[p, :CHUNK, :] = _bf(sol[:, DV_A:])
        return carry

    lax.fori_loop(0, n_pairs // pair_group, solve_body, 0)

    def scan_body(i, carry):
        loaded = []
        for chain in range(n_chain):
            c = n_chunks - 1 - i if chain % 2 == 1 else i
            p = ((chain // 2) * n_chunks + c) * 2 + chain % 2
            loaded.append((c, s_scr[chain], u_scr[p], wq_scr[p], akd_scr[p], dk_scr[p]))
        first = [dot(wq, _bf(state)) for c, state, u, wq, akd, decay in loaded]
        second = [dot(akd, _bf(u - ws[:CHUNK])) for (c, state, u, wq, akd, decay), ws in zip(loaded, first)]
        results = [(c, ws[CHUNK:] + av[:CHUNK], decay * state + av[CHUNK:])
                   for (c, state, u, wq, akd, decay), ws, av in zip(loaded, first, second)]
        for chain, (c, o, state) in enumerate(results):
            o_scr[chain, chunk_rows(c), :] = o
            s_scr[chain] = state
        return carry

    lax.fori_loop(0, n_chunks, scan_body, 0)

    z = z_ref[...]
    for hd in range(heads):
        y_ref[:, head_lanes(hd)] = (_rms(o_scr[hd * 2] + o_scr[hd * 2 + 1], ng_ref[0])
                                    * _silu(z[:, head_lanes(hd)]))
        if emit_state:
            for d in range(2):
                st_ref[0, d, hd] = s_scr[hd * 2 + d]


def _delta_mixer(proj, conv_w, norm_g, layer, batch, seq, state0):
    has_init = state0 is not None
    emit_state = not has_init
    heads = next(h for h in (4, 2, 1) if HEADS_A % h == 0 and h * seq <= max(DELTA_HEAD_TOKENS, seq))
    n_groups = HEADS_A // heads
    width = heads * DK_A
    col = lambda tile: (lambda r, g: (r, tile // heads + g))
    cw = lambda part: (lambda r, g: (layer, 0, part * n_groups + g))
    in_specs = [
        pl.BlockSpec((seq, width), col(T_QA)),
        pl.BlockSpec((seq, width), col(T_KA)),
        pl.BlockSpec((seq, width), col(T_VA)),
        pl.BlockSpec((seq, width), col(T_ZA)),
        pl.BlockSpec((seq, LANE), lambda r, g: (r, T_GATES)),
        pl.BlockSpec((1, CONV_K, width), cw(0)),
        pl.BlockSpec((1, CONV_K, width), cw(1)),
        pl.BlockSpec((1, CONV_K, width), cw(2)),
        pl.BlockSpec((1, 1, DV_A), lambda r, g: (layer, 0, 0)),
    ]
    args = [proj, proj, proj, proj, proj, conv_w, conv_w, conv_w, norm_g.reshape(DEPTH, 1, DV_A)]
    if has_init:
        in_specs.append(pl.BlockSpec((1, 1, 2, heads, DK_A, DV_A), lambda r, g: (r, layer, 0, g, 0, 0)))
        args.append(state0)
    out_specs = [pl.BlockSpec((seq, width), lambda r, g: (r, g))]
    out_shape = [jax.ShapeDtypeStruct((batch * seq, HEADS_A * DV_A), F32)]
    if emit_state:
        out_specs.append(pl.BlockSpec((1, 2, heads, DK_A, DV_A), lambda r, g: (r, 0, g, 0, 0)))
        out_shape.append(jax.ShapeDtypeStruct((batch, 2, HEADS_A, DK_A, DV_A), F32))
    n_chain = 2 * heads
    n_prob = n_chain * (seq // CHUNK)
    outs = pl.pallas_call(
        functools.partial(_delta_kernel, seq=seq, heads=heads, has_init=has_init, emit_state=emit_state),
        grid=(batch, n_groups),
        in_specs=in_specs,
        out_specs=out_specs,
        out_shape=out_shape,
        scratch_shapes=[
            pltpu.VMEM((seq, width), F32), pltpu.VMEM((seq, width), F32), pltpu.VMEM((seq, width), F32),
            pltpu.VMEM((2 * n_chain, seq, LANE), F32), pltpu.VMEM((n_chain, DK_A, DV_A), F32),
            pltpu.VMEM((n_chain, seq, DV_A), F32),
            pltpu.VMEM((n_prob, CHUNK, DV_A), F32), pltpu.VMEM((n_prob, 2 * CHUNK, DK_A), BF16),
            pltpu.VMEM((n_prob, CHUNK + DK_A, CHUNK), BF16), pltpu.VMEM((n_prob, 1, DV_A), F32),
            pltpu.VMEM((n_prob // 2, CHUNK, 2 * CHUNK), F32), pltpu.VMEM((n_prob // 2, CHUNK, 2 * CHUNK), F32),
            pltpu.VMEM((n_prob, CHUNK, DV_A + DK_A), F32),
        ],
        compiler_params=_cparams("parallel", "parallel"),
    )(*args)
    return (outs[0], outs[1]) if emit_state else (outs[0], None)


HEADS_PER_STEP_B = 4


def _mlstm_kernel(q_ref, k_ref, v_ref, og_ref, gt_ref, ng_ref, *rest, seq, has_init, emit_state):
    rest = list(rest)
    if has_init:
        c0_ref, n0_ref, m0_ref = rest[:3]
        rest = rest[3:]
    y_ref = rest.pop(0)
    if emit_state:
        co_ref, no_ref, mo_ref = rest[:3]
        rest = rest[3:]
    gates, c_scr, n_scr, m_scr, h_scr = rest
    pair = pl.program_id(1)
    n_chunks = seq // CHUNK
    gt_parts = _split3(gt_ref[...])

    for j in range(HEADS_PER_STEP_B):
        base = GATE_B_OFF + (pair * HEADS_PER_STEP_B + j) * GATE_STRIDE
        for kind in range(4):
            gates[j * 4 + kind] = _gate_dense(gt_parts, base + kind)
        for d in range(2):
            idx = j * 2 + d
            if has_init:
                c_scr[idx] = c0_ref[0, 0, d, j]
                n_scr[idx] = n0_ref[0, 0, d, j]
                m_scr[idx] = m0_ref[0, 0, d, j]
            else:
                c_scr[idx] = jnp.zeros((DK_B, DV_B), F32)
                n_scr[idx] = jnp.zeros((1, DK_B), F32)
                m_scr[idx] = jnp.zeros((1, LANE), F32)

    n_chain = 2 * HEADS_PER_STEP_B

    def scan_body(i, carry):
        loaded = []
        for chain in range(n_chain):
            j, d = chain // 2, chain % 2
            c = n_chunks - 1 - i if d == 1 else i
            rows = pl.ds(pl.multiple_of(c * CHUNK, CHUNK), CHUNK)
            loaded.append((chain, rows, q_ref[rows, j * DK_B:(j + 1) * DK_B],
                           k_ref[rows, j * DK_B:(j + 1) * DK_B] * (DK_B ** -0.5),
                           v_ref[rows, j * DV_B:(j + 1) * DV_B],
                           gates[j * 4 + d, rows, :], gates[j * 4 + 2 + d, rows, :],
                           c_scr[chain], n_scr[chain], m_scr[chain]))
        stage1 = []
        for chain, rows, q, k, v, ig, lf, cmat, nvec, m_prev in loaded:
            incl, strict, diag = _chunk_masks(chain % 2 == 1)
            rhs = jnp.concatenate(
                [jnp.where(strict, lf[:, :CHUNK], 0.0) + jnp.where(diag, ig[:, :CHUNK], 0.0),
                 jnp.zeros((CHUNK, LANE - CHUNK), F32), lf], axis=1)
            stage1.append((_mm_sel(jnp.where(incl, 1.0, 0.0).astype(BF16), rhs), _mm_nt(q, k), _mm(q, cmat)))
        stage2 = []
        for (chain, rows, q, k, v, ig, lf, cmat, nvec, m_prev), (cs, qk, qc) in zip(loaded, stage1):
            incl, _, _ = _chunk_masks(chain % 2 == 1)
            bc = cs[:, LANE:]
            b_last = bc[0:1, :] if chain % 2 == 1 else bc[CHUNK - 1:CHUNK, :]
            d_log = jnp.where(incl, cs[:, :CHUNK], -jnp.inf)
            d_max = jnp.max(d_log, axis=1, keepdims=True)
            tok = b_last - bc + ig
            m_new = jnp.maximum(b_last + m_prev, jnp.max(tok, axis=0, keepdims=True))
            w_prev = jnp.exp(b_last + m_prev - m_new)
            kw = k * jnp.exp(tok - m_new)[:, :DK_B]
            stage2.append((bc, d_log, d_max, m_new, w_prev, kw))
        stage3 = []
        for (chain, rows, q, k, v, ig, lf, cmat, nvec, m_prev), (cs, qk, qc), (bc, d_log, d_max, m_new, w_prev, kw) \
                in zip(loaded, stage1, stage2):
            m_t = jnp.maximum(bc + m_prev, d_max)
            w_inter = jnp.exp(bc + m_prev - m_t)
            pm = jnp.exp(d_log - m_t[:, :CHUNK]) * qk
            den = jnp.sum(w_inter[:, :DK_B] * (q * nvec) + pm, axis=1, keepdims=True)
            stage3.append((m_t, w_inter, den, _mm(pm, v), _mm_tn(kw, v)))
        results = []
        for (chain, rows, q, k, v, ig, lf, cmat, nvec, m_prev), (cs, qk, qc), (bc, d_log, d_max, m_new, w_prev, kw), \
                (m_t, w_inter, den, pv, inc) in zip(loaded, stage1, stage2, stage3):
            results.append((chain, rows, (w_inter * qc + pv) / jnp.maximum(jnp.abs(den), jnp.exp(-m_t)),
                            w_prev * cmat + inc,
                            w_prev[:, :DK_B] * nvec + jnp.sum(kw, axis=0, keepdims=True), m_new))
        for chain, rows, h, cmat, nvec, m_new in results:
            h_scr[chain, rows, :] = h
            c_scr[chain] = cmat
            n_scr[chain] = nvec
            m_scr[chain] = m_new
        return carry

    lax.fori_loop(0, n_chunks, scan_body, 0)

    og = og_ref[...]
    for j in range(HEADS_PER_STEP_B):
        h = h_scr[j * 2] + h_scr[j * 2 + 1]
        y_ref[:, j * DV_B:(j + 1) * DV_B] = _rms(h, ng_ref[0]) * _sigmoid(og[:, j * DV_B:(j + 1) * DV_B])
        if emit_state:
            for d in range(2):
                co_ref[0, d, j] = c_scr[j * 2 + d]
                no_ref[0, d, j] = n_scr[j * 2 + d]
                mo_ref[0, d, j] = m_scr[j * 2 + d]


def _mlstm_mixer(proj, norm_g, layer, batch, seq, state0):
    has_init = state0 is not None
    emit_state = not has_init
    hp = HEADS_PER_STEP_B
    n_pairs = HEADS_B // hp
    in_specs = [
        pl.BlockSpec((seq, hp * DK_B), lambda r, p: (r, T_QB * LANE // (hp * DK_B) + p)),
        pl.BlockSpec((seq, hp * DK_B), lambda r, p: (r, T_KB * LANE // (hp * DK_B) + p)),
        pl.BlockSpec((seq, hp * DV_B), lambda r, p: (r, T_VB * LANE // (hp * DV_B) + p)),
        pl.BlockSpec((seq, hp * DV_B), lambda r, p: (r, T_OB * LANE // (hp * DV_B) + p)),
        pl.BlockSpec((seq, LANE), lambda r, p: (r, T_GATES)),
        pl.BlockSpec((1, 1, DV_B), lambda r, p: (layer, 0, 0)),
    ]
    args = [proj, proj, proj, proj, proj, norm_g.reshape(DEPTH, 1, DV_B)]
    if has_init:
        c0, n0, m0 = state0
        in_specs += [
            pl.BlockSpec((1, 1, 2, hp, DK_B, DV_B), lambda r, p: (r, layer, 0, p, 0, 0)),
            pl.BlockSpec((1, 1, 2, hp, 1, DK_B), lambda r, p: (r, layer, 0, p, 0, 0)),
            pl.BlockSpec((1, 1, 2, hp, 1, LANE), lambda r, p: (r, layer, 0, p, 0, 0)),
        ]
        args += [c0, n0, m0]
    out_specs = [pl.BlockSpec((seq, hp * DV_B), lambda r, p: (r, p))]
    out_shape = [jax.ShapeDtypeStruct((batch * seq, HEADS_B * DV_B), F32)]
    if emit_state:
        out_specs += [
            pl.BlockSpec((1, 2, hp, DK_B, DV_B), lambda r, p: (r, 0, p, 0, 0)),
            pl.BlockSpec((1, 2, hp, 1, DK_B), lambda r, p: (r, 0, p, 0, 0)),
            pl.BlockSpec((1, 2, hp, 1, LANE), lambda r, p: (r, 0, p, 0, 0)),
        ]
        out_shape += [
            jax.ShapeDtypeStruct((batch, 2, HEADS_B, DK_B, DV_B), F32),
            jax.ShapeDtypeStruct((batch, 2, HEADS_B, 1, DK_B), F32),
            jax.ShapeDtypeStruct((batch, 2, HEADS_B, 1, LANE), F32),
        ]
    outs = pl.pallas_call(
        functools.partial(_mlstm_kernel, seq=seq, has_init=has_init, emit_state=emit_state),
        grid=(batch, n_pairs),
        in_specs=in_specs,
        out_specs=out_specs,
        out_shape=out_shape,
        scratch_shapes=[
            pltpu.VMEM((4 * hp, seq, LANE), F32), pltpu.VMEM((2 * hp, DK_B, DV_B), F32),
            pltpu.VMEM((2 * hp, 1, DK_B), F32), pltpu.VMEM((2 * hp, 1, LANE), F32),
            pltpu.VMEM((2 * hp, seq, DV_B), F32),
        ],
        compiler_params=_cparams("parallel", "parallel"),
    )(*args)
    if emit_state:
        return outs[0], (outs[1], outs[2][:, :, :, 0, :], outs[3][:, :, :, 0, 0])
    return outs[0], None


Q_SLABS_C = HEADS_C * HEAD_DIM_C // LANE
HEAD_ORDER_C = [h for s in range(Q_SLABS_C) for h in (s, s + GROUP_C)]


def _attend(q_slabs, segments, sink):
    lane = lax.broadcasted_iota(jnp.int32, (1, LANE), 1)
    scale = HEAD_DIM_C ** -0.5
    nt = (((1,), (1,)), ((), ()))
    operands = []
    for kv in range(KV_HEADS_C):
        mine = (lane >= kv * HEAD_DIM_C) & (lane < (kv + 1) * HEAD_DIM_C)
        operands.append([(_bf(jnp.where(mine, k, 0.0)), _bf(jnp.where(mine, v, 0.0)), valid)
                         for k, v, valid in segments])
    problems = [(s, kv) for s in range(len(q_slabs)) for kv in range(KV_HEADS_C)]
    q_bf = [_bf(q) for q in q_slabs]
    scores = [[lax.dot_general(q_bf[s], kb, nt, preferred_element_type=F32) * scale for kb, _, _ in operands[kv]]
              for s, kv in problems]
    weights = []
    for (s, kv), per_seg in zip(problems, scores):
        per_seg = [x if valid is None else jnp.where(valid, x, -jnp.inf)
                   for x, (_, _, valid) in zip(per_seg, operands[kv])]
        head = kv * GROUP_C + s
        tiles = [x[:, t * LANE:(t + 1) * LANE] for x in per_seg for t in range(x.shape[1] // LANE)]
        m = jnp.maximum(jnp.max(functools.reduce(jnp.maximum, tiles), axis=1, keepdims=True),
                        sink[head:head + 1, 0:1])
        es = [jnp.exp(x - m) for x in per_seg]
        e_tiles = [e[:, t * LANE:(t + 1) * LANE] for e in es for t in range(e.shape[1] // LANE)]
        den = (jnp.sum(functools.reduce(jnp.add, e_tiles), axis=1, keepdims=True)
               + jnp.exp(sink[head:head + 1, 0:1] - m))
        weights.append((es, den))
    outs = []
    for (s, kv), (es, den) in zip(problems, weights):
        acc = None
        for e, (_, vb, _) in zip(es, operands[kv]):
            part = jnp.dot(_bf(e), vb, preferred_element_type=F32)
            acc = part if acc is None else acc + part
        outs.append(acc / den)
    return [sum(outs[s * KV_HEADS_C + 1:(s + 1) * KV_HEADS_C], outs[s * KV_HEADS_C]) for s in range(len(q_slabs))]


def _ctx_attn_kernel(q_ref, k_ref, v_ref, sink_ref, o_ref, *, seq):
    q_slabs = [q_ref[:, s * LANE:(s + 1) * LANE] for s in range(Q_SLABS_C)]
    outs = _attend(q_slabs, [(k_ref[...], v_ref[...], None)], sink_ref[0])
    for s, o in enumerate(outs):
        o_ref[:, s * LANE:(s + 1) * LANE] = o


def _ctx_attention(proj, sink, layer, batch, seq):
    width = HEADS_C * HEAD_DIM_C
    return pl.pallas_call(
        functools.partial(_ctx_attn_kernel, seq=seq),
        grid=(batch,),
        in_specs=[
            pl.BlockSpec((seq, width), lambda r: (r, T_QC * LANE // width)),
            pl.BlockSpec((seq, LANE), lambda r: (r, T_KC)),
            pl.BlockSpec((seq, LANE), lambda r: (r, T_VC)),
            pl.BlockSpec((1, HEADS_C, LANE), lambda r: (layer, 0, 0)),
        ],
        out_specs=pl.BlockSpec((seq, width), lambda r: (r, 0)),
        out_shape=jax.ShapeDtypeStruct((batch * seq, width), F32),
        compiler_params=_cparams("parallel"),
    )(proj, proj, proj, sink)


def _rope(x, cos, sin):
    quarter = HEAD_DIM_C // 4
    lane = lax.broadcasted_iota(jnp.int32, (1, LANE), 1)
    first = (lane % (2 * quarter)) < quarter
    partner = jnp.where(first, -pltpu.roll(x, LANE - quarter, 1), pltpu.roll(x, quarter, 1))
    return x * cos + partner * sin


def _latent_attn_kernel(q_ref, k_ref, v_ref, ck_ref, cv_ref, cq_ref, sq_ref, cos_ref, sin_ref, sink_ref, o_ref, *, seq):
    blk = pl.program_id(1)
    span = Q_BLOCK + 2 * WINDOW
    start = blk * Q_BLOCK
    k_start = pl.multiple_of(jnp.clip(start - WINDOW, 0, seq - span), Q_BLOCK)
    win = pl.ds(k_start, span)
    cq = cq_ref[...]
    sq = sq_ref[...]
    q_slabs = [_rope(q_ref[:, s * LANE:(s + 1) * LANE], cq, sq) for s in range(Q_SLABS_C)]
    k = _rope(k_ref[win, :], cos_ref[win, :], sin_ref[win, :])
    q_pos = start + lax.broadcasted_iota(jnp.int32, (Q_BLOCK, 1), 0)
    k_pos = k_start + lax.broadcasted_iota(jnp.int32, (1, span), 1)
    valid = jnp.abs(q_pos - k_pos) <= WINDOW
    outs = _attend(q_slabs, [(k, v_ref[win, :], valid), (ck_ref[0, 0], cv_ref[0, 0], None)], sink_ref[0])
    for s, o in enumerate(outs):
        o_ref[:, s * LANE:(s + 1) * LANE] = o


def _latent_attention(proj, cache_k, cache_v, cos, sin, sink, layer, batch, seq):
    width = HEADS_C * HEAD_DIM_C
    n_blk = seq // Q_BLOCK
    past = cache_k.shape[2]
    return pl.pallas_call(
        functools.partial(_latent_attn_kernel, seq=seq),
        grid=(batch, n_blk),
        in_specs=[
            pl.BlockSpec((Q_BLOCK, width), lambda r, i: (r * n_blk + i, T_QC * LANE // width)),
            pl.BlockSpec((seq, LANE), lambda r, i: (r, T_KC)),
            pl.BlockSpec((seq, LANE), lambda r, i: (r, T_VC)),
            pl.BlockSpec((1, 1, past, LANE), lambda r, i: (r, layer, 0, 0)),
            pl.BlockSpec((1, 1, past, LANE), lambda r, i: (r, layer, 0, 0)),
            pl.BlockSpec((Q_BLOCK, LANE), lambda r, i: (i, 0)),
            pl.BlockSpec((Q_BLOCK, LANE), lambda r, i: (i, 0)),
            pl.BlockSpec((seq, LANE), lambda r, i: (0, 0)),
            pl.BlockSpec((seq, LANE), lambda r, i: (0, 0)),
            pl.BlockSpec((1, HEADS_C, LANE), lambda r, i: (layer, 0, 0)),
        ],
        out_specs=pl.BlockSpec((Q_BLOCK, width), lambda r, i: (r * n_blk + i, 0)),
        out_shape=jax.ShapeDtypeStruct((batch * seq, width), F32),
        compiler_params=_cparams("parallel", "parallel"),
    )(proj, proj, proj, cache_k, cache_v, cos, sin, cos, sin, sink)


def _rope_tables(seq):
    quarter = HEAD_DIM_C // 4
    pos = jnp.arange(seq)
    row = (pos // GRID_W).astype(F32)
    col = (pos % GRID_W).astype(F32)
    inv = jnp.power(ROPE_BASE, -jnp.arange(quarter, dtype=F32) / quarter)
    ang_row = row[:, None] * inv[None, :]
    ang_col = col[:, None] * inv[None, :]
    ang = jnp.concatenate([ang_row, ang_row, ang_col, ang_col], axis=1)
    ang = jnp.concatenate([ang] * (LANE // HEAD_DIM_C), axis=1)
    return jnp.cos(ang), jnp.sin(ang)


def _mix_kernel(x_ref, ya_ref, yb_ref, yc_ref, ga_ref, gb_ref, gc_ref, mod_ref, n2_ref,
                wa_ref, wb_ref, wc_ref, wo_ref, wr_ref, xo_ref, h_ref, aff_ref, afft_ref):
    m = mod_ref[0]
    mixed = (_sigmoid(ga_ref[...]) * _mm(ya_ref[...], wa_ref[0])
             + _sigmoid(gb_ref[...]) * _mm(yb_ref[...], wb_ref[0])
             + _sigmoid(gc_ref[...]) * _mm(yc_ref[...], wc_ref[0]))
    x = x_ref[...] + m[2:3] * _mm(mixed, wo_ref[0])
    xo_ref[...] = x
    h = _bf(_rms(x, n2_ref[0]) * (1.0 + m[4:5]) + m[3:4])
    h_ref[...] = h
    logits = jnp.dot(h, wr_ref[0], preferred_element_type=F32)
    lane = lax.broadcasted_iota(jnp.int32, (1, LANE), 1)
    logits = jnp.where(lane < N_EXPERTS, logits, -jnp.inf)
    e = jnp.exp(logits - jnp.max(logits, axis=1, keepdims=True))
    aff = e / jnp.sum(e, axis=1, keepdims=True)
    aff_ref[...] = aff
    afft_ref[...] = aff.T[:N_EXPERTS, :]


def _mix(x2d, ya, yb, yc, proj, mod, norm2_g, wa, wb, wc, wo, wr, layer, seq, per_request):
    m_rows = x2d.shape[0]
    tm = min(512, seq)
    gate_blk = lambda tile: (lambda i: (i, tile * LANE // D_MODEL))
    wspec = lambda w: pl.BlockSpec((1,) + w.shape[1:], lambda i: (layer, 0, 0))
    branch = pl.BlockSpec((tm, ya.shape[1]), lambda i: (i, 0))
    return pl.pallas_call(
        _mix_kernel,
        grid=(m_rows // tm,),
        in_specs=[
            pl.BlockSpec((tm, D_MODEL), lambda i: (i, 0)),
            branch, branch, branch,
            pl.BlockSpec((tm, D_MODEL), gate_blk(T_GA)),
            pl.BlockSpec((tm, D_MODEL), gate_blk(T_GB)),
            pl.BlockSpec((tm, D_MODEL), gate_blk(T_GC)),
            pl.BlockSpec((1, MOD_ROWS, D_MODEL), lambda i: ((i * tm) // seq if per_request else 0, 0, 0)),
            pl.BlockSpec((1, 1, D_MODEL), lambda i: (layer, 0, 0)),
            wspec(wa), wspec(wb), wspec(wc), wspec(wo), wspec(wr),
        ],
        out_specs=[
            pl.BlockSpec((tm, D_MODEL), lambda i: (i, 0)),
            pl.BlockSpec((tm, D_MODEL), lambda i: (i, 0)),
            pl.BlockSpec((tm, LANE), lambda i: (i, 0)),
            pl.BlockSpec((N_EXPERTS, tm), lambda i: (0, i)),
        ],
        out_shape=[
            jax.ShapeDtypeStruct((m_rows, D_MODEL), F32),
            jax.ShapeDtypeStruct((m_rows, D_MODEL), BF16),
            jax.ShapeDtypeStruct((m_rows, LANE), F32),
            jax.ShapeDtypeStruct((N_EXPERTS, m_rows), F32),
        ],
        compiler_params=_cparams("parallel"),
    )(x2d, ya, yb, yc, proj, proj, proj, mod, norm2_g.reshape(DEPTH, 1, D_MODEL), wa, wb, wc, wo, wr)


SCATTER_K = 512


def _gather_kernel(aff_ref, afft_ref, h_ref, xs_ref, gate_ref, rankc_ref, rank_scr, onehot_scr, col_scr, *,
                   seq, cap):
    n_blk = seq // LANE
    ri = lax.broadcasted_iota(jnp.int32, (LANE, LANE), 0)
    ci = lax.broadcasted_iota(jnp.int32, (LANE, LANE), 1)
    lane = lax.broadcasted_iota(jnp.int32, (1, LANE), 1)
    slot = lax.broadcasted_iota(jnp.int32, (cap, 1), 0).astype(F32)

    def expert_body(e, carry):
        a_row = afft_ref[pl.ds(e, 1), :]
        a_col = jnp.sum(jnp.where(lane == e, aff_ref[...], 0.0), axis=1, keepdims=True)
        col_scr[...] = jnp.broadcast_to(a_col, (seq, LANE))
        rank_parts = []
        for tb in range(n_blk):
            a_t = a_row[:, tb * LANE:(tb + 1) * LANE]
            count = jnp.zeros((LANE, LANE), F32)
            for sb in range(n_blk):
                a_s = col_scr[sb * LANE:(sb + 1) * LANE, :]
                if sb < tb:
                    beats = a_s >= a_t
                elif sb > tb:
                    beats = a_s > a_t
                else:
                    beats = (a_s > a_t) | ((a_s == a_t) & (ri < ci))
                count = count + jnp.where(beats, 1.0, 0.0)
            rank_parts.append(jnp.sum(count, axis=0, keepdims=True))
        rank = jnp.concatenate(rank_parts, axis=1)
        rank_scr[pl.ds(e, 1), :] = rank
        chosen = rank == slot
        onehot_scr[pl.ds(pl.multiple_of(e * cap, cap), cap), :] = jnp.where(chosen, 1.0, 0.0).astype(BF16)
        gate_ref[e, 0] = jnp.sum(jnp.where(chosen, a_row, 0.0), axis=1, keepdims=True)
        return carry

    lax.fori_loop(0, N_EXPERTS, expert_body, 0)
    ranks = jnp.concatenate([rank_scr[...], jnp.zeros((LANE - N_EXPERTS, seq), F32)], axis=0)
    rankc_ref[...] = ranks.T
    per_group = SCATTER_K // cap
    h = h_ref[...]
    for i in range(N_EXPERTS // per_group):
        rows = jnp.dot(onehot_scr[i * SCATTER_K:(i + 1) * SCATTER_K, :], h, preferred_element_type=F32)
        xs_ref[i * per_group:(i + 1) * per_group, 0] = _bf(rows).reshape(per_group, cap, D_MODEL)


def _gather(aff, afft, h2, batch, seq, cap):
    return pl.pallas_call(
        functools.partial(_gather_kernel, seq=seq, cap=cap),
        grid=(batch,),
        in_specs=[
            pl.BlockSpec((seq, LANE), lambda r: (r, 0)),
            pl.BlockSpec((N_EXPERTS, seq), lambda r: (0, r)),
            pl.BlockSpec((seq, D_MODEL), lambda r: (r, 0)),
        ],
        out_specs=[
            pl.BlockSpec((N_EXPERTS, 1, cap, D_MODEL), lambda r: (0, r, 0, 0)),
            pl.BlockSpec((N_EXPERTS, 1, cap, 1), lambda r: (0, r, 0, 0)),
            pl.BlockSpec((seq, LANE), lambda r: (r, 0)),
        ],
        out_shape=[
            jax.ShapeDtypeStruct((N_EXPERTS, batch, cap, D_MODEL), BF16),
            jax.ShapeDtypeStruct((N_EXPERTS, batch, cap, 1), F32),
            jax.ShapeDtypeStruct((batch * seq, LANE), F32),
        ],
        scratch_shapes=[pltpu.VMEM((N_EXPERTS, seq), F32), pltpu.VMEM((N_EXPERTS * cap, seq), BF16),
                        pltpu.VMEM((seq, LANE), F32)],
        compiler_params=_cparams("parallel"),
    )(aff, afft, h2)


def _ffn_kernel(*refs, n_groups):
    x_refs, gate_refs = refs[:n_groups], refs[n_groups:2 * n_groups]
    wg_ref, wu_ref, wd_ref = refs[2 * n_groups:2 * n_groups + 3]
    y_refs = refs[2 * n_groups + 3:]
    wg, wu, wd = _bf(wg_ref[0, 0]), _bf(wu_ref[0, 0]), _bf(wd_ref[0, 0])
    d = functools.partial(jnp.dot, preferred_element_type=F32)
    for x_ref, gate_ref, y_ref in zip(x_refs, gate_refs, y_refs):
        x = x_ref[0]
        hid = _silu(d(x, wg)) * d(x, wu)
        y_ref[0] = _bf(d(_bf(hid), wd) * gate_ref[0])


def _expert_ffn(xs_groups, gate_groups, w_gate, w_up, w_down, layer):
    n_groups = len(xs_groups)
    row_specs = [pl.BlockSpec((1, xs.shape[1], D_MODEL), lambda e: (e, 0, 0)) for xs in xs_groups]
    gate_specs = [pl.BlockSpec((1, g.shape[1], 1), lambda e: (e, 0, 0)) for g in gate_groups]
    return pl.pallas_call(
        functools.partial(_ffn_kernel, n_groups=n_groups),
        grid=(N_EXPERTS,),
        in_specs=row_specs + gate_specs + [
            pl.BlockSpec((1, 1, D_MODEL, D_EXPERT), lambda e: (layer, e, 0, 0)),
            pl.BlockSpec((1, 1, D_MODEL, D_EXPERT), lambda e: (layer, e, 0, 0)),
            pl.BlockSpec((1, 1, D_EXPERT, D_MODEL), lambda e: (layer, e, 0, 0)),
        ],
        out_specs=row_specs,
        out_shape=[jax.ShapeDtypeStruct(xs.shape, BF16) for xs in xs_groups],
        compiler_params=_cparams("parallel"),
    )(*xs_groups, *gate_groups, w_gate, w_up, w_down)


def _scatter_kernel(x_ref, rankc_ref, ye_ref, mod_ref, fg_ref, o_ref, acc, *, cap, final):
    g = pl.program_id(1)
    n_groups = pl.num_programs(1)
    per_lane_tile = LANE // cap
    rank_parts = _split3(rankc_ref[...])
    row = lax.broadcasted_iota(jnp.int32, (LANE, LANE), 0)
    lane = lax.broadcasted_iota(jnp.int32, (LANE, LANE), 1)
    lane_slot = (lax.broadcasted_iota(jnp.int32, (1, LANE), 1) % cap).astype(F32)
    d = functools.partial(jnp.dot, preferred_element_type=F32)
    tiles = []
    for b in range(SCATTER_K // LANE):
        first = (g * (SCATTER_K // LANE) + b) * per_lane_tile
        sel = jnp.where(row == first + lane // cap, 1.0, 0.0).astype(BF16)
        hi, mid, lo = rank_parts
        token_rank = (d(hi, sel) + d(mid, sel)) + d(lo, sel)
        tiles.append(jnp.where(token_rank == lane_slot, 1.0, 0.0).astype(BF16))
    onehot = jnp.concatenate(tiles, axis=1)
    spread = d(onehot, ye_ref[:, 0].reshape(SCATTER_K, D_MODEL))

    @pl.when(g == 0)
    def _():
        acc[...] = spread

    @pl.when(g > 0)
    def _():
        acc[...] += spread

    @pl.when(g == n_groups - 1)
    def _():
        x = x_ref[...] + mod_ref[0][5:6] * acc[...]
        o_ref[...] = _rms(x, fg_ref[...]) if final else x


def _scatter(x2d, rankc, ye, mod, final_g, batch, seq, cap, per_request, final):
    experts_per_group = SCATTER_K // cap
    slots = pl.BlockSpec((experts_per_group, 1, cap, D_MODEL), lambda r, g: (g, r, 0, 0))
    return pl.pallas_call(
        functools.partial(_scatter_kernel, cap=cap, final=final),
        grid=(batch, N_EXPERTS // experts_per_group),
        in_specs=[
            pl.BlockSpec((seq, D_MODEL), lambda r, g: (r, 0)),
            pl.BlockSpec((seq, LANE), lambda r, g: (r, 0)),
            slots,
            pl.BlockSpec((1, MOD_ROWS, D_MODEL), lambda r, g: (r if per_request else 0, 0, 0)),
            pl.BlockSpec((1, D_MODEL), lambda r, g: (0, 0)),
        ],
        out_specs=pl.BlockSpec((seq, D_MODEL), lambda r, g: (r, 0)),
        out_shape=jax.ShapeDtypeStruct((batch * seq, D_MODEL), F32),
        scratch_shapes=[pltpu.VMEM((seq, D_MODEL), F32)],
        compiler_params=_cparams("parallel", "arbitrary"),
    )(x2d, rankc, ye, mod, final_g.reshape(1, D_MODEL))


def _reorder_w_in(w_in):
    a_main = w_in[:, :, 0:2048]
    a_gate = w_in[:, :, 2048:2064]
    b_main = w_in[:, :, 2064:3600]
    b_gate = w_in[:, :, 3600:3616]
    c_main = w_in[:, :, 3616:4384]
    merge = w_in[:, :, 4384:7456]

    def per_head(g, heads):
        g = g.reshape(DEPTH, D_MODEL, 4, heads).transpose(0, 1, 3, 2)
        g = jnp.pad(g, ((0, 0), (0, 0), (0, 0), (0, GATE_STRIDE - 4)))
        return g.reshape(DEPTH, D_MODEL, heads * GATE_STRIDE)

    width_q = HEADS_C * HEAD_DIM_C
    qc = c_main[:, :, :width_q].reshape(DEPTH, D_MODEL, HEADS_C, HEAD_DIM_C)
    qc = jnp.stack([qc[:, :, h] for h in HEAD_ORDER_C], axis=2).reshape(DEPTH, D_MODEL, width_q)
    c_main = jnp.concatenate([qc, c_main[:, :, width_q:]], axis=-1)
    gates = jnp.concatenate([per_head(a_gate, HEADS_A), per_head(b_gate, HEADS_B)], axis=-1)
    pad = jnp.zeros((DEPTH, D_MODEL, N_PROJ - T_GATES * LANE - gates.shape[-1]), w_in.dtype)
    return _bf(jnp.concatenate([merge, a_main, b_main, c_main, gates, pad], axis=-1))


def _gate_lane_rows(delta_kinds, mlstm_kinds):
    def block(kinds, heads):
        zero = jnp.zeros((DEPTH, heads), F32)
        cols = [zero if k is None else k.astype(F32) for k in kinds] + [zero] * (GATE_STRIDE - len(kinds))
        return jnp.stack(cols, axis=-1).reshape(DEPTH, heads * GATE_STRIDE)

    used = jnp.concatenate([block(delta_kinds, HEADS_A), block(mlstm_kinds, HEADS_B)], axis=-1)
    return jnp.pad(used, ((0, 0), (0, LANE - used.shape[-1])))[:, None, :]


def kernel(x_prompt, x_sample, cache_attn_k, cache_attn_v, state_delta, state_mlstm_c, state_mlstm_n, state_mlstm_m, c, c_ctx, ada_w, ada_b, norm1_g, norm2_g, w_in, conv_qkv_a, delta_a_log, delta_dt_bias, delta_norm_g, mlstm_i_bias, mlstm_f_bias, mlstm_norm_g, attn_sink, w_branch_a, w_branch_b, w_branch_c, w_out, w_router, w_expert_gate, w_expert_up, w_expert_down, final_norm_g):
    batch_p, seq_p, _ = x_prompt.shape
    batch_s, seq_s, _ = x_sample.shape
    past = cache_attn_k.shape[2]

    cond = jnp.concatenate([c_ctx[None, :], c, jnp.zeros((COND_ROWS - 1 - batch_s, D_MODEL), F32)], axis=0)
    mod = _modulation(cond, ada_w, ada_b).reshape(DEPTH, COND_ROWS, ADA_CHUNKS, D_MODEL)
    mod = jnp.pad(mod, ((0, 0), (0, 0), (0, MOD_ROWS - ADA_CHUNKS), (0, 0)))

    w_in_r = _reorder_w_in(w_in)
    wa, wb, wo = _bf(w_branch_a), _bf(w_branch_b), _bf(w_out)
    wc = w_branch_c.reshape(DEPTH, HEADS_C, HEAD_DIM_C, D_MODEL)
    wc = _bf(jnp.stack([wc[:, h] for h in HEAD_ORDER_C], axis=1).reshape(DEPTH, HEADS_C * HEAD_DIM_C, D_MODEL))
    wr = _bf(jnp.pad(w_router, ((0, 0), (0, 0), (0, LANE - N_EXPERTS))))
    gate_bias = _gate_lane_rows([None, None, delta_dt_bias[:, 0], delta_dt_bias[:, 1]],
                                [mlstm_i_bias[:, 0], mlstm_i_bias[:, 1], mlstm_f_bias[:, 0], mlstm_f_bias[:, 1]])
    gate_a_log = _gate_lane_rows([None, None, delta_a_log[:, 0], delta_a_log[:, 1]], [])
    sink = jnp.broadcast_to(attn_sink[:, :, None], (DEPTH, HEADS_C, LANE)).astype(F32)
    cache_k = cache_attn_k.reshape(batch_s, DEPTH, past, KV_HEADS_C * HEAD_DIM_C)
    cache_v = cache_attn_v.reshape(batch_s, DEPTH, past, KV_HEADS_C * HEAD_DIM_C)
    state_n = state_mlstm_n.reshape(batch_s, DEPTH, 2, HEADS_B, 1, DK_B)
    state_m = jnp.broadcast_to(state_mlstm_m[..., None, None], (batch_s, DEPTH, 2, HEADS_B, 1, LANE)).astype(F32)
    cos, sin = _rope_tables(seq_s)

    def mix_and_route(x2d, l, batch, seq, latent):
        mod_l = mod[l, 1:1 + batch] if latent else mod[l, 0:1]
        cap = EC_CAPACITY * seq // N_EXPERTS
        proj = _in_proj(x2d, mod_l, norm1_g, w_in_r, gate_bias, gate_a_log, l, seq, latent)
        ya, d_new = _delta_mixer(proj, conv_qkv_a, delta_norm_g, l, batch, seq,
                                 state_delta if latent else None)
        yb, b_new = _mlstm_mixer(proj, mlstm_norm_g, l, batch, seq,
                                 (state_mlstm_c, state_n, state_m) if latent else None)
        if latent:
            yc = _latent_attention(proj, cache_k, cache_v, cos, sin, sink, l, batch, seq)
        else:
            yc = _ctx_attention(proj, sink, l, batch, seq)
        x1, h2, aff, afft = _mix(x2d, ya, yb, yc, proj, mod_l, norm2_g, wa, wb, wc, wo, wr, l, seq, latent)
        xs, gate, rankc = _gather(aff, afft, h2, batch, seq, cap)
        return dict(x1=x1, rankc=rankc, mod=mod_l, cap=cap, proj=proj, delta=d_new, mlstm=b_new,
                    xs=xs.reshape(N_EXPERTS, batch * cap, D_MODEL), gate=gate.reshape(N_EXPERTS, batch * cap, 1))

    def add_experts(routed, ye, l, batch, seq, latent):
        return _scatter(routed["x1"], routed["rankc"], ye.reshape(N_EXPERTS, batch, routed["cap"], D_MODEL),
                        routed["mod"], final_norm_g, batch, seq, routed["cap"], latent, l == DEPTH - 1)

    xp = x_prompt.reshape(batch_p * seq_p, D_MODEL)
    xs = x_sample.reshape(batch_s * seq_s, D_MODEL)
    ks, vs, ds, cs, ns, ms = [], [], [], [], [], []
    for l in range(DEPTH):
        rp = mix_and_route(xp, l, batch_p, seq_p, False)
        rs = mix_and_route(xs, l, batch_s, seq_s, True)
        ye_p, ye_s = _expert_ffn([rp["xs"], rs["xs"]], [rp["gate"], rs["gate"]],
                                 w_expert_gate, w_expert_up, w_expert_down, l)
        xp = add_experts(rp, ye_p, l, batch_p, seq_p, False)
        xs = add_experts(rs, ye_s, l, batch_s, seq_s, True)
        proj = rp["proj"]
        ks.append(proj[:, T_KC * LANE:(T_KC + 1) * LANE].reshape(batch_p, seq_p, KV_HEADS_C, HEAD_DIM_C))
        vs.append(proj[:, T_VC * LANE:(T_VC + 1) * LANE].reshape(batch_p, seq_p, KV_HEADS_C, HEAD_DIM_C))
        ds.append(rp["delta"])
        c_new, n_new, m_new = rp["mlstm"]
        cs.append(c_new)
        ns.append(n_new)
        ms.append(m_new)

    stack = lambda parts: jnp.stack(parts, axis=1)
    return (xp.reshape(batch_p, seq_p, D_MODEL), xs.reshape(batch_s, seq_s, D_MODEL),
            stack(ks), stack(vs), stack(ds), stack(cs), stack(ns), stack(ms))
```

```python
import functools

import jax
import jax.numpy as jnp
from jax import lax
from jax.experimental import pallas as pl
from jax.experimental.pallas import tpu as pltpu

F32 = jnp.float32
BF16 = jnp.bfloat16

D_MODEL = 1024
DEPTH = 2
GRID_W = 64
EPS = 1e-6
HEADS_A = 4
DK_A = 128
DV_A = 128
CONV_K = 5
CHUNK = 64
HEADS_B = 4
DK_B = 64
DV_B = 128
HEADS_C = 8
KV_HEADS_C = 2
HEAD_DIM_C = 64
GROUP_C = HEADS_C // KV_HEADS_C
WINDOW = 128
Q_BLOCK = 128
ROPE_BASE = 10000.0
N_EXPERTS = 16
D_EXPERT = 512
EC_CAPACITY = 2
ADA_CHUNKS = 6

LANE = 128
MOD_ROWS = 8
COND_ROWS = 16

T_GA, T_GB, T_GC = 0, 8, 16
T_QA, T_KA, T_VA, T_ZA = 24, 28, 32, 36
T_QB, T_KB, T_VB, T_OB = 40, 42, 44, 48
T_QC, T_KC, T_VC = 52, 56, 57
T_GATES = 58
N_TILES = 60
N_PROJ = N_TILES * LANE
GATE_STRIDE = 8
GATE_B_OFF = HEADS_A * GATE_STRIDE

VMEM_LIMIT = 48 * 1024 * 1024


def _cparams(*sem):
    return pltpu.CompilerParams(dimension_semantics=sem, vmem_limit_bytes=VMEM_LIMIT)


def _bf(x):
    return x.astype(BF16)


def _mm(a, b):
    return jnp.dot(_bf(a), _bf(b), preferred_element_type=F32)


def _mm_nt(a, b):
    return lax.dot_general(_bf(a), _bf(b), (((1,), (1,)), ((), ())), preferred_element_type=F32)


def _mm_tn(a, b):
    return lax.dot_general(_bf(a), _bf(b), (((0,), (0,)), ((), ())), preferred_element_type=F32)


def _split2(x):
    hi = _bf(x)
    return hi, _bf(x - hi.astype(F32))


def _split3(x):
    hi = _bf(x)
    r = x - hi.astype(F32)
    mid = _bf(r)
    return hi, mid, _bf(r - mid.astype(F32))


def _mm_sel(sel, x):
    hi, mid, lo = _split3(x)
    d = functools.partial(jnp.dot, preferred_element_type=F32)
    return (d(sel, hi) + d(sel, mid)) + d(sel, lo)


def _mm_hi(a, b):
    ah, al = _split2(a)
    bh, bl = _split2(b)
    d = functools.partial(jnp.dot, preferred_element_type=F32)
    return d(ah, bh) + (d(ah, bl) + d(al, bh))


def _sigmoid(x):
    return 1.0 / (1.0 + jnp.exp(-x))


def _silu(x):
    return x * _sigmoid(x)


def _softplus(x):
    return jnp.maximum(x, 0.0) + jnp.log(1.0 + jnp.exp(-jnp.abs(x)))


def _rms(x, g):
    return x * lax.rsqrt(jnp.mean(x * x, axis=-1, keepdims=True) + EPS) * g


def _chunk_masks(backward):
    ri = lax.broadcasted_iota(jnp.int32, (CHUNK, CHUNK), 0)
    ci = lax.broadcasted_iota(jnp.int32, (CHUNK, CHUNK), 1)
    if backward:
        return ri <= ci, ri < ci, ri == ci
    return ri >= ci, ri > ci, ri == ci


def _mod_kernel(c_ref, w_ref, b_ref, o_ref):
    o_ref[0] = _mm(_silu(c_ref[...]), w_ref[0]) + b_ref[0]


def _modulation(cond, ada_w, ada_b):
    n_out = ADA_CHUNKS * D_MODEL
    tn = 512
    return pl.pallas_call(
        _mod_kernel,
        grid=(DEPTH, n_out // tn),
        in_specs=[
            pl.BlockSpec((COND_ROWS, D_MODEL), lambda l, j: (0, 0)),
            pl.BlockSpec((1, D_MODEL, tn), lambda l, j: (l, 0, j)),
            pl.BlockSpec((1, 1, tn), lambda l, j: (l, 0, j)),
        ],
        out_specs=pl.BlockSpec((1, COND_ROWS, tn), lambda l, j: (l, 0, j)),
        out_shape=jax.ShapeDtypeStruct((DEPTH, COND_ROWS, n_out), F32),
        compiler_params=_cparams("parallel", "parallel"),
    )(cond, ada_w, ada_b.reshape(DEPTH, 1, n_out))


def _activate_gates(x, bias, a_log):
    lane = lax.broadcasted_iota(jnp.int32, (1, LANE), 1)
    kind = lane % GATE_STRIDE
    is_delta = lane < GATE_B_OFF
    is_mlstm = (lane >= GATE_B_OFF) & (lane < GATE_B_OFF + HEADS_B * GATE_STRIDE)
    y = x + bias
    log_term = jnp.log(1.0 + jnp.exp(-jnp.abs(y)))
    out = jnp.where(is_delta & (kind < 2), _sigmoid(y), x)
    out = jnp.where(is_delta & (kind >= 2) & (kind < 4), -jnp.exp(a_log) * (jnp.maximum(y, 0.0) + log_term), out)
    out = jnp.where(is_mlstm & (kind < 2), y, out)
    return jnp.where(is_mlstm & (kind >= 2) & (kind < 4), -(jnp.maximum(-y, 0.0) + log_term), out)


def _in_proj_kernel(x_ref, mod_ref, g_ref, w_ref, gb_ref, ga_ref, o_ref, gt_ref, h_scr, *, gate_off):
    @pl.when(pl.program_id(1) == 0)
    def _():
        m = mod_ref[0]
        h_scr[...] = _bf(_rms(x_ref[...], g_ref[0]) * (1.0 + m[1:2]) + m[0:1])

    res = jnp.dot(h_scr[...], w_ref[0], preferred_element_type=F32)
    o_ref[...] = _bf(res)

    @pl.when(pl.program_id(1) == pl.num_programs(1) - 1)
    def _():
        gt_ref[...] = _activate_gates(res[:, gate_off:gate_off + LANE], gb_ref[0], ga_ref[0])


def _in_proj(x2d, mod, norm_g, w_in, gate_bias, gate_a_log, layer, seq, per_request):
    m_rows = x2d.shape[0]
    tm = min(1024, m_rows)
    tn = 768
    if per_request:
        assert seq % tm == 0
    gate_off = T_GATES * LANE - (N_PROJ // tn - 1) * tn
    assert 0 <= gate_off and gate_off + LANE <= tn
    lane_row = pl.BlockSpec((1, 1, LANE), lambda i, j: (layer, 0, 0))
    return pl.pallas_call(
        functools.partial(_in_proj_kernel, gate_off=gate_off),
        grid=(m_rows // tm, N_PROJ // tn),
        in_specs=[
            pl.BlockSpec((tm, D_MODEL), lambda i, j: (i, 0)),
            pl.BlockSpec((1, MOD_ROWS, D_MODEL), lambda i, j: ((i * tm) // seq if per_request else 0, 0, 0)),
            pl.BlockSpec((1, 1, D_MODEL), lambda i, j: (layer, 0, 0)),
            pl.BlockSpec((1, D_MODEL, tn), lambda i, j: (layer, 0, j)),
            lane_row, lane_row,
        ],
        out_specs=[pl.BlockSpec((tm, tn), lambda i, j: (i, j)), pl.BlockSpec((tm, LANE), lambda i, j: (i, 0))],
        out_shape=[jax.ShapeDtypeStruct((m_rows, N_PROJ), BF16), jax.ShapeDtypeStruct((m_rows, LANE), F32)],
        scratch_shapes=[pltpu.VMEM((tm, D_MODEL), BF16)],
        compiler_params=_cparams("parallel", "arbitrary"),
    )(x2d, mod, norm_g.reshape(DEPTH, 1, D_MODEL), w_in, gate_bias, gate_a_log)


PREP_PROBLEMS = 8
CONV_PAD = 8
DELTA_HEAD_TOKENS = 2048
INV_GROUP = 32


def _gate_dense(gt_parts, lane_index):
    row = lax.broadcasted_iota(jnp.int32, (LANE, LANE), 0)
    sel = jnp.where(row == lane_index, 1.0, 0.0).astype(BF16)
    hi, mid, lo = gt_parts
    d = functools.partial(jnp.dot, preferred_element_type=F32)
    return (d(hi, sel) + d(mid, sel)) + d(lo, sel)


def _delta_kernel(q_ref, k_ref, v_ref, z_ref, gt_ref, cq_ref, ck_ref, cv_ref, ng_ref, *rest,
                  seq, heads, has_init, emit_state):
    rest = list(rest)
    s0_ref = rest.pop(0) if has_init else None
    y_ref = rest.pop(0)
    st_ref = rest.pop(0) if emit_state else None
    qs, ks, vs, gates, s_scr, o_scr, u_scr, wq_scr, akd_scr, dk_scr, pw_scr, inv_scr, rhs_scr, pad_scr = rest
    head_group = pl.program_id(1)
    n_chunks = seq // CHUNK
    n_chain = 2 * heads
    head_lanes = lambda hd: slice(hd * DK_A, (hd + 1) * DK_A)

    pad_scr[0:CONV_PAD, :] = jnp.zeros((CONV_PAD, pad_scr.shape[1]), F32)
    pad_scr[CONV_PAD + seq:, :] = jnp.zeros((CONV_PAD, pad_scr.shape[1]), F32)

    def conv_silu(x_ref, w_ref):
        w = w_ref[0]
        pad_scr[CONV_PAD:CONV_PAD + seq, :] = x_ref[...].astype(F32)
        acc = None
        for j in range(CONV_K):
            first = CONV_PAD + j - CONV_K // 2
            term = pad_scr[first:first + seq, :] * w[j:j + 1, :]
            acc = term if acc is None else acc + term
        return _silu(acc)

    def l2n(x):
        parts = [x[:, head_lanes(hd)] for hd in range(heads)]
        return jnp.concatenate(
            [p * lax.rsqrt(jnp.sum(p * p, axis=-1, keepdims=True) + EPS) for p in parts], axis=1)

    qs[...] = l2n(conv_silu(q_ref, cq_ref)) * (DK_A ** -0.5)
    ks[...] = l2n(conv_silu(k_ref, ck_ref))
    vs[...] = conv_silu(v_ref, cv_ref)

    gt_parts = _split3(gt_ref[...])
    for hd in range(heads):
        base = (head_group * heads + hd) * GATE_STRIDE
        for kind in range(4):
            gates[hd * 4 + kind] = _gate_dense(gt_parts, base + kind)
        for d in range(2):
            s_scr[hd * 2 + d] = s0_ref[0, 0, d, hd] if has_init else jnp.zeros((DK_A, DV_A), F32)

    def chunk_rows(c):
        return pl.ds(pl.multiple_of(c * CHUNK, CHUNK), CHUNK)

    n_prob = n_chain * n_chunks
    group = min(INV_GROUP, n_prob)
    prep_chunks = max(1, PREP_PROBLEMS // n_chain)
    ri = lax.broadcasted_iota(jnp.int32, (CHUNK, 2 * CHUNK), 0)
    ci = lax.broadcasted_iota(jnp.int32, (CHUNK, 2 * CHUNK), 1) % CHUNK
    dot = functools.partial(jnp.dot, preferred_element_type=F32)

    def setup_body(i, carry):
        loaded = []
        for cc in range(prep_chunks):
            c = i * prep_chunks + cc
            rows = chunk_rows(c)
            for hd in range(heads):
                loaded.append((c, hd, qs[rows, head_lanes(hd)], ks[rows, head_lanes(hd)], vs[rows, head_lanes(hd)],
                               [(gates[hd * 4 + d, rows, :], gates[hd * 4 + 2 + d, rows, :]) for d in range(2)]))
        products = []
        for c, hd, q, k, v, gate_cols in loaded:
            per_dir = []
            for d in range(2):
                g = gate_cols[d][1]
                incl, strict, _ = _chunk_masks(d == 1)
                rhs = jnp.concatenate(
                    [jnp.where(strict, g[:, :CHUNK], 0.0), jnp.zeros((CHUNK, LANE - CHUNK), F32), g], axis=1)
                per_dir.append(_mm_sel(jnp.where(incl, 1.0, 0.0).astype(BF16), rhs))
            gram = _mm_nt(jnp.concatenate([k, q], axis=0), k)
            products.append((gram[:CHUNK], gram[CHUNK:], per_dir))
        results = []
        pairs = []
        for (c, hd, q, k, v, gate_cols), (kk, qk, per_dir) in zip(loaded, products):
            lows = []
            for d in range(2):
                bt, g = gate_cols[d]
                incl, strict, _ = _chunk_masks(d == 1)
                cs = per_dir[d]
                gc = cs[:, LANE:]
                g_last = gc[0:1, :] if d == 1 else gc[CHUNK - 1:CHUNK, :]
                egc = jnp.exp(gc)
                dec = jnp.where(incl, jnp.exp(cs[:, :CHUNK]), 0.0)
                lows.append(jnp.where(strict, bt[:, :CHUNK] * kk * dec, 0.0))
                rhs2 = jnp.concatenate([v * bt, k * (bt * egc)], axis=1)
                akd = _bf(jnp.concatenate([qk * dec, (k * jnp.exp(g_last - gc)).T], axis=0))
                results.append(((hd * n_chunks + c) * 2 + d, rhs2, akd, _bf(q * egc), jnp.exp(g_last)))
            pairs.append((hd * n_chunks + c, jnp.concatenate(lows, axis=1)))
        for p, rhs2, akd, qd, decay in results:
            rhs_scr[p] = rhs2
            akd_scr[p] = akd
            wq_scr[p, CHUNK:, :] = qd
            dk_scr[p] = decay
        for pp, low in pairs:
            pw_scr[pp] = low
            inv_scr[pp] = jnp.where(ri == ci, 1.0, 0.0) - jnp.where((ri // 2) == (ci // 2), low, 0.0)
        return carry

    lax.fori_loop(0, n_chunks // prep_chunks, setup_body, 0)

    lane_pair = lax.broadcasted_iota(jnp.int32, (CHUNK, 2 * CHUNK), 1)

    def block_diagonal(y):
        zero = jnp.zeros_like(y)
        return jnp.concatenate(
            [jnp.where(lane_pair < CHUNK, y, zero), jnp.where(lane_pair < CHUNK, zero, y)], axis=0)

    def pair_product(x, y):
        xh, xl = _split2(x)
        both = dot(jnp.concatenate([xh, xl], axis=0), block_diagonal(_bf(y)))
        return both[:CHUNK] + both[CHUNK:]

    n_pairs = n_prob // 2
    pair_group = min(INV_GROUP // 2, n_pairs)

    def doubling_pass(size):
        joins = ((ri // (2 * size)) == (ci // (2 * size))) & ((ri // size) != (ci // size))

        def body(i, carry):
            loaded = []
            for j in range(pair_group):
                pp = i * pair_group + j
                loaded.append((pp, pw_scr[pp], inv_scr[pp]))
            partial = [(pp, inv, pair_product(inv, jnp.where(joins, low, 0.0))) for pp, low, inv in loaded]
            results = [(pp, inv - pair_product(t, inv)) for pp, inv, t in partial]
            for pp, new_inv in results:
                inv_scr[pp] = new_inv
            return carry

        lax.fori_loop(0, n_pairs // pair_group, body, 0)

    size = 2
    while size < CHUNK:
        doubling_pass(size)
        size *= 2

    def solve_body(i, carry):
        loaded = []
        for j in range(pair_group):
            pp = i * pair_group + j
            inv = inv_scr[pp]
            for d in range(2):
                loaded.append((pp * 2 + d, inv[:, d * CHUNK:(d + 1) * CHUNK], rhs_scr[pp * 2 + d]))
        results = []
        for p, inv, rhs2 in loaded:
            ih, il = _split2(inv)
            rh, rl = _split2(rhs2)
            by_hi = dot(jnp.concatenate([ih, il], axis=0), rh)
            results.append((p, by_hi[:CHUNK] + (dot(ih, rl) + by_hi[CHUNK:])))
        for p, sol in results:
            u_scr[p] = sol[:, :DV_A]
            wq_scr[p, :CHUNK, :] = _bf(sol[:, DV_A:])
        return carry

    lax.fori_loop(0, n_pairs // pair_group, solve_body, 0)

    def scan_body(i, carry):
        loaded = []
        for chain in range(n_chain):
            c = n_chunks - 1 - i if chain % 2 == 1 else i
            p = ((chain // 2) * n_chunks + c) * 2 + chain % 2
            loaded.append((c, s_scr[chain], u_scr[p], wq_scr[p], akd_scr[p], dk_scr[p]))
        first = [dot(wq, _bf(state)) for c, state, u, wq, akd, decay in loaded]
        second = [dot(akd, _bf(u - ws[:CHUNK])) for (c, state, u, wq, akd, decay), ws in zip(loaded, first)]
        results = [(c, ws[CHUNK:] + av[:CHUNK], decay * state + av[CHUNK:])
                   for (c, state, u, wq, akd, decay), ws, av in zip(loaded, first, second)]
        for chain, (c, o, state) in enumerate(results):
            o_scr[chain, chunk_rows(c), :] = o
            s_scr[chain] = state
        return carry

    lax.fori_loop(0, n_chunks, scan_body, 0)

    z = z_ref[...].astype(F32)
    for hd in range(heads):
        y_ref[:, head_lanes(hd)] = _bf(_rms(o_scr[hd * 2] + o_scr[hd * 2 + 1], ng_ref[0])
                                       * _silu(z[:, head_lanes(hd)]))
        if emit_state:
            for d in range(2):
                st_ref[0, d, hd] = s_scr[hd * 2 + d]


def _delta_mixer(proj, scan_gates, conv_w, norm_g, layer, batch, seq, state0):
    has_init = state0 is not None
    emit_state = not has_init
    heads = next(h for h in (4, 2, 1) if HEADS_A % h == 0 and h * seq <= max(DELTA_HEAD_TOKENS, seq))
    n_groups = HEADS_A // heads
    width = heads * DK_A
    col = lambda tile: (lambda r, g: (r, tile // heads + g))
    cw = lambda part: (lambda r, g: (layer, 0, part * n_groups + g))
    in_specs = [
        pl.BlockSpec((seq, width), col(T_QA)),
        pl.BlockSpec((seq, width), col(T_KA)),
        pl.BlockSpec((seq, width), col(T_VA)),
        pl.BlockSpec((seq, width), col(T_ZA)),
        pl.BlockSpec((seq, LANE), lambda r, g: (r, 0)),
        pl.BlockSpec((1, CONV_K, width), cw(0)),
        pl.BlockSpec((1, CONV_K, width), cw(1)),
        pl.BlockSpec((1, CONV_K, width), cw(2)),
        pl.BlockSpec((1, 1, DV_A), lambda r, g: (layer, 0, 0)),
    ]
    args = [proj, proj, proj, proj, scan_gates, conv_w, conv_w, conv_w, norm_g.reshape(DEPTH, 1, DV_A)]
    if has_init:
        in_specs.append(pl.BlockSpec((1, 1, 2, heads, DK_A, DV_A), lambda r, g: (r, layer, 0, g, 0, 0)))
        args.append(state0)
    out_specs = [pl.BlockSpec((seq, width), lambda r, g: (r, g))]
    out_shape = [jax.ShapeDtypeStruct((batch * seq, HEADS_A * DV_A), BF16)]
    if emit_state:
        out_specs.append(pl.BlockSpec((1, 2, heads, DK_A, DV_A), lambda r, g: (r, 0, g, 0, 0)))
        out_shape.append(jax.ShapeDtypeStruct((batch, 2, HEADS_A, DK_A, DV_A), F32))
    n_chain = 2 * heads
    n_prob = n_chain * (seq // CHUNK)
    outs = pl.pallas_call(
        functools.partial(_delta_kernel, seq=seq, heads=heads, has_init=has_init, emit_state=emit_state),
        grid=(batch, n_groups),
        in_specs=in_specs,
        out_specs=out_specs,
        out_shape=out_shape,
        scratch_shapes=[
            pltpu.VMEM((seq, width), F32), pltpu.VMEM((seq, width), F32), pltpu.VMEM((seq, width), F32),
            pltpu.VMEM((2 * n_chain, seq, LANE), F32), pltpu.VMEM((n_chain, DK_A, DV_A), F32),
            pltpu.VMEM((n_chain, seq, DV_A), F32),
            pltpu.VMEM((n_prob, CHUNK, DV_A), F32), pltpu.VMEM((n_prob, 2 * CHUNK, DK_A), BF16),
            pltpu.VMEM((n_prob, CHUNK + DK_A, CHUNK), BF16), pltpu.VMEM((n_prob, 1, DV_A), F32),
            pltpu.VMEM((n_prob // 2, CHUNK, 2 * CHUNK), F32), pltpu.VMEM((n_prob // 2, CHUNK, 2 * CHUNK), F32),
            pltpu.VMEM((n_prob, CHUNK, DV_A + DK_A), F32),
            pltpu.VMEM((seq + 2 * CONV_PAD, width), F32),
        ],
        compiler_params=_cparams("parallel", "parallel"),
    )(*args)
    return (outs[0], outs[1]) if emit_state else (outs[0], None)


HEADS_PER_STEP_B = 4


def _mlstm_kernel(q_ref, k_ref, v_ref, og_ref, gt_ref, ng_ref, *rest, seq, has_init, emit_state):
    rest = list(rest)
    if has_init:
        c0_ref, n0_ref, m0_ref = rest[:3]
        rest = rest[3:]
    y_ref = rest.pop(0)
    if emit_state:
        co_ref, no_ref, mo_ref = rest[:3]
        rest = rest[3:]
    gates, c_scr, n_scr, m_scr, h_scr = rest
    pair = pl.program_id(1)
    n_chunks = seq // CHUNK
    gt_parts = _split3(gt_ref[...])

    for j in range(HEADS_PER_STEP_B):
        base = GATE_B_OFF + (pair * HEADS_PER_STEP_B + j) * GATE_STRIDE
        for kind in range(4):
            gates[j * 4 + kind] = _gate_dense(gt_parts, base + kind)
        for d in range(2):
            idx = j * 2 + d
            if has_init:
                c_scr[idx] = c0_ref[0, 0, d, j]
                n_scr[idx] = n0_ref[0, 0, d, j]
                m_scr[idx] = m0_ref[0, 0, d, j]
            else:
                c_scr[idx] = jnp.zeros((DK_B, DV_B), F32)
                n_scr[idx] = jnp.zeros((1, DK_B), F32)
                m_scr[idx] = jnp.zeros((1, LANE), F32)

    n_chain = 2 * HEADS_PER_STEP_B

    def scan_body(i, carry):
        loaded = []
        for chain in range(n_chain):
            j, d = chain // 2, chain % 2
            c = n_chunks - 1 - i if d == 1 else i
            rows = pl.ds(pl.multiple_of(c * CHUNK, CHUNK), CHUNK)
            loaded.append((chain, rows, q_ref[rows, j * DK_B:(j + 1) * DK_B].astype(F32),
                           k_ref[rows, j * DK_B:(j + 1) * DK_B].astype(F32) * (DK_B ** -0.5),
                           v_ref[rows, j * DV_B:(j + 1) * DV_B],
                           gates[j * 4 + d, rows, :], gates[j * 4 + 2 + d, rows, :],
                           c_scr[chain], n_scr[chain], m_scr[chain]))
        stage1 = []
        for chain, rows, q, k, v, ig, lf, cmat, nvec, m_prev in loaded:
            incl, strict, diag = _chunk_masks(chain % 2 == 1)
            rhs = jnp.concatenate(
                [jnp.where(strict, lf[:, :CHUNK], 0.0) + jnp.where(diag, ig[:, :CHUNK], 0.0),
                 jnp.zeros((CHUNK, LANE - CHUNK), F32), lf], axis=1)
            stage1.append((_mm_sel(jnp.where(incl, 1.0, 0.0).astype(BF16), rhs), _mm_nt(q, k), _mm(q, cmat)))
        stage2 = []
        for (chain, rows, q, k, v, ig, lf, cmat, nvec, m_prev), (cs, qk, qc) in zip(loaded, stage1):
            incl, _, _ = _chunk_masks(chain % 2 == 1)
            bc = cs[:, LANE:]
            b_last = bc[0:1, :] if chain % 2 == 1 else bc[CHUNK - 1:CHUNK, :]
            d_log = jnp.where(incl, cs[:, :CHUNK], -jnp.inf)
            d_max = jnp.max(d_log, axis=1, keepdims=True)
            tok = b_last - bc + ig
            m_new = jnp.maximum(b_last + m_prev, jnp.max(tok, axis=0, keepdims=True))
            w_prev = jnp.exp(b_last + m_prev - m_new)
            kw = k * jnp.exp(tok - m_new)[:, :DK_B]
            stage2.append((bc, d_log, d_max, m_new, w_prev, kw))
        stage3 = []
        for (chain, rows, q, k, v, ig, lf, cmat, nvec, m_prev), (cs, qk, qc), (bc, d_log, d_max, m_new, w_prev, kw) \
                in zip(loaded, stage1, stage2):
            m_t = jnp.maximum(bc + m_prev, d_max)
            w_inter = jnp.exp(bc + m_prev - m_t)
            pm = jnp.exp(d_log - m_t[:, :CHUNK]) * qk
            den = jnp.sum(w_inter[:, :DK_B] * (q * nvec) + pm, axis=1, keepdims=True)
            stage3.append((m_t, w_inter, den, _mm(pm, v), _mm_tn(kw, v)))
        results = []
        for (chain, rows, q, k, v, ig, lf, cmat, nvec, m_prev), (cs, qk, qc), (bc, d_log, d_max, m_new, w_prev, kw), \
                (m_t, w_inter, den, pv, inc) in zip(loaded, stage1, stage2, stage3):
            results.append((chain, rows, (w_inter * qc + pv) / jnp.maximum(jnp.abs(den), jnp.exp(-m_t)),
                            w_prev * cmat + inc,
                            w_prev[:, :DK_B] * nvec + jnp.sum(kw, axis=0, keepdims=True), m_new))
        for chain, rows, h, cmat, nvec, m_new in results:
            h_scr[chain, rows, :] = h
            c_scr[chain] = cmat
            n_scr[chain] = nvec
            m_scr[chain] = m_new
        return carry

    lax.fori_loop(0, n_chunks, scan_body, 0)

    og = og_ref[...].astype(F32)
    for j in range(HEADS_PER_STEP_B):
        h = h_scr[j * 2] + h_scr[j * 2 + 1]
        y_ref[:, j * DV_B:(j + 1) * DV_B] = _bf(_rms(h, ng_ref[0]) * _sigmoid(og[:, j * DV_B:(j + 1) * DV_B]))
        if emit_state:
            for d in range(2):
                co_ref[0, d, j] = c_scr[j * 2 + d]
                no_ref[0, d, j] = n_scr[j * 2 + d]
                mo_ref[0, d, j] = m_scr[j * 2 + d]


def _mlstm_mixer(proj, scan_gates, norm_g, layer, batch, seq, state0):
    has_init = state0 is not None
    emit_state = not has_init
    hp = HEADS_PER_STEP_B
    n_pairs = HEADS_B // hp
    in_specs = [
        pl.BlockSpec((seq, hp * DK_B), lambda r, p: (r, T_QB * LANE // (hp * DK_B) + p)),
        pl.BlockSpec((seq, hp * DK_B), lambda r, p: (r, T_KB * LANE // (hp * DK_B) + p)),
        pl.BlockSpec((seq, hp * DV_B), lambda r, p: (r, T_VB * LANE // (hp * DV_B) + p)),
        pl.BlockSpec((seq, hp * DV_B), lambda r, p: (r, T_OB * LANE // (hp * DV_B) + p)),
        pl.BlockSpec((seq, LANE), lambda r, p: (r, 0)),
        pl.BlockSpec((1, 1, DV_B), lambda r, p: (layer, 0, 0)),
    ]
    args = [proj, proj, proj, proj, scan_gates, norm_g.reshape(DEPTH, 1, DV_B)]
    if has_init:
        c0, n0, m0 = state0
        in_specs += [
            pl.BlockSpec((1, 1, 2, hp, DK_B, DV_B), lambda r, p: (r, layer, 0, p, 0, 0)),
            pl.BlockSpec((1, 1, 2, hp, 1, DK_B), lambda r, p: (r, layer, 0, p, 0, 0)),
            pl.BlockSpec((1, 1, 2, hp, 1, LANE), lambda r, p: (r, layer, 0, p, 0, 0)),
        ]
        args += [c0, n0, m0]
    out_specs = [pl.BlockSpec((seq, hp * DV_B), lambda r, p: (r, p))]
    out_shape = [jax.ShapeDtypeStruct((batch * seq, HEADS_B * DV_B), BF16)]
    if emit_state:
        out_specs += [
            pl.BlockSpec((1, 2, hp, DK_B, DV_B), lambda r, p: (r, 0, p, 0, 0)),
            pl.BlockSpec((1, 2, hp, 1, DK_B), lambda r, p: (r, 0, p, 0, 0)),
            pl.BlockSpec((1, 2, hp, 1, LANE), lambda r, p: (r, 0, p, 0, 0)),
        ]
        out_shape += [
            jax.ShapeDtypeStruct((batch, 2, HEADS_B, DK_B, DV_B), F32),
            jax.ShapeDtypeStruct((batch, 2, HEADS_B, 1, DK_B), F32),
            jax.ShapeDtypeStruct((batch, 2, HEADS_B, 1, LANE), F32),
        ]
    outs = pl.pallas_call(
        functools.partial(_mlstm_kernel, seq=seq, has_init=has_init, emit_state=emit_state),
        grid=(batch, n_pairs),
        in_specs=in_specs,
        out_specs=out_specs,
        out_shape=out_shape,
        scratch_shapes=[
            pltpu.VMEM((4 * hp, seq, LANE), F32), pltpu.VMEM((2 * hp, DK_B, DV_B), F32),
            pltpu.VMEM((2 * hp, 1, DK_B), F32), pltpu.VMEM((2 * hp, 1, LANE), F32),
            pltpu.VMEM((2 * hp, seq, DV_B), F32),
        ],
        compiler_params=_cparams("parallel", "parallel"),
    )(*args)
    if emit_state:
        return outs[0], (outs[1], outs[2][:, :, :, 0, :], outs[3][:, :, :, 0, 0])
    return outs[0], None


Q_SLABS_C = HEADS_C * HEAD_DIM_C // LANE
HEAD_ORDER_C = [h for s in range(Q_SLABS_C) for h in (s, s + GROUP_C)]


def _attend(q_slabs, segments, sink):
    lane = lax.broadcasted_iota(jnp.int32, (1, LANE), 1)
    scale = HEAD_DIM_C ** -0.5
    nt = (((1,), (1,)), ((), ()))
    operands = []
    for kv in range(KV_HEADS_C):
        mine = (lane >= kv * HEAD_DIM_C) & (lane < (kv + 1) * HEAD_DIM_C)
        operands.append([(_bf(jnp.where(mine, k, 0.0)), _bf(jnp.where(mine, v, 0.0)), valid)
                         for k, v, valid in segments])
    problems = [(s, kv) for s in range(len(q_slabs)) for kv in range(KV_HEADS_C)]
    q_bf = [_bf(q) for q in q_slabs]
    scores = [[lax.dot_general(q_bf[s], kb, nt, preferred_element_type=F32) * scale for kb, _, _ in operands[kv]]
              for s, kv in problems]
    weights = []
    for (s, kv), per_seg in zip(problems, scores):
        per_seg = [x if valid is None else jnp.where(valid, x, -jnp.inf)
                   for x, (_, _, valid) in zip(per_seg, operands[kv])]
        head = kv * GROUP_C + s
        tiles = [x[:, t * LANE:(t + 1) * LANE] for x in per_seg for t in range(x.shape[1] // LANE)]
        m = jnp.maximum(jnp.max(functools.reduce(jnp.maximum, tiles), axis=1, keepdims=True),
                        sink[head:head + 1, 0:1])
        es = [jnp.exp(x - m) for x in per_seg]
        e_tiles = [e[:, t * LANE:(t + 1) * LANE] for e in es for t in range(e.shape[1] // LANE)]
        den = (jnp.sum(functools.reduce(jnp.add, e_tiles), axis=1, keepdims=True)
               + jnp.exp(sink[head:head + 1, 0:1] - m))
        weights.append((es, den))
    outs = []
    for (s, kv), (es, den) in zip(problems, weights):
        acc = None
        for e, (_, vb, _) in zip(es, operands[kv]):
            part = jnp.dot(_bf(e), vb, preferred_element_type=F32)
            acc = part if acc is None else acc + part
        outs.append(acc / den)
    return [sum(outs[s * KV_HEADS_C + 1:(s + 1) * KV_HEADS_C], outs[s * KV_HEADS_C]) for s in range(len(q_slabs))]


def _ctx_attn_kernel(q_ref, k_ref, v_ref, sink_ref, o_ref, *, seq):
    q_slabs = [q_ref[:, s * LANE:(s + 1) * LANE] for s in range(Q_SLABS_C)]
    outs = _attend(q_slabs, [(k_ref[...], v_ref[...], None)], sink_ref[0])
    for s, o in enumerate(outs):
        o_ref[:, s * LANE:(s + 1) * LANE] = _bf(o)


def _ctx_attention(proj, sink, layer, batch, seq):
    width = HEADS_C * HEAD_DIM_C
    return pl.pallas_call(
        functools.partial(_ctx_attn_kernel, seq=seq),
        grid=(batch,),
        in_specs=[
            pl.BlockSpec((seq, width), lambda r: (r, T_QC * LANE // width)),
            pl.BlockSpec((seq, LANE), lambda r: (r, T_KC)),
            pl.BlockSpec((seq, LANE), lambda r: (r, T_VC)),
            pl.BlockSpec((1, HEADS_C, LANE), lambda r: (layer, 0, 0)),
        ],
        out_specs=pl.BlockSpec((seq, width), lambda r: (r, 0)),
        out_shape=jax.ShapeDtypeStruct((batch * seq, width), BF16),
        compiler_params=_cparams("parallel"),
    )(proj, proj, proj, sink)


def _rope(x, cos, sin):
    quarter = HEAD_DIM_C // 4
    lane = lax.broadcasted_iota(jnp.int32, (1, LANE), 1)
    first = (lane % (2 * quarter)) < quarter
    partner = jnp.where(first, -pltpu.roll(x, LANE - quarter, 1), pltpu.roll(x, quarter, 1))
    return x * cos + partner * sin


def _latent_attn_kernel(q_ref, k_ref, v_ref, ck_ref, cv_ref, cq_ref, sq_ref, cos_ref, sin_ref, sink_ref, o_ref, *, seq):
    blk = pl.program_id(1)
    span = Q_BLOCK + 2 * WINDOW
    start = blk * Q_BLOCK
    k_start = pl.multiple_of(jnp.clip(start - WINDOW, 0, seq - span), Q_BLOCK)
    win = pl.ds(k_start, span)
    cq = cq_ref[...]
    sq = sq_ref[...]
    q_slabs = [_rope(q_ref[:, s * LANE:(s + 1) * LANE].astype(F32), cq, sq) for s in range(Q_SLABS_C)]
    k = _rope(k_ref[win, :].astype(F32), cos_ref[win, :], sin_ref[win, :])
    q_pos = start + lax.broadcasted_iota(jnp.int32, (Q_BLOCK, 1), 0)
    k_pos = k_start + lax.broadcasted_iota(jnp.int32, (1, span), 1)
    valid = jnp.abs(q_pos - k_pos) <= WINDOW
    outs = _attend(q_slabs, [(k, v_ref[win, :], valid), (ck_ref[0, 0], cv_ref[0, 0], None)], sink_ref[0])
    for s, o in enumerate(outs):
        o_ref[:, s * LANE:(s + 1) * LANE] = _bf(o)


def _latent_attention(proj, cache_k, cache_v, cos, sin, sink, layer, batch, seq):
    width = HEADS_C * HEAD_DIM_C
    n_blk = seq // Q_BLOCK
    past = cache_k.shape[2]
    return pl.pallas_call(
        functools.partial(_latent_attn_kernel, seq=seq),
        grid=(batch, n_blk),
        in_specs=[
            pl.BlockSpec((Q_BLOCK, width), lambda r, i: (r * n_blk + i, T_QC * LANE // width)),
            pl.BlockSpec((seq, LANE), lambda r, i: (r, T_KC)),
            pl.BlockSpec((seq, LANE), lambda r, i: (r, T_VC)),
            pl.BlockSpec((1, 1, past, LANE), lambda r, i: (r, layer, 0, 0)),
            pl.BlockSpec((1, 1, past, LANE), lambda r, i: (r, layer, 0, 0)),
            pl.BlockSpec((Q_BLOCK, LANE), lambda r, i: (i, 0)),
            pl.BlockSpec((Q_BLOCK, LANE), lambda r, i: (i, 0)),
            pl.BlockSpec((seq, LANE), lambda r, i: (0, 0)),
            pl.BlockSpec((seq, LANE), lambda r, i: (0, 0)),
            pl.BlockSpec((1, HEADS_C, LANE), lambda r, i: (layer, 0, 0)),
        ],
        out_specs=pl.BlockSpec((Q_BLOCK, width), lambda r, i: (r * n_blk + i, 0)),
        out_shape=jax.ShapeDtypeStruct((batch * seq, width), BF16),
        compiler_params=_cparams("parallel", "parallel"),
    )(proj, proj, proj, cache_k, cache_v, cos, sin, cos, sin, sink)


def _rope_tables(seq):
    quarter = HEAD_DIM_C // 4
    pos = jnp.arange(seq)
    row = (pos // GRID_W).astype(F32)
    col = (pos % GRID_W).astype(F32)
    inv = jnp.power(ROPE_BASE, -jnp.arange(quarter, dtype=F32) / quarter)
    ang_row = row[:, None] * inv[None, :]
    ang_col = col[:, None] * inv[None, :]
    ang = jnp.concatenate([ang_row, ang_row, ang_col, ang_col], axis=1)
    ang = jnp.concatenate([ang] * (LANE // HEAD_DIM_C), axis=1)
    return jnp.cos(ang), jnp.sin(ang)


def _mix_kernel(x_ref, ya_ref, yb_ref, yc_ref, ga_ref, gb_ref, gc_ref, mod_ref, n2_ref,
                wa_ref, wb_ref, wc_ref, wo_ref, wr_ref, xo_ref, h_ref, aff_ref, afft_ref):
    m = mod_ref[0]
    mixed = (_sigmoid(ga_ref[...].astype(F32)) * _mm(ya_ref[...], wa_ref[0])
             + _sigmoid(gb_ref[...].astype(F32)) * _mm(yb_ref[...], wb_ref[0])
             + _sigmoid(gc_ref[...].astype(F32)) * _mm(yc_ref[...], wc_ref[0]))
    x = x_ref[...] + m[2:3] * _mm(mixed, wo_ref[0])
    xo_ref[...] = x
    h = _bf(_rms(x, n2_ref[0]) * (1.0 + m[4:5]) + m[3:4])
    h_ref[...] = h
    logits = jnp.dot(h, wr_ref[0], preferred_element_type=F32)
    lane = lax.broadcasted_iota(jnp.int32, (1, LANE), 1)
    logits = jnp.where(lane < N_EXPERTS, logits, -jnp.inf)
    e = jnp.exp(logits - jnp.max(logits, axis=1, keepdims=True))
    aff = e / jnp.sum(e, axis=1, keepdims=True)
    aff_ref[...] = aff
    afft_ref[...] = aff.T[:N_EXPERTS, :]


def _mix(x2d, ya, yb, yc, proj, mod, norm2_g, wa, wb, wc, wo, wr, layer, seq, per_request):
    m_rows = x2d.shape[0]
    tm = min(512, seq)
    gate_blk = lambda tile: (lambda i: (i, tile * LANE // D_MODEL))
    wspec = lambda w: pl.BlockSpec((1,) + w.shape[1:], lambda i: (layer, 0, 0))
    branch = pl.BlockSpec((tm, ya.shape[1]), lambda i: (i, 0))
    return pl.pallas_call(
        _mix_kernel,
        grid=(m_rows // tm,),
        in_specs=[
            pl.BlockSpec((tm, D_MODEL), lambda i: (i, 0)),
            branch, branch, branch,
            pl.BlockSpec((tm, D_MODEL), gate_blk(T_GA)),
            pl.BlockSpec((tm, D_MODEL), gate_blk(T_GB)),
            pl.BlockSpec((tm, D_MODEL), gate_blk(T_GC)),
            pl.BlockSpec((1, MOD_ROWS, D_MODEL), lambda i: ((i * tm) // seq if per_request else 0, 0, 0)),
            pl.BlockSpec((1, 1, D_MODEL), lambda i: (layer, 0, 0)),
            wspec(wa), wspec(wb), wspec(wc), wspec(wo), wspec(wr),
        ],
        out_specs=[
            pl.BlockSpec((tm, D_MODEL), lambda i: (i, 0)),
            pl.BlockSpec((tm, D_MODEL), lambda i: (i, 0)),
            pl.BlockSpec((tm, LANE), lambda i: (i, 0)),
            pl.BlockSpec((N_EXPERTS, tm), lambda i: (0, i)),
        ],
        out_shape=[
            jax.ShapeDtypeStruct((m_rows, D_MODEL), F32),
            jax.ShapeDtypeStruct((m_rows, D_MODEL), BF16),
            jax.ShapeDtypeStruct((m_rows, LANE), F32),
            jax.ShapeDtypeStruct((N_EXPERTS, m_rows), F32),
        ],
        compiler_params=_cparams("parallel"),
    )(x2d, ya, yb, yc, proj, proj, proj, mod, norm2_g.reshape(DEPTH, 1, D_MODEL), wa, wb, wc, wo, wr)


SCATTER_K = 512


def _gather_kernel(aff_ref, afft_ref, h_ref, xs_ref, gate_ref, rankc_ref, rank_scr, onehot_scr, col_scr, *,
                   seq, cap):
    n_blk = seq // LANE
    ri = lax.broadcasted_iota(jnp.int32, (LANE, LANE), 0)
    ci = lax.broadcasted_iota(jnp.int32, (LANE, LANE), 1)
    lane = lax.broadcasted_iota(jnp.int32, (1, LANE), 1)
    slot = lax.broadcasted_iota(jnp.int32, (cap, 1), 0).astype(F32)

    def expert_body(e, carry):
        a_row = afft_ref[pl.ds(e, 1), :]
        a_col = jnp.sum(jnp.where(lane == e, aff_ref[...], 0.0), axis=1, keepdims=True)
        col_scr[...] = jnp.broadcast_to(a_col, (seq, LANE))
        rank_parts = []
        for tb in range(n_blk):
            a_t = a_row[:, tb * LANE:(tb + 1) * LANE]
            count = jnp.zeros((LANE, LANE), F32)
            for sb in range(n_blk):
                a_s = col_scr[sb * LANE:(sb + 1) * LANE, :]
                if sb < tb:
                    beats = a_s >= a_t
                elif sb > tb:
                    beats = a_s > a_t
                else:
                    beats = (a_s > a_t) | ((a_s == a_t) & (ri < ci))
                count = count + jnp.where(beats, 1.0, 0.0)
            rank_parts.append(jnp.sum(count, axis=0, keepdims=True))
        rank = jnp.concatenate(rank_parts, axis=1)
        rank_scr[pl.ds(e, 1), :] = rank
        chosen = rank == slot
        onehot_scr[pl.ds(pl.multiple_of(e * cap, cap), cap), :] = jnp.where(chosen, 1.0, 0.0).astype(BF16)
        gate_ref[e, 0] = jnp.sum(jnp.where(chosen, a_row, 0.0), axis=1, keepdims=True)
        return carry

    lax.fori_loop(0, N_EXPERTS, expert_body, 0)
    ranks = jnp.concatenate([rank_scr[...], jnp.zeros((LANE - N_EXPERTS, seq), F32)], axis=0)
    rankc_ref[...] = ranks.T
    per_group = SCATTER_K // cap
    h = h_ref[...]
    for i in range(N_EXPERTS // per_group):
        rows = jnp.dot(onehot_scr[i * SCATTER_K:(i + 1) * SCATTER_K, :], h, preferred_element_type=F32)
        xs_ref[i * per_group:(i + 1) * per_group, 0] = _bf(rows).reshape(per_group, cap, D_MODEL)


def _gather(aff, afft, h2, batch, seq, cap):
    return pl.pallas_call(
        functools.partial(_gather_kernel, seq=seq, cap=cap),
        grid=(batch,),
        in_specs=[
            pl.BlockSpec((seq, LANE), lambda r: (r, 0)),
            pl.BlockSpec((N_EXPERTS, seq), lambda r: (0, r)),
            pl.BlockSpec((seq, D_MODEL), lambda r: (r, 0)),
        ],
        out_specs=[
            pl.BlockSpec((N_EXPERTS, 1, cap, D_MODEL), lambda r: (0, r, 0, 0)),
            pl.BlockSpec((N_EXPERTS, 1, cap, 1), lambda r: (0, r, 0, 0)),
            pl.BlockSpec((seq, LANE), lambda r: (r, 0)),
        ],
        out_shape=[
            jax.ShapeDtypeStruct((N_EXPERTS, batch, cap, D_MODEL), BF16),
            jax.ShapeDtypeStruct((N_EXPERTS, batch, cap, 1), F32),
            jax.ShapeDtypeStruct((batch * seq, LANE), F32),
        ],
        scratch_shapes=[pltpu.VMEM((N_EXPERTS, seq), F32), pltpu.VMEM((N_EXPERTS * cap, seq), BF16),
                        pltpu.VMEM((seq, LANE), F32)],
        compiler_params=_cparams("parallel"),
    )(aff, afft, h2)


def _ffn_kernel(*refs, n_groups):
    x_refs, gate_refs = refs[:n_groups], refs[n_groups:2 * n_groups]
    wg_ref, wu_ref, wd_ref = refs[2 * n_groups:2 * n_groups + 3]
    y_refs = refs[2 * n_groups + 3:]
    wg, wu, wd = _bf(wg_ref[0, 0]), _bf(wu_ref[0, 0]), _bf(wd_ref[0, 0])
    d = functools.partial(jnp.dot, preferred_element_type=F32)
    for x_ref, gate_ref, y_ref in zip(x_refs, gate_refs, y_refs):
        x = x_ref[0]
        hid = _silu(d(x, wg)) * d(x, wu)
        y_ref[0] = _bf(d(_bf(hid), wd) * gate_ref[0])


def _expert_ffn(xs_groups, gate_groups, w_gate, w_up, w_down, layer):
    n_groups = len(xs_groups)
    row_specs = [pl.BlockSpec((1, xs.shape[1], D_MODEL), lambda e: (e, 0, 0)) for xs in xs_groups]
    gate_specs = [pl.BlockSpec((1, g.shape[1], 1), lambda e: (e, 0, 0)) for g in gate_groups]
    return pl.pallas_call(
        functools.partial(_ffn_kernel, n_groups=n_groups),
        grid=(N_EXPERTS,),
        in_specs=row_specs + gate_specs + [
            pl.BlockSpec((1, 1, D_MODEL, D_EXPERT), lambda e: (layer, e, 0, 0)),
            pl.BlockSpec((1, 1, D_MODEL, D_EXPERT), lambda e: (layer, e, 0, 0)),
            pl.BlockSpec((1, 1, D_EXPERT, D_MODEL), lambda e: (layer, e, 0, 0)),
        ],
        out_specs=row_specs,
        out_shape=[jax.ShapeDtypeStruct(xs.shape, BF16) for xs in xs_groups],
        compiler_params=_cparams("parallel"),
    )(*xs_groups, *gate_groups, w_gate, w_up, w_down)


def _scatter_kernel(x_ref, rankc_ref, ye_ref, mod_ref, fg_ref, o_ref, acc, *, cap, final):
    g = pl.program_id(1)
    n_groups = pl.num_programs(1)
    per_lane_tile = LANE // cap
    rank_parts = _split3(rankc_ref[...])
    row = lax.broadcasted_iota(jnp.int32, (LANE, LANE), 0)
    lane = lax.broadcasted_iota(jnp.int32, (LANE, LANE), 1)
    lane_slot = (lax.broadcasted_iota(jnp.int32, (1, LANE), 1) % cap).astype(F32)
    d = functools.partial(jnp.dot, preferred_element_type=F32)
    tiles = []
    for b in range(SCATTER_K // LANE):
        first = (g * (SCATTER_K // LANE) + b) * per_lane_tile
        sel = jnp.where(row == first + lane // cap, 1.0, 0.0).astype(BF16)
        hi, mid, lo = rank_parts
        token_rank = (d(hi, sel) + d(mid, sel)) + d(lo, sel)
        tiles.append(jnp.where(token_rank == lane_slot, 1.0, 0.0).astype(BF16))
    onehot = jnp.concatenate(tiles, axis=1)
    spread = d(onehot, ye_ref[:, 0].reshape(SCATTER_K, D_MODEL))

    @pl.when(g == 0)
    def _():
        acc[...] = spread

    @pl.when(g > 0)
    def _():
        acc[...] += spread

    @pl.when(g == n_groups - 1)
    def _():
        x = x_ref[...] + mod_ref[0][5:6] * acc[...]
        o_ref[...] = _rms(x, fg_ref[...]) if final else x


def _scatter(x2d, rankc, ye, mod, final_g, batch, seq, cap, per_request, final):
    experts_per_group = SCATTER_K // cap
    slots = pl.BlockSpec((experts_per_group, 1, cap, D_MODEL), lambda r, g: (g, r, 0, 0))
    return pl.pallas_call(
        functools.partial(_scatter_kernel, cap=cap, final=final),
        grid=(batch, N_EXPERTS // experts_per_group),
        in_specs=[
            pl.BlockSpec((seq, D_MODEL), lambda r, g: (r, 0)),
            pl.BlockSpec((seq, LANE), lambda r, g: (r, 0)),
            slots,
            pl.BlockSpec((1, MOD_ROWS, D_MODEL), lambda r, g: (r if per_request else 0, 0, 0)),
            pl.BlockSpec((1, D_MODEL), lambda r, g: (0, 0)),
        ],
        out_specs=pl.BlockSpec((seq, D_MODEL), lambda r, g: (r, 0)),
        out_shape=jax.ShapeDtypeStruct((batch * seq, D_MODEL), F32),
        scratch_shapes=[pltpu.VMEM((seq, D_MODEL), F32)],
        compiler_params=_cparams("parallel", "arbitrary"),
    )(x2d, rankc, ye, mod, final_g.reshape(1, D_MODEL))


def _reorder_w_in(w_in):
    a_main = w_in[:, :, 0:2048]
    a_gate = w_in[:, :, 2048:2064]
    b_main = w_in[:, :, 2064:3600]
    b_gate = w_in[:, :, 3600:3616]
    c_main = w_in[:, :, 3616:4384]
    merge = w_in[:, :, 4384:7456]

    def per_head(g, heads):
        g = g.reshape(DEPTH, D_MODEL, 4, heads).transpose(0, 1, 3, 2)
        g = jnp.pad(g, ((0, 0), (0, 0), (0, 0), (0, GATE_STRIDE - 4)))
        return g.reshape(DEPTH, D_MODEL, heads * GATE_STRIDE)

    width_q = HEADS_C * HEAD_DIM_C
    qc = c_main[:, :, :width_q].reshape(DEPTH, D_MODEL, HEADS_C, HEAD_DIM_C)
    qc = jnp.stack([qc[:, :, h] for h in HEAD_ORDER_C], axis=2).reshape(DEPTH, D_MODEL, width_q)
    c_main = jnp.concatenate([qc, c_main[:, :, width_q:]], axis=-1)
    gates = jnp.concatenate([per_head(a_gate, HEADS_A), per_head(b_gate, HEADS_B)], axis=-1)
    pad = jnp.zeros((DEPTH, D_MODEL, N_PROJ - T_GATES * LANE - gates.shape[-1]), w_in.dtype)
    return _bf(jnp.concatenate([merge, a_main, b_main, c_main, gates, pad], axis=-1))


def _gate_lane_rows(delta_kinds, mlstm_kinds):
    def block(kinds, heads):
        zero = jnp.zeros((DEPTH, heads), F32)
        cols = [zero if k is None else k.astype(F32) for k in kinds] + [zero] * (GATE_STRIDE - len(kinds))
        return jnp.stack(cols, axis=-1).reshape(DEPTH, heads * GATE_STRIDE)

    used = jnp.concatenate([block(delta_kinds, HEADS_A), block(mlstm_kinds, HEADS_B)], axis=-1)
    return jnp.pad(used, ((0, 0), (0, LANE - used.shape[-1])))[:, None, :]


def kernel(x_prompt, x_sample, cache_attn_k, cache_attn_v, state_delta, state_mlstm_c, state_mlstm_n, state_mlstm_m, c, c_ctx, ada_w, ada_b, norm1_g, norm2_g, w_in, conv_qkv_a, delta_a_log, delta_dt_bias, delta_norm_g, mlstm_i_bias, mlstm_f_bias, mlstm_norm_g, attn_sink, w_branch_a, w_branch_b, w_branch_c, w_out, w_router, w_expert_gate, w_expert_up, w_expert_down, final_norm_g):
    batch_p, seq_p, _ = x_prompt.shape
    batch_s, seq_s, _ = x_sample.shape
    past = cache_attn_k.shape[2]

    cond = jnp.concatenate([c_ctx[None, :], c, jnp.zeros((COND_ROWS - 1 - batch_s, D_MODEL), F32)], axis=0)
    mod = _modulation(cond, ada_w, ada_b).reshape(DEPTH, COND_ROWS, ADA_CHUNKS, D_MODEL)
    mod = jnp.pad(mod, ((0, 0), (0, 0), (0, MOD_ROWS - ADA_CHUNKS), (0, 0)))

    w_in_r = _reorder_w_in(w_in)
    wa, wb, wo = _bf(w_branch_a), _bf(w_branch_b), _bf(w_out)
    wc = w_branch_c.reshape(DEPTH, HEADS_C, HEAD_DIM_C, D_MODEL)
    wc = _bf(jnp.stack([wc[:, h] for h in HEAD_ORDER_C], axis=1).reshape(DEPTH, HEADS_C * HEAD_DIM_C, D_MODEL))
    wr = _bf(jnp.pad(w_router, ((0, 0), (0, 0), (0, LANE - N_EXPERTS))))
    gate_bias = _gate_lane_rows([None, None, delta_dt_bias[:, 0], delta_dt_bias[:, 1]],
                                [mlstm_i_bias[:, 0], mlstm_i_bias[:, 1], mlstm_f_bias[:, 0], mlstm_f_bias[:, 1]])
    gate_a_log = _gate_lane_rows([None, None, delta_a_log[:, 0], delta_a_log[:, 1]], [])
    sink = jnp.broadcast_to(attn_sink[:, :, None], (DEPTH, HEADS_C, LANE)).astype(F32)
    cache_k = cache_attn_k.reshape(batch_s, DEPTH, past, KV_HEADS_C * HEAD_DIM_C)
    cache_v = cache_attn_v.reshape(batch_s, DEPTH, past, KV_HEADS_C * HEAD_DIM_C)
    state_n = state_mlstm_n.reshape(batch_s, DEPTH, 2, HEADS_B, 1, DK_B)
    state_m = jnp.broadcast_to(state_mlstm_m[..., None, None], (batch_s, DEPTH, 2, HEADS_B, 1, LANE)).astype(F32)
    cos, sin = _rope_tables(seq_s)

    def mix_and_route(x2d, l, batch, seq, latent):
        mod_l = mod[l, 1:1 + batch] if latent else mod[l, 0:1]
        cap = EC_CAPACITY * seq // N_EXPERTS
        proj, scan_gates = _in_proj(x2d, mod_l, norm1_g, w_in_r, gate_bias, gate_a_log, l, seq, latent)
        ya, d_new = _delta_mixer(proj, scan_gates, conv_qkv_a, delta_norm_g, l, batch, seq,
                                 state_delta if latent else None)
        yb, b_new = _mlstm_mixer(proj, scan_gates, mlstm_norm_g, l, batch, seq,
                                 (state_mlstm_c, state_n, state_m) if latent else None)
        if latent:
            yc = _latent_attention(proj, cache_k, cache_v, cos, sin, sink, l, batch, seq)
        else:
            yc = _ctx_attention(proj, sink, l, batch, seq)
        x1, h2, aff, afft = _mix(x2d, ya, yb, yc, proj, mod_l, norm2_g, wa, wb, wc, wo, wr, l, seq, latent)
        xs, gate, rankc = _gather(aff, afft, h2, batch, seq, cap)
        return dict(x1=x1, rankc=rankc, mod=mod_l, cap=cap, proj=proj, delta=d_new, mlstm=b_new,
                    xs=xs.reshape(N_EXPERTS, batch * cap, D_MODEL), gate=gate.reshape(N_EXPERTS, batch * cap, 1))

    def add_experts(routed, ye, l, batch, seq, latent):
        return _scatter(routed["x1"], routed["rankc"], ye.reshape(N_EXPERTS, batch, routed["cap"], D_MODEL),
                        routed["mod"], final_norm_g, batch, seq, routed["cap"], latent, l == DEPTH - 1)

    xp = x_prompt.reshape(batch_p * seq_p, D_MODEL)
    xs = x_sample.reshape(batch_s * seq_s, D_MODEL)
    ks, vs, ds, cs, ns, ms = [], [], [], [], [], []
    for l in range(DEPTH):
        rp = mix_and_route(xp, l, batch_p, seq_p, False)
        rs = mix_and_route(xs, l, batch_s, seq_s, True)
        ye_p, ye_s = _expert_ffn([rp["xs"], rs["xs"]], [rp["gate"], rs["gate"]],
                                 w_expert_gate, w_expert_up, w_expert_down, l)
        xp = add_experts(rp, ye_p, l, batch_p, seq_p, False)
        xs = add_experts(rs, ye_s, l, batch_s, seq_s, True)
        kv = rp["proj"][:, T_KC * LANE:(T_VC + 1) * LANE].astype(F32)
        ks.append(kv[:, :LANE].reshape(batch_p, seq_p, KV_HEADS_C, HEAD_DIM_C))
        vs.append(kv[:, LANE:].reshape(batch_p, seq_p, KV_HEADS_C, HEAD_DIM_C))
        ds.append(rp["delta"])
        c_new, n_new, m_new = rp["mlstm"]
        cs.append(c_new)
        ns.append(n_new)
        ms.append(m_new)

    stack = lambda parts: jnp.stack(parts, axis=1)
    return (xp.reshape(batch_p, seq_p, D_MODEL), xs.reshape(batch_s, seq_s, D_MODEL),
            stack(ks), stack(vs), stack(ds), stack(cs), stack(ns), stack(ms))
```

```python
import functools

import jax
import jax.numpy as jnp
from jax import lax
from jax.experimental import pallas as pl
from jax.experimental.pallas import tpu as pltpu

F32 = jnp.float32
BF16 = jnp.bfloat16

D_MODEL = 1024
DEPTH = 2
GRID_W = 64
EPS = 1e-6
HEADS_A = 4
DK_A = 128
DV_A = 128
CONV_K = 5
CHUNK = 64
HEADS_B = 4
DK_B = 64
DV_B = 128
HEADS_C = 8
KV_HEADS_C = 2
HEAD_DIM_C = 64
GROUP_C = HEADS_C // KV_HEADS_C
WINDOW = 128
Q_BLOCK = 128
ROPE_BASE = 10000.0
N_EXPERTS = 16
D_EXPERT = 512
EC_CAPACITY = 2
ADA_CHUNKS = 6

LANE = 128
MOD_ROWS = 8
COND_ROWS = 16

T_GA, T_GB, T_GC = 0, 8, 16
T_QA, T_KA, T_VA, T_ZA = 24, 28, 32, 36
T_QB, T_KB, T_VB, T_OB = 40, 42, 44, 48
T_QC, T_KC, T_VC = 52, 56, 57
T_GATES = 58
N_TILES = 60
N_PROJ = N_TILES * LANE
GATE_STRIDE = 8
GATE_B_OFF = HEADS_A * GATE_STRIDE

VMEM_LIMIT = 48 * 1024 * 1024


def _cparams(*sem):
    return pltpu.CompilerParams(dimension_semantics=sem, vmem_limit_bytes=VMEM_LIMIT)


def _bf(x):
    return x.astype(BF16)


def _mm(a, b):
    return jnp.dot(_bf(a), _bf(b), preferred_element_type=F32)


def _mm_nt(a, b):
    return lax.dot_general(_bf(a), _bf(b), (((1,), (1,)), ((), ())), preferred_element_type=F32)


def _mm_tn(a, b):
    return lax.dot_general(_bf(a), _bf(b), (((0,), (0,)), ((), ())), preferred_element_type=F32)


def _split2(x):
    hi = _bf(x)
    return hi, _bf(x - hi.astype(F32))


def _split3(x):
    hi = _bf(x)
    r = x - hi.astype(F32)
    mid = _bf(r)
    return hi, mid, _bf(r - mid.astype(F32))


def _mm_sel(sel, x):
    hi, mid, lo = _split3(x)
    d = functools.partial(jnp.dot, preferred_element_type=F32)
    return (d(sel, hi) + d(sel, mid)) + d(sel, lo)


def _mm_hi(a, b):
    ah, al = _split2(a)
    bh, bl = _split2(b)
    d = functools.partial(jnp.dot, preferred_element_type=F32)
    return d(ah, bh) + (d(ah, bl) + d(al, bh))


def _sigmoid(x):
    return 1.0 / (1.0 + jnp.exp(-x))


def _silu(x):
    return x * _sigmoid(x)


def _softplus(x):
    return jnp.maximum(x, 0.0) + jnp.log(1.0 + jnp.exp(-jnp.abs(x)))


def _rms(x, g):
    return x * lax.rsqrt(jnp.mean(x * x, axis=-1, keepdims=True) + EPS) * g


def _chunk_masks(backward):
    ri = lax.broadcasted_iota(jnp.int32, (CHUNK, CHUNK), 0)
    ci = lax.broadcasted_iota(jnp.int32, (CHUNK, CHUNK), 1)
    if backward:
        return ri <= ci, ri < ci, ri == ci
    return ri >= ci, ri > ci, ri == ci


def _mod_kernel(c_ref, w_ref, b_ref, o_ref):
    o_ref[0] = _mm(_silu(c_ref[...]), w_ref[0]) + b_ref[0]


def _modulation(cond, ada_w, ada_b):
    n_out = ADA_CHUNKS * D_MODEL
    tn = 512
    return pl.pallas_call(
        _mod_kernel,
        grid=(DEPTH, n_out // tn),
        in_specs=[
            pl.BlockSpec((COND_ROWS, D_MODEL), lambda l, j: (0, 0)),
            pl.BlockSpec((1, D_MODEL, tn), lambda l, j: (l, 0, j)),
            pl.BlockSpec((1, 1, tn), lambda l, j: (l, 0, j)),
        ],
        out_specs=pl.BlockSpec((1, COND_ROWS, tn), lambda l, j: (l, 0, j)),
        out_shape=jax.ShapeDtypeStruct((DEPTH, COND_ROWS, n_out), F32),
        compiler_params=_cparams("parallel", "parallel"),
    )(cond, ada_w, ada_b.reshape(DEPTH, 1, n_out))


def _activate_gates(x, bias, a_log):
    lane = lax.broadcasted_iota(jnp.int32, (1, LANE), 1)
    kind = lane % GATE_STRIDE
    is_delta = lane < GATE_B_OFF
    is_mlstm = (lane >= GATE_B_OFF) & (lane < GATE_B_OFF + HEADS_B * GATE_STRIDE)
    y = x + bias
    log_term = jnp.log(1.0 + jnp.exp(-jnp.abs(y)))
    out = jnp.where(is_delta & (kind < 2), _sigmoid(y), x)
    out = jnp.where(is_delta & (kind >= 2) & (kind < 4), -jnp.exp(a_log) * (jnp.maximum(y, 0.0) + log_term), out)
    out = jnp.where(is_mlstm & (kind < 2), y, out)
    return jnp.where(is_mlstm & (kind >= 2) & (kind < 4), -(jnp.maximum(-y, 0.0) + log_term), out)


def _in_proj_kernel(x_ref, mod_ref, g_ref, w_ref, wlo_ref, gb_ref, ga_ref, o_ref, gt_ref, h_scr, hlo_scr, *,
                    gate_off):
    @pl.when(pl.program_id(1) == 0)
    def _():
        m = mod_ref[0]
        hi, lo = _split2(_rms(x_ref[...], g_ref[0]) * (1.0 + m[1:2]) + m[0:1])
        h_scr[...] = hi
        hlo_scr[...] = lo

    res = jnp.dot(h_scr[...], w_ref[0], preferred_element_type=F32)
    o_ref[...] = _bf(res)

    @pl.when(pl.program_id(1) == pl.num_programs(1) - 1)
    def _():
        lanes = slice(gate_off, gate_off + LANE)
        d = functools.partial(jnp.dot, preferred_element_type=F32)
        logits = res[:, lanes] + (d(hlo_scr[...], w_ref[0][:, lanes]) + d(h_scr[...], wlo_ref[0]))
        gt_ref[...] = _activate_gates(logits, gb_ref[0], ga_ref[0])


def _in_proj(x2d, mod, norm_g, w_in, w_gates_lo, gate_bias, gate_a_log, layer, seq, per_request):
    m_rows = x2d.shape[0]
    tm = min(1024, m_rows)
    tn = 768
    if per_request:
        assert seq % tm == 0
    gate_off = T_GATES * LANE - (N_PROJ // tn - 1) * tn
    assert 0 <= gate_off and gate_off + LANE <= tn
    lane_row = pl.BlockSpec((1, 1, LANE), lambda i, j: (layer, 0, 0))
    return pl.pallas_call(
        functools.partial(_in_proj_kernel, gate_off=gate_off),
        grid=(m_rows // tm, N_PROJ // tn),
        in_specs=[
            pl.BlockSpec((tm, D_MODEL), lambda i, j: (i, 0)),
            pl.BlockSpec((1, MOD_ROWS, D_MODEL), lambda i, j: ((i * tm) // seq if per_request else 0, 0, 0)),
            pl.BlockSpec((1, 1, D_MODEL), lambda i, j: (layer, 0, 0)),
            pl.BlockSpec((1, D_MODEL, tn), lambda i, j: (layer, 0, j)),
            pl.BlockSpec((1, D_MODEL, LANE), lambda i, j: (layer, 0, 0)),
            lane_row, lane_row,
        ],
        out_specs=[pl.BlockSpec((tm, tn), lambda i, j: (i, j)), pl.BlockSpec((tm, LANE), lambda i, j: (i, 0))],
        out_shape=[jax.ShapeDtypeStruct((m_rows, N_PROJ), BF16), jax.ShapeDtypeStruct((m_rows, LANE), F32)],
        scratch_shapes=[pltpu.VMEM((tm, D_MODEL), BF16), pltpu.VMEM((tm, D_MODEL), BF16)],
        compiler_params=_cparams("parallel", "arbitrary"),
    )(x2d, mod, norm_g.reshape(DEPTH, 1, D_MODEL), w_in, w_gates_lo, gate_bias, gate_a_log)


PREP_PROBLEMS = 8
CONV_PAD = 8
DELTA_HEAD_TOKENS = 2048
INV_GROUP = 32


def _gate_dense(gt_parts, lane_index):
    row = lax.broadcasted_iota(jnp.int32, (LANE, LANE), 0)
    sel = jnp.where(row == lane_index, 1.0, 0.0).astype(BF16)
    hi, mid, lo = gt_parts
    d = functools.partial(jnp.dot, preferred_element_type=F32)
    return (d(hi, sel) + d(mid, sel)) + d(lo, sel)


def _delta_kernel(q_ref, k_ref, v_ref, z_ref, gt_ref, cq_ref, ck_ref, cv_ref, ng_ref, *rest,
                  seq, heads, has_init, emit_state):
    rest = list(rest)
    s0_ref = rest.pop(0) if has_init else None
    y_ref = rest.pop(0)
    st_ref = rest.pop(0) if emit_state else None
    qs, ks, vs, gates, s_scr, o_scr, u_scr, wq_scr, akd_scr, dk_scr, pw_scr, inv_scr, rhs_scr, pad_scr = rest
    head_group = pl.program_id(1)
    n_chunks = seq // CHUNK
    n_chain = 2 * heads
    head_lanes = lambda hd: slice(hd * DK_A, (hd + 1) * DK_A)

    pad_scr[0:CONV_PAD, :] = jnp.zeros((CONV_PAD, pad_scr.shape[1]), F32)
    pad_scr[CONV_PAD + seq:, :] = jnp.zeros((CONV_PAD, pad_scr.shape[1]), F32)

    def conv_silu(x_ref, w_ref):
        w = w_ref[0]
        pad_scr[CONV_PAD:CONV_PAD + seq, :] = x_ref[...].astype(F32)
        acc = None
        for j in range(CONV_K):
            first = CONV_PAD + j - CONV_K // 2
            term = pad_scr[first:first + seq, :] * w[j:j + 1, :]
            acc = term if acc is None else acc + term
        return _silu(acc)

    def l2n(x):
        parts = [x[:, head_lanes(hd)] for hd in range(heads)]
        return jnp.concatenate(
            [p * lax.rsqrt(jnp.sum(p * p, axis=-1, keepdims=True) + EPS) for p in parts], axis=1)

    qs[...] = l2n(conv_silu(q_ref, cq_ref)) * (DK_A ** -0.5)
    ks[...] = l2n(conv_silu(k_ref, ck_ref))
    vs[...] = conv_silu(v_ref, cv_ref)

    gt_parts = _split3(gt_ref[...])
    for hd in range(heads):
        base = (head_group * heads + hd) * GATE_STRIDE
        for kind in range(4):
            gates[hd * 4 + kind] = _gate_dense(gt_parts, base + kind)
        for d in range(2):
            s_scr[hd * 2 + d] = s0_ref[0, 0, d, hd] if has_init else jnp.zeros((DK_A, DV_A), F32)

    def chunk_rows(c):
        return pl.ds(pl.multiple_of(c * CHUNK, CHUNK), CHUNK)

    n_prob = n_chain * n_chunks
    group = min(INV_GROUP, n_prob)
    prep_chunks = max(1, PREP_PROBLEMS // n_chain)
    ri = lax.broadcasted_iota(jnp.int32, (CHUNK, 2 * CHUNK), 0)
    ci = lax.broadcasted_iota(jnp.int32, (CHUNK, 2 * CHUNK), 1) % CHUNK
    dot = functools.partial(jnp.dot, preferred_element_type=F32)

    def setup_body(i, carry):
        loaded = []
        for cc in range(prep_chunks):
            c = i * prep_chunks + cc
            rows = chunk_rows(c)
            for hd in range(heads):
                loaded.append((c, hd, qs[rows, head_lanes(hd)], ks[rows, head_lanes(hd)], vs[rows, head_lanes(hd)],
                               [(gates[hd * 4 + d, rows, :], gates[hd * 4 + 2 + d, rows, :]) for d in range(2)]))
        products = []
        for c, hd, q, k, v, gate_cols in loaded:
            per_dir = []
            for d in range(2):
                g = gate_cols[d][1]
                incl, strict, _ = _chunk_masks(d == 1)
                rhs = jnp.concatenate(
                    [jnp.where(strict, g[:, :CHUNK], 0.0), jnp.zeros((CHUNK, LANE - CHUNK), F32), g], axis=1)
                per_dir.append(_mm_sel(jnp.where(incl, 1.0, 0.0).astype(BF16), rhs))
            gram = _mm_nt(jnp.concatenate([k, q], axis=0), k)
            products.append((gram[:CHUNK], gram[CHUNK:], per_dir))
        results = []
        pairs = []
        for (c, hd, q, k, v, gate_cols), (kk, qk, per_dir) in zip(loaded, products):
            lows = []
            for d in range(2):
                bt, g = gate_cols[d]
                incl, strict, _ = _chunk_masks(d == 1)
                cs = per_dir[d]
                gc = cs[:, LANE:]
                g_last = gc[0:1, :] if d == 1 else gc[CHUNK - 1:CHUNK, :]
                egc = jnp.exp(gc)
                dec = jnp.where(incl, jnp.exp(cs[:, :CHUNK]), 0.0)
                lows.append(jnp.where(strict, bt[:, :CHUNK] * kk * dec, 0.0))
                rhs2 = jnp.concatenate([v * bt, k * (bt * egc)], axis=1)
                akd = _bf(jnp.concatenate([qk * dec, (k * jnp.exp(g_last - gc)).T], axis=0))
                results.append(((hd * n_chunks + c) * 2 + d, rhs2, akd, _bf(q * egc), jnp.exp(g_last)))
            pairs.append((hd * n_chunks + c, jnp.concatenate(lows, axis=1)))
        for p, rhs2, akd, qd, decay in results:
            rhs_scr[p] = rhs2
            akd_scr[p] = akd
            wq_scr[p, CHUNK:, :] = qd
            dk_scr[p] = decay
        for pp, low in pairs:
            pw_scr[pp] = low
            inv_scr[pp] = jnp.where(ri == ci, 1.0, 0.0) - jnp.where((ri // 2) == (ci // 2), low, 0.0)
        return carry

    lax.fori_loop(0, n_chunks // prep_chunks, setup_body, 0)

    lane_pair = lax.broadcasted_iota(jnp.int32, (CHUNK, 2 * CHUNK), 1)

    def block_diagonal(y):
        zero = jnp.zeros_like(y)
        return jnp.concatenate(
            [jnp.where(lane_pair < CHUNK, y, zero), jnp.where(lane_pair < CHUNK, zero, y)], axis=0)

    def pair_product(x, y):
        xh, xl = _split2(x)
        both = dot(jnp.concatenate([xh, xl], axis=0), block_diagonal(_bf(y)))
        return both[:CHUNK] + both[CHUNK:]

    n_pairs = n_prob // 2
    pair_group = min(INV_GROUP // 2, n_pairs)

    def doubling_pass(size):
        joins = ((ri // (2 * size)) == (ci // (2 * size))) & ((ri // size) != (ci // size))

        def body(i, carry):
            loaded = []
            for j in range(pair_group):
                pp = i * pair_group + j
                loaded.append((pp, pw_scr[pp], inv_scr[pp]))
            partial = [(pp, inv, pair_product(inv, jnp.where(joins, low, 0.0))) for pp, low, inv in loaded]
            results = [(pp, inv - pair_product(t, inv)) for pp, inv, t in partial]
            for pp, new_inv in results:
                inv_scr[pp] = new_inv
            return carry

        lax.fori_loop(0, n_pairs // pair_group, body, 0)

    size = 2
    while size < CHUNK:
        doubling_pass(size)
        size *= 2

    def solve_body(i, carry):
        loaded = []
        for j in range(pair_group):
            pp = i * pair_group + j
            inv = inv_scr[pp]
            for d in range(2):
                loaded.append((pp * 2 + d, inv[:, d * CHUNK:(d + 1) * CHUNK], rhs_scr[pp * 2 + d]))
        results = []
        for p, inv, rhs2 in loaded:
            ih, il = _split2(inv)
            rh, rl = _split2(rhs2)
            by_hi = dot(jnp.concatenate([ih, il], axis=0), rh)
            results.append((p, by_hi[:CHUNK] + (dot(ih, rl) + by_hi[CHUNK:])))
        for p, sol in results:
            u_scr[p] = sol[:, :DV_A]
            wq_scr[p, :CHUNK, :] = _bf(sol[:, DV_A:])
        return carry

    lax.fori_loop(0, n_pairs // pair_group, solve_body, 0)

    def scan_body(i, carry):
        loaded = []
        for chain in range(n_chain):
            c = n_chunks - 1 - i if chain % 2 == 1 else i
            p = ((chain // 2) * n_chunks + c) * 2 + chain % 2
            loaded.append((c, s_scr[chain], u_scr[p], wq_scr[p], akd_scr[p], dk_scr[p]))
        first = [dot(wq, _bf(state)) for c, state, u, wq, akd, decay in loaded]
        second = [dot(akd, _bf(u - ws[:CHUNK])) for (c, state, u, wq, akd, decay), ws in zip(loaded, first)]
        results = [(c, ws[CHUNK:] + av[:CHUNK], decay * state + av[CHUNK:])
                   for (c, state, u, wq, akd, decay), ws, av in zip(loaded, first, second)]
        for chain, (c, o, state) in enumerate(results):
            o_scr[chain, chunk_rows(c), :] = o
            s_scr[chain] = state
        return carry

    lax.fori_loop(0, n_chunks, scan_body, 0)

    z = z_ref[...].astype(F32)
    for hd in range(heads):
        y_ref[:, head_lanes(hd)] = _bf(_rms(o_scr[hd * 2] + o_scr[hd * 2 + 1], ng_ref[0])
                                       * _silu(z[:, head_lanes(hd)]))
        if emit_state:
            for d in range(2):
                st_ref[0, d, hd] = s_scr[hd * 2 + d]


def _delta_mixer(proj, scan_gates, conv_w, norm_g, layer, batch, seq, state0):
    has_init = state0 is not None
    emit_state = not has_init
    heads = next(h for h in (4, 2, 1) if HEADS_A % h == 0 and h * seq <= max(DELTA_HEAD_TOKENS, seq))
    n_groups = HEADS_A // heads
    width = heads * DK_A
    col = lambda tile: (lambda r, g: (r, tile // heads + g))
    cw = lambda part: (lambda r, g: (layer, 0, part * n_groups + g))
    in_specs = [
        pl.BlockSpec((seq, width), col(T_QA)),
        pl.BlockSpec((seq, width), col(T_KA)),
        pl.BlockSpec((seq, width), col(T_VA)),
        pl.BlockSpec((seq, width), col(T_ZA)),
        pl.BlockSpec((seq, LANE), lambda r, g: (r, 0)),
        pl.BlockSpec((1, CONV_K, width), cw(0)),
        pl.BlockSpec((1, CONV_K, width), cw(1)),
        pl.BlockSpec((1, CONV_K, width), cw(2)),
        pl.BlockSpec((1, 1, DV_A), lambda r, g: (layer, 0, 0)),
    ]
    args = [proj, proj, proj, proj, scan_gates, conv_w, conv_w, conv_w, norm_g.reshape(DEPTH, 1, DV_A)]
    if has_init:
        in_specs.append(pl.BlockSpec((1, 1, 2, heads, DK_A, DV_A), lambda r, g: (r, layer, 0, g, 0, 0)))
        args.append(state0)
    out_specs = [pl.BlockSpec((seq, width), lambda r, g: (r, g))]
    out_shape = [jax.ShapeDtypeStruct((batch * seq, HEADS_A * DV_A), BF16)]
    if emit_state:
        out_specs.append(pl.BlockSpec((1, 2, heads, DK_A, DV_A), lambda r, g: (r, 0, g, 0, 0)))
        out_shape.append(jax.ShapeDtypeStruct((batch, 2, HEADS_A, DK_A, DV_A), F32))
    n_chain = 2 * heads
    n_prob = n_chain * (seq // CHUNK)
    outs = pl.pallas_call(
        functools.partial(_delta_kernel, seq=seq, heads=heads, has_init=has_init, emit_state=emit_state),
        grid=(batch, n_groups),
        in_specs=in_specs,
        out_specs=out_specs,
        out_shape=out_shape,
        scratch_shapes=[
            pltpu.VMEM((seq, width), F32), pltpu.VMEM((seq, width), F32), pltpu.VMEM((seq, width), F32),
            pltpu.VMEM((2 * n_chain, seq, LANE), F32), pltpu.VMEM((n_chain, DK_A, DV_A), F32),
            pltpu.VMEM((n_chain, seq, DV_A), F32),
            pltpu.VMEM((n_prob, CHUNK, DV_A), F32), pltpu.VMEM((n_prob, 2 * CHUNK, DK_A), BF16),
            pltpu.VMEM((n_prob, CHUNK + DK_A, CHUNK), BF16), pltpu.VMEM((n_prob, 1, DV_A), F32),
            pltpu.VMEM((n_prob // 2, CHUNK, 2 * CHUNK), F32), pltpu.VMEM((n_prob // 2, CHUNK, 2 * CHUNK), F32),
            pltpu.VMEM((n_prob, CHUNK, DV_A + DK_A), F32),
            pltpu.VMEM((seq + 2 * CONV_PAD, width), F32),
        ],
        compiler_params=_cparams("parallel", "parallel"),
    )(*args)
    return (outs[0], outs[1]) if emit_state else (outs[0], None)


HEADS_PER_STEP_B = 4


def _mlstm_kernel(q_ref, k_ref, v_ref, og_ref, gt_ref, ng_ref, *rest, seq, has_init, emit_state):
    rest = list(rest)
    if has_init:
        c0_ref, n0_ref, m0_ref = rest[:3]
        rest = rest[3:]
    y_ref = rest.pop(0)
    if emit_state:
        co_ref, no_ref, mo_ref = rest[:3]
        rest = rest[3:]
    gates, c_scr, n_scr, m_scr, h_scr = rest
    pair = pl.program_id(1)
    n_chunks = seq // CHUNK
    gt_parts = _split3(gt_ref[...])

    for j in range(HEADS_PER_STEP_B):
        base = GATE_B_OFF + (pair * HEADS_PER_STEP_B + j) * GATE_STRIDE
        for kind in range(4):
            gates[j * 4 + kind] = _gate_dense(gt_parts, base + kind)
        for d in range(2):
            idx = j * 2 + d
            if has_init:
                c_scr[idx] = c0_ref[0, 0, d, j]
                n_scr[idx] = n0_ref[0, 0, d, j]
                m_scr[idx] = m0_ref[0, 0, d, j]
            else:
                c_scr[idx] = jnp.zeros((DK_B, DV_B), F32)
                n_scr[idx] = jnp.zeros((1, DK_B), F32)
                m_scr[idx] = jnp.zeros((1, LANE), F32)

    n_chain = 2 * HEADS_PER_STEP_B

    def scan_body(i, carry):
        loaded = []
        for chain in range(n_chain):
            j, d = chain // 2, chain % 2
            c = n_chunks - 1 - i if d == 1 else i
            rows = pl.ds(pl.multiple_of(c * CHUNK, CHUNK), CHUNK)
            loaded.append((chain, rows, q_ref[rows, j * DK_B:(j + 1) * DK_B].astype(F32),
                           k_ref[rows, j * DK_B:(j + 1) * DK_B].astype(F32) * (DK_B ** -0.5),
                           v_ref[rows, j * DV_B:(j + 1) * DV_B],
                           gates[j * 4 + d, rows, :], gates[j * 4 + 2 + d, rows, :],
                           c_scr[chain], n_scr[chain], m_scr[chain]))
        stage1 = []
        for chain, rows, q, k, v, ig, lf, cmat, nvec, m_prev in loaded:
            incl, strict, diag = _chunk_masks(chain % 2 == 1)
            rhs = jnp.concatenate(
                [jnp.where(strict, lf[:, :CHUNK], 0.0) + jnp.where(diag, ig[:, :CHUNK], 0.0),
                 jnp.zeros((CHUNK, LANE - CHUNK), F32), lf], axis=1)
            stage1.append((_mm_sel(jnp.where(incl, 1.0, 0.0).astype(BF16), rhs), _mm_nt(q, k), _mm(q, cmat)))
        stage2 = []
        for (chain, rows, q, k, v, ig, lf, cmat, nvec, m_prev), (cs, qk, qc) in zip(loaded, stage1):
            incl, _, _ = _chunk_masks(chain % 2 == 1)
            bc = cs[:, LANE:]
            b_last = bc[0:1, :] if chain % 2 == 1 else bc[CHUNK - 1:CHUNK, :]
            d_log = jnp.where(incl, cs[:, :CHUNK], -jnp.inf)
            d_max = jnp.max(d_log, axis=1, keepdims=True)
            tok = b_last - bc + ig
            m_new = jnp.maximum(b_last + m_prev, jnp.max(tok, axis=0, keepdims=True))
            w_prev = jnp.exp(b_last + m_prev - m_new)
            kw = k * jnp.exp(tok - m_new)[:, :DK_B]
            stage2.append((bc, d_log, d_max, m_new, w_prev, kw))
        stage3 = []
        for (chain, rows, q, k, v, ig, lf, cmat, nvec, m_prev), (cs, qk, qc), (bc, d_log, d_max, m_new, w_prev, kw) \
                in zip(loaded, stage1, stage2):
            m_t = jnp.maximum(bc + m_prev, d_max)
            w_inter = jnp.exp(bc + m_prev - m_t)
            pm = jnp.exp(d_log - m_t[:, :CHUNK]) * qk
            den = jnp.sum(w_inter[:, :DK_B] * (q * nvec) + pm, axis=1, keepdims=True)
            stage3.append((m_t, w_inter, den, _mm(pm, v), _mm_tn(kw, v)))
        results = []
        for (chain, rows, q, k, v, ig, lf, cmat, nvec, m_prev), (cs, qk, qc), (bc, d_log, d_max, m_new, w_prev, kw), \
                (m_t, w_inter, den, pv, inc) in zip(loaded, stage1, stage2, stage3):
            results.append((chain, rows, (w_inter * qc + pv) / jnp.maximum(jnp.abs(den), jnp.exp(-m_t)),
                            w_prev * cmat + inc,
                            w_prev[:, :DK_B] * nvec + jnp.sum(kw, axis=0, keepdims=True), m_new))
        for chain, rows, h, cmat, nvec, m_new in results:
            h_scr[chain, rows, :] = h
            c_scr[chain] = cmat
            n_scr[chain] = nvec
            m_scr[chain] = m_new
        return carry

    lax.fori_loop(0, n_chunks, scan_body, 0)

    og = og_ref[...].astype(F32)
    for j in range(HEADS_PER_STEP_B):
        h = h_scr[j * 2] + h_scr[j * 2 + 1]
        y_ref[:, j * DV_B:(j + 1) * DV_B] = _bf(_rms(h, ng_ref[0]) * _sigmoid(og[:, j * DV_B:(j + 1) * DV_B]))
        if emit_state:
            for d in range(2):
                co_ref[0, d, j] = c_scr[j * 2 + d]
                no_ref[0, d, j] = n_scr[j * 2 + d]
                mo_ref[0, d, j] = m_scr[j * 2 + d]


def _mlstm_mixer(proj, scan_gates, norm_g, layer, batch, seq, state0):
    has_init = state0 is not None
    emit_state = not has_init
    hp = HEADS_PER_STEP_B
    n_pairs = HEADS_B // hp
    in_specs = [
        pl.BlockSpec((seq, hp * DK_B), lambda r, p: (r, T_QB * LANE // (hp * DK_B) + p)),
        pl.BlockSpec((seq, hp * DK_B), lambda r, p: (r, T_KB * LANE // (hp * DK_B) + p)),
        pl.BlockSpec((seq, hp * DV_B), lambda r, p: (r, T_VB * LANE // (hp * DV_B) + p)),
        pl.BlockSpec((seq, hp * DV_B), lambda r, p: (r, T_OB * LANE // (hp * DV_B) + p)),
        pl.BlockSpec((seq, LANE), lambda r, p: (r, 0)),
        pl.BlockSpec((1, 1, DV_B), lambda r, p: (layer, 0, 0)),
    ]
    args = [proj, proj, proj, proj, scan_gates, norm_g.reshape(DEPTH, 1, DV_B)]
    if has_init:
        c0, n0, m0 = state0
        in_specs += [
            pl.BlockSpec((1, 1, 2, hp, DK_B, DV_B), lambda r, p: (r, layer, 0, p, 0, 0)),
            pl.BlockSpec((1, 1, 2, hp, 1, DK_B), lambda r, p: (r, layer, 0, p, 0, 0)),
            pl.BlockSpec((1, 1, 2, hp, 1, LANE), lambda r, p: (r, layer, 0, p, 0, 0)),
        ]
        args += [c0, n0, m0]
    out_specs = [pl.BlockSpec((seq, hp * DV_B), lambda r, p: (r, p))]
    out_shape = [jax.ShapeDtypeStruct((batch * seq, HEADS_B * DV_B), BF16)]
    if emit_state:
        out_specs += [
            pl.BlockSpec((1, 2, hp, DK_B, DV_B), lambda r, p: (r, 0, p, 0, 0)),
            pl.BlockSpec((1, 2, hp, 1, DK_B), lambda r, p: (r, 0, p, 0, 0)),
            pl.BlockSpec((1, 2, hp, 1, LANE), lambda r, p: (r, 0, p, 0, 0)),
        ]
        out_shape += [
            jax.ShapeDtypeStruct((batch, 2, HEADS_B, DK_B, DV_B), F32),
            jax.ShapeDtypeStruct((batch, 2, HEADS_B, 1, DK_B), F32),
            jax.ShapeDtypeStruct((batch, 2, HEADS_B, 1, LANE), F32),
        ]
    outs = pl.pallas_call(
        functools.partial(_mlstm_kernel, seq=seq, has_init=has_init, emit_state=emit_state),
        grid=(batch, n_pairs),
        in_specs=in_specs,
        out_specs=out_specs,
        out_shape=out_shape,
        scratch_shapes=[
            pltpu.VMEM((4 * hp, seq, LANE), F32), pltpu.VMEM((2 * hp, DK_B, DV_B), F32),
            pltpu.VMEM((2 * hp, 1, DK_B), F32), pltpu.VMEM((2 * hp, 1, LANE), F32),
            pltpu.VMEM((2 * hp, seq, DV_B), F32),
        ],
        compiler_params=_cparams("parallel", "parallel"),
    )(*args)
    if emit_state:
        return outs[0], (outs[1], outs[2][:, :, :, 0, :], outs[3][:, :, :, 0, 0])
    return outs[0], None


Q_SLABS_C = HEADS_C * HEAD_DIM_C // LANE
HEAD_ORDER_C = [h for s in range(Q_SLABS_C) for h in (s, s + GROUP_C)]


def _attend(q_slabs, segments, sink):
    lane = lax.broadcasted_iota(jnp.int32, (1, LANE), 1)
    scale = HEAD_DIM_C ** -0.5
    nt = (((1,), (1,)), ((), ()))
    operands = []
    for kv in range(KV_HEADS_C):
        mine = (lane >= kv * HEAD_DIM_C) & (lane < (kv + 1) * HEAD_DIM_C)
        operands.append([(_bf(jnp.where(mine, k, 0.0)), _bf(jnp.where(mine, v, 0.0)), valid)
                         for k, v, valid in segments])
    problems = [(s, kv) for s in range(len(q_slabs)) for kv in range(KV_HEADS_C)]
    q_bf = [_bf(q) for q in q_slabs]
    scores = [[lax.dot_general(q_bf[s], kb, nt, preferred_element_type=F32) * scale for kb, _, _ in operands[kv]]
              for s, kv in problems]
    weights = []
    for (s, kv), per_seg in zip(problems, scores):
        per_seg = [x if valid is None else jnp.where(valid, x, -jnp.inf)
                   for x, (_, _, valid) in zip(per_seg, operands[kv])]
        head = kv * GROUP_C + s
        tiles = [x[:, t * LANE:(t + 1) * LANE] for x in per_seg for t in range(x.shape[1] // LANE)]
        m = jnp.maximum(jnp.max(functools.reduce(jnp.maximum, tiles), axis=1, keepdims=True),
                        sink[head:head + 1, 0:1])
        es = [jnp.exp(x - m) for x in per_seg]
        e_tiles = [e[:, t * LANE:(t + 1) * LANE] for e in es for t in range(e.shape[1] // LANE)]
        den = (jnp.sum(functools.reduce(jnp.add, e_tiles), axis=1, keepdims=True)
               + jnp.exp(sink[head:head + 1, 0:1] - m))
        weights.append((es, den))
    outs = []
    for (s, kv), (es, den) in zip(problems, weights):
        acc = None
        for e, (_, vb, _) in zip(es, operands[kv]):
            part = jnp.dot(_bf(e), vb, preferred_element_type=F32)
            acc = part if acc is None else acc + part
        outs.append(acc / den)
    return [sum(outs[s * KV_HEADS_C + 1:(s + 1) * KV_HEADS_C], outs[s * KV_HEADS_C]) for s in range(len(q_slabs))]


def _ctx_attn_kernel(q_ref, k_ref, v_ref, sink_ref, o_ref, *, seq):
    q_slabs = [q_ref[:, s * LANE:(s + 1) * LANE] for s in range(Q_SLABS_C)]
    outs = _attend(q_slabs, [(k_ref[...], v_ref[...], None)], sink_ref[0])
    for s, o in enumerate(outs):
        o_ref[:, s * LANE:(s + 1) * LANE] = _bf(o)


def _ctx_attention(proj, sink, layer, batch, seq):
    width = HEADS_C * HEAD_DIM_C
    return pl.pallas_call(
        functools.partial(_ctx_attn_kernel, seq=seq),
        grid=(batch,),
        in_specs=[
            pl.BlockSpec((seq, width), lambda r: (r, T_QC * LANE // width)),
            pl.BlockSpec((seq, LANE), lambda r: (r, T_KC)),
            pl.BlockSpec((seq, LANE), lambda r: (r, T_VC)),
            pl.BlockSpec((1, HEADS_C, LANE), lambda r: (layer, 0, 0)),
        ],
        out_specs=pl.BlockSpec((seq, width), lambda r: (r, 0)),
        out_shape=jax.ShapeDtypeStruct((batch * seq, width), BF16),
        compiler_params=_cparams("parallel"),
    )(proj, proj, proj, sink)


def _rope(x, cos, sin):
    quarter = HEAD_DIM_C // 4
    lane = lax.broadcasted_iota(jnp.int32, (1, LANE), 1)
    first = (lane % (2 * quarter)) < quarter
    partner = jnp.where(first, -pltpu.roll(x, LANE - quarter, 1), pltpu.roll(x, quarter, 1))
    return x * cos + partner * sin


def _latent_attn_kernel(q_ref, k_ref, v_ref, ck_ref, cv_ref, cq_ref, sq_ref, cos_ref, sin_ref, sink_ref, o_ref, *, seq):
    blk = pl.program_id(1)
    span = Q_BLOCK + 2 * WINDOW
    start = blk * Q_BLOCK
    k_start = pl.multiple_of(jnp.clip(start - WINDOW, 0, seq - span), Q_BLOCK)
    win = pl.ds(k_start, span)
    cq = cq_ref[...]
    sq = sq_ref[...]
    q_slabs = [_rope(q_ref[:, s * LANE:(s + 1) * LANE].astype(F32), cq, sq) for s in range(Q_SLABS_C)]
    k = _rope(k_ref[win, :].astype(F32), cos_ref[win, :], sin_ref[win, :])
    q_pos = start + lax.broadcasted_iota(jnp.int32, (Q_BLOCK, 1), 0)
    k_pos = k_start + lax.broadcasted_iota(jnp.int32, (1, span), 1)
    valid = jnp.abs(q_pos - k_pos) <= WINDOW
    outs = _attend(q_slabs, [(k, v_ref[win, :], valid), (ck_ref[0, 0], cv_ref[0, 0], None)], sink_ref[0])
    for s, o in enumerate(outs):
        o_ref[:, s * LANE:(s + 1) * LANE] = _bf(o)


def _latent_attention(proj, cache_k, cache_v, cos, sin, sink, layer, batch, seq):
    width = HEADS_C * HEAD_DIM_C
    n_blk = seq // Q_BLOCK
    past = cache_k.shape[2]
    return pl.pallas_call(
        functools.partial(_latent_attn_kernel, seq=seq),
        grid=(batch, n_blk),
        in_specs=[
            pl.BlockSpec((Q_BLOCK, width), lambda r, i: (r * n_blk + i, T_QC * LANE // width)),
            pl.BlockSpec((seq, LANE), lambda r, i: (r, T_KC)),
            pl.BlockSpec((seq, LANE), lambda r, i: (r, T_VC)),
            pl.BlockSpec((1, 1, past, LANE), lambda r, i: (r, layer, 0, 0)),
            pl.BlockSpec((1, 1, past, LANE), lambda r, i: (r, layer, 0, 0)),
            pl.BlockSpec((Q_BLOCK, LANE), lambda r, i: (i, 0)),
            pl.BlockSpec((Q_BLOCK, LANE), lambda r, i: (i, 0)),
            pl.BlockSpec((seq, LANE), lambda r, i: (0, 0)),
            pl.BlockSpec((seq, LANE), lambda r, i: (0, 0)),
            pl.BlockSpec((1, HEADS_C, LANE), lambda r, i: (layer, 0, 0)),
        ],
        out_specs=pl.BlockSpec((Q_BLOCK, width), lambda r, i: (r * n_blk + i, 0)),
        out_shape=jax.ShapeDtypeStruct((batch * seq, width), BF16),
        compiler_params=_cparams("parallel", "parallel"),
    )(proj, proj, proj, cache_k, cache_v, cos, sin, cos, sin, sink)


def _rope_tables(seq):
    quarter = HEAD_DIM_C // 4
    pos = jnp.arange(seq)
    row = (pos // GRID_W).astype(F32)
    col = (pos % GRID_W).astype(F32)
    inv = jnp.power(ROPE_BASE, -jnp.arange(quarter, dtype=F32) / quarter)
    ang_row = row[:, None] * inv[None, :]
    ang_col = col[:, None] * inv[None, :]
    ang = jnp.concatenate([ang_row, ang_row, ang_col, ang_col], axis=1)
    ang = jnp.concatenate([ang] * (LANE // HEAD_DIM_C), axis=1)
    return jnp.cos(ang), jnp.sin(ang)


def _mix_kernel(x_ref, ya_ref, yb_ref, yc_ref, ga_ref, gb_ref, gc_ref, mod_ref, n2_ref,
                wa_ref, wb_ref, wc_ref, wo_ref, wr_ref, xo_ref, h_ref, aff_ref, afft_ref):
    m = mod_ref[0]
    mixed = (_sigmoid(ga_ref[...].astype(F32)) * _mm(ya_ref[...], wa_ref[0])
             + _sigmoid(gb_ref[...].astype(F32)) * _mm(yb_ref[...], wb_ref[0])
             + _sigmoid(gc_ref[...].astype(F32)) * _mm(yc_ref[...], wc_ref[0]))
    x = x_ref[...] + m[2:3] * _mm(mixed, wo_ref[0])
    xo_ref[...] = x
    h = _bf(_rms(x, n2_ref[0]) * (1.0 + m[4:5]) + m[3:4])
    h_ref[...] = h
    logits = jnp.dot(h, wr_ref[0], preferred_element_type=F32)
    lane = lax.broadcasted_iota(jnp.int32, (1, LANE), 1)
    logits = jnp.where(lane < N_EXPERTS, logits, -jnp.inf)
    e = jnp.exp(logits - jnp.max(logits, axis=1, keepdims=True))
    aff = e / jnp.sum(e, axis=1, keepdims=True)
    aff_ref[...] = aff
    afft_ref[...] = aff.T[:N_EXPERTS, :]


def _mix(x2d, ya, yb, yc, proj, mod, norm2_g, wa, wb, wc, wo, wr, layer, seq, per_request):
    m_rows = x2d.shape[0]
    tm = min(512, seq)
    gate_blk = lambda tile: (lambda i: (i, tile * LANE // D_MODEL))
    wspec = lambda w: pl.BlockSpec((1,) + w.shape[1:], lambda i: (layer, 0, 0))
    branch = pl.BlockSpec((tm, ya.shape[1]), lambda i: (i, 0))
    return pl.pallas_call(
        _mix_kernel,
        grid=(m_rows // tm,),
        in_specs=[
            pl.BlockSpec((tm, D_MODEL), lambda i: (i, 0)),
            branch, branch, branch,
            pl.BlockSpec((tm, D_MODEL), gate_blk(T_GA)),
            pl.BlockSpec((tm, D_MODEL), gate_blk(T_GB)),
            pl.BlockSpec((tm, D_MODEL), gate_blk(T_GC)),
            pl.BlockSpec((1, MOD_ROWS, D_MODEL), lambda i: ((i * tm) // seq if per_request else 0, 0, 0)),
            pl.BlockSpec((1, 1, D_MODEL), lambda i: (layer, 0, 0)),
            wspec(wa), wspec(wb), wspec(wc), wspec(wo), wspec(wr),
        ],
        out_specs=[
            pl.BlockSpec((tm, D_MODEL), lambda i: (i, 0)),
            pl.BlockSpec((tm, D_MODEL), lambda i: (i, 0)),
            pl.BlockSpec((tm, LANE), lambda i: (i, 0)),
            pl.BlockSpec((N_EXPERTS, tm), lambda i: (0, i)),
        ],
        out_shape=[
            jax.ShapeDtypeStruct((m_rows, D_MODEL), F32),
            jax.ShapeDtypeStruct((m_rows, D_MODEL), BF16),
            jax.ShapeDtypeStruct((m_rows, LANE), F32),
            jax.ShapeDtypeStruct((N_EXPERTS, m_rows), F32),
        ],
        compiler_params=_cparams("parallel"),
    )(x2d, ya, yb, yc, proj, proj, proj, mod, norm2_g.reshape(DEPTH, 1, D_MODEL), wa, wb, wc, wo, wr)


SCATTER_K = 512


def _gather_kernel(aff_ref, afft_ref, h_ref, xs_ref, gate_ref, rankc_ref, rank_scr, onehot_scr, col_scr, *,
                   seq, cap):
    n_blk = seq // LANE
    ri = lax.broadcasted_iota(jnp.int32, (LANE, LANE), 0)
    ci = lax.broadcasted_iota(jnp.int32, (LANE, LANE), 1)
    lane = lax.broadcasted_iota(jnp.int32, (1, LANE), 1)
    slot = lax.broadcasted_iota(jnp.int32, (cap, 1), 0).astype(F32)

    def expert_body(e, carry):
        a_row = afft_ref[pl.ds(e, 1), :]
        a_col = jnp.sum(jnp.where(lane == e, aff_ref[...], 0.0), axis=1, keepdims=True)
        col_scr[...] = jnp.broadcast_to(a_col, (seq, LANE))
        rank_parts = []
        for tb in range(n_blk):
            a_t = a_row[:, tb * LANE:(tb + 1) * LANE]
            count = jnp.zeros((LANE, LANE), F32)
            for sb in range(n_blk):
                a_s = col_scr[sb * LANE:(sb + 1) * LANE, :]
                if sb < tb:
                    beats = a_s >= a_t
                elif sb > tb:
                    beats = a_s > a_t
                else:
                    beats = (a_s > a_t) | ((a_s == a_t) & (ri < ci))
                count = count + jnp.where(beats, 1.0, 0.0)
            rank_parts.append(jnp.sum(count, axis=0, keepdims=True))
        rank = jnp.concatenate(rank_parts, axis=1)
        rank_scr[pl.ds(e, 1), :] = rank
        chosen = rank == slot
        onehot_scr[pl.ds(pl.multiple_of(e * cap, cap), cap), :] = jnp.where(chosen, 1.0, 0.0).astype(BF16)
        gate_ref[e, 0] = jnp.sum(jnp.where(chosen, a_row, 0.0), axis=1, keepdims=True)
        return carry

    lax.fori_loop(0, N_EXPERTS, expert_body, 0)
    ranks = jnp.concatenate([rank_scr[...], jnp.zeros((LANE - N_EXPERTS, seq), F32)], axis=0)
    rankc_ref[...] = ranks.T
    per_group = SCATTER_K // cap
    h = h_ref[...]
    for i in range(N_EXPERTS // per_group):
        rows = jnp.dot(onehot_scr[i * SCATTER_K:(i + 1) * SCATTER_K, :], h, preferred_element_type=F32)
        xs_ref[i * per_group:(i + 1) * per_group, 0] = _bf(rows).reshape(per_group, cap, D_MODEL)


def _gather(aff, afft, h2, batch, seq, cap):
    return pl.pallas_call(
        functools.partial(_gather_kernel, seq=seq, cap=cap),
        grid=(batch,),
        in_specs=[
            pl.BlockSpec((seq, LANE), lambda r: (r, 0)),
            pl.BlockSpec((N_EXPERTS, seq), lambda r: (0, r)),
            pl.BlockSpec((seq, D_MODEL), lambda r: (r, 0)),
        ],
        out_specs=[
            pl.BlockSpec((N_EXPERTS, 1, cap, D_MODEL), lambda r: (0, r, 0, 0)),
            pl.BlockSpec((N_EXPERTS, 1, cap, 1), lambda r: (0, r, 0, 0)),
            pl.BlockSpec((seq, LANE), lambda r: (r, 0)),
        ],
        out_shape=[
            jax.ShapeDtypeStruct((N_EXPERTS, batch, cap, D_MODEL), BF16),
            jax.ShapeDtypeStruct((N_EXPERTS, batch, cap, 1), F32),
            jax.ShapeDtypeStruct((batch * seq, LANE), F32),
        ],
        scratch_shapes=[pltpu.VMEM((N_EXPERTS, seq), F32), pltpu.VMEM((N_EXPERTS * cap, seq), BF16),
                        pltpu.VMEM((seq, LANE), F32)],
        compiler_params=_cparams("parallel"),
    )(aff, afft, h2)


def _ffn_kernel(*refs, n_groups):
    x_refs, gate_refs = refs[:n_groups], refs[n_groups:2 * n_groups]
    wg_ref, wu_ref, wd_ref = refs[2 * n_groups:2 * n_groups + 3]
    y_refs = refs[2 * n_groups + 3:]
    wg, wu, wd = _bf(wg_ref[0, 0]), _bf(wu_ref[0, 0]), _bf(wd_ref[0, 0])
    d = functools.partial(jnp.dot, preferred_element_type=F32)
    for x_ref, gate_ref, y_ref in zip(x_refs, gate_refs, y_refs):
        x = x_ref[0]
        hid = _silu(d(x, wg)) * d(x, wu)
        y_ref[0] = _bf(d(_bf(hid), wd) * gate_ref[0])


def _expert_ffn(xs_groups, gate_groups, w_gate, w_up, w_down, layer):
    n_groups = len(xs_groups)
    row_specs = [pl.BlockSpec((1, xs.shape[1], D_MODEL), lambda e: (e, 0, 0)) for xs in xs_groups]
    gate_specs = [pl.BlockSpec((1, g.shape[1], 1), lambda e: (e, 0, 0)) for g in gate_groups]
    return pl.pallas_call(
        functools.partial(_ffn_kernel, n_groups=n_groups),
        grid=(N_EXPERTS,),
        in_specs=row_specs + gate_specs + [
            pl.BlockSpec((1, 1, D_MODEL, D_EXPERT), lambda e: (layer, e, 0, 0)),
            pl.BlockSpec((1, 1, D_MODEL, D_EXPERT), lambda e: (layer, e, 0, 0)),
            pl.BlockSpec((1, 1, D_EXPERT, D_MODEL), lambda e: (layer, e, 0, 0)),
        ],
        out_specs=row_specs,
        out_shape=[jax.ShapeDtypeStruct(xs.shape, BF16) for xs in xs_groups],
        compiler_params=_cparams("parallel"),
    )(*xs_groups, *gate_groups, w_gate, w_up, w_down)


def _scatter_kernel(x_ref, rankc_ref, ye_ref, mod_ref, fg_ref, o_ref, acc, *, cap, final):
    g = pl.program_id(1)
    n_groups = pl.num_programs(1)
    per_lane_tile = LANE // cap
    rank_parts = _split3(rankc_ref[...])
    row = lax.broadcasted_iota(jnp.int32, (LANE, LANE), 0)
    lane = lax.broadcasted_iota(jnp.int32, (LANE, LANE), 1)
    lane_slot = (lax.broadcasted_iota(jnp.int32, (1, LANE), 1) % cap).astype(F32)
    d = functools.partial(jnp.dot, preferred_element_type=F32)
    tiles = []
    for b in range(SCATTER_K // LANE):
        first = (g * (SCATTER_K // LANE) + b) * per_lane_tile
        sel = jnp.where(row == first + lane // cap, 1.0, 0.0).astype(BF16)
        hi, mid, lo = rank_parts
        token_rank = (d(hi, sel) + d(mid, sel)) + d(lo, sel)
        tiles.append(jnp.where(token_rank == lane_slot, 1.0, 0.0).astype(BF16))
    onehot = jnp.concatenate(tiles, axis=1)
    spread = d(onehot, ye_ref[:, 0].reshape(SCATTER_K, D_MODEL))

    @pl.when(g == 0)
    def _():
        acc[...] = spread

    @pl.when(g > 0)
    def _():
        acc[...] += spread

    @pl.when(g == n_groups - 1)
    def _():
        x = x_ref[...] + mod_ref[0][5:6] * acc[...]
        o_ref[...] = _rms(x, fg_ref[...]) if final else x


def _scatter(x2d, rankc, ye, mod, final_g, batch, seq, cap, per_request, final):
    experts_per_group = SCATTER_K // cap
    slots = pl.BlockSpec((experts_per_group, 1, cap, D_MODEL), lambda r, g: (g, r, 0, 0))
    return pl.pallas_call(
        functools.partial(_scatter_kernel, cap=cap, final=final),
        grid=(batch, N_EXPERTS // experts_per_group),
        in_specs=[
            pl.BlockSpec((seq, D_MODEL), lambda r, g: (r, 0)),
            pl.BlockSpec((seq, LANE), lambda r, g: (r, 0)),
            slots,
            pl.BlockSpec((1, MOD_ROWS, D_MODEL), lambda r, g: (r if per_request else 0, 0, 0)),
            pl.BlockSpec((1, D_MODEL), lambda r, g: (0, 0)),
        ],
        out_specs=pl.BlockSpec((seq, D_MODEL), lambda r, g: (r, 0)),
        out_shape=jax.ShapeDtypeStruct((batch * seq, D_MODEL), F32),
        scratch_shapes=[pltpu.VMEM((seq, D_MODEL), F32)],
        compiler_params=_cparams("parallel", "arbitrary"),
    )(x2d, rankc, ye, mod, final_g.reshape(1, D_MODEL))


def _reorder_w_in(w_in):
    a_main = w_in[:, :, 0:2048]
    a_gate = w_in[:, :, 2048:2064]
    b_main = w_in[:, :, 2064:3600]
    b_gate = w_in[:, :, 3600:3616]
    c_main = w_in[:, :, 3616:4384]
    merge = w_in[:, :, 4384:7456]

    def per_head(g, heads):
        g = g.reshape(DEPTH, D_MODEL, 4, heads).transpose(0, 1, 3, 2)
        g = jnp.pad(g, ((0, 0), (0, 0), (0, 0), (0, GATE_STRIDE - 4)))
        return g.reshape(DEPTH, D_MODEL, heads * GATE_STRIDE)

    width_q = HEADS_C * HEAD_DIM_C
    qc = c_main[:, :, :width_q].reshape(DEPTH, D_MODEL, HEADS_C, HEAD_DIM_C)
    qc = jnp.stack([qc[:, :, h] for h in HEAD_ORDER_C], axis=2).reshape(DEPTH, D_MODEL, width_q)
    c_main = jnp.concatenate([qc, c_main[:, :, width_q:]], axis=-1)
    gates = jnp.concatenate([per_head(a_gate, HEADS_A), per_head(b_gate, HEADS_B)], axis=-1)
    pad = jnp.zeros((DEPTH, D_MODEL, N_PROJ - T_GATES * LANE - gates.shape[-1]), w_in.dtype)
    gates_lo = _bf(gates - _bf(gates).astype(F32))
    gates_lo = jnp.pad(gates_lo, ((0, 0), (0, 0), (0, LANE - gates.shape[-1])))
    return _bf(jnp.concatenate([merge, a_main, b_main, c_main, gates, pad], axis=-1)), gates_lo


def _gate_lane_rows(delta_kinds, mlstm_kinds):
    def block(kinds, heads):
        zero = jnp.zeros((DEPTH, heads), F32)
        cols = [zero if k is None else k.astype(F32) for k in kinds] + [zero] * (GATE_STRIDE - len(kinds))
        return jnp.stack(cols, axis=-1).reshape(DEPTH, heads * GATE_STRIDE)

    used = jnp.concatenate([block(delta_kinds, HEADS_A), block(mlstm_kinds, HEADS_B)], axis=-1)
    return jnp.pad(used, ((0, 0), (0, LANE - used.shape[-1])))[:, None, :]


def kernel(x_prompt, x_sample, cache_attn_k, cache_attn_v, state_delta, state_mlstm_c, state_mlstm_n, state_mlstm_m, c, c_ctx, ada_w, ada_b, norm1_g, norm2_g, w_in, conv_qkv_a, delta_a_log, delta_dt_bias, delta_norm_g, mlstm_i_bias, mlstm_f_bias, mlstm_norm_g, attn_sink, w_branch_a, w_branch_b, w_branch_c, w_out, w_router, w_expert_gate, w_expert_up, w_expert_down, final_norm_g):
    batch_p, seq_p, _ = x_prompt.shape
    batch_s, seq_s, _ = x_sample.shape
    past = cache_attn_k.shape[2]

    cond = jnp.concatenate([c_ctx[None, :], c, jnp.zeros((COND_ROWS - 1 - batch_s, D_MODEL), F32)], axis=0)
    mod = _modulation(cond, ada_w, ada_b).reshape(DEPTH, COND_ROWS, ADA_CHUNKS, D_MODEL)
    mod = jnp.pad(mod, ((0, 0), (0, 0), (0, MOD_ROWS - ADA_CHUNKS), (0, 0)))

    w_in_r, w_gates_lo = _reorder_w_in(w_in)
    wa, wb, wo = _bf(w_branch_a), _bf(w_branch_b), _bf(w_out)
    wc = w_branch_c.reshape(DEPTH, HEADS_C, HEAD_DIM_C, D_MODEL)
    wc = _bf(jnp.stack([wc[:, h] for h in HEAD_ORDER_C], axis=1).reshape(DEPTH, HEADS_C * HEAD_DIM_C, D_MODEL))
    wr = _bf(jnp.pad(w_router, ((0, 0), (0, 0), (0, LANE - N_EXPERTS))))
    gate_bias = _gate_lane_rows([None, None, delta_dt_bias[:, 0], delta_dt_bias[:, 1]],
                                [mlstm_i_bias[:, 0], mlstm_i_bias[:, 1], mlstm_f_bias[:, 0], mlstm_f_bias[:, 1]])
    gate_a_log = _gate_lane_rows([None, None, delta_a_log[:, 0], delta_a_log[:, 1]], [])
    sink = jnp.broadcast_to(attn_sink[:, :, None], (DEPTH, HEADS_C, LANE)).astype(F32)
    cache_k = cache_attn_k.reshape(batch_s, DEPTH, past, KV_HEADS_C * HEAD_DIM_C)
    cache_v = cache_attn_v.reshape(batch_s, DEPTH, past, KV_HEADS_C * HEAD_DIM_C)
    state_n = state_mlstm_n.reshape(batch_s, DEPTH, 2, HEADS_B, 1, DK_B)
    state_m = jnp.broadcast_to(state_mlstm_m[..., None, None], (batch_s, DEPTH, 2, HEADS_B, 1, LANE)).astype(F32)
    cos, sin = _rope_tables(seq_s)

    def mix_and_route(x2d, l, batch, seq, latent):
        mod_l = mod[l, 1:1 + batch] if latent else mod[l, 0:1]
        cap = EC_CAPACITY * seq // N_EXPERTS
        proj, scan_gates = _in_proj(x2d, mod_l, norm1_g, w_in_r, w_gates_lo, gate_bias, gate_a_log, l, seq, latent)
        ya, d_new = _delta_mixer(proj, scan_gates, conv_qkv_a, delta_norm_g, l, batch, seq,
                                 state_delta if latent else None)
        yb, b_new = _mlstm_mixer(proj, scan_gates, mlstm_norm_g, l, batch, seq,
                                 (state_mlstm_c, state_n, state_m) if latent else None)
        if latent:
            yc = _latent_attention(proj, cache_k, cache_v, cos, sin, sink, l, batch, seq)
        else:
            yc = _ctx_attention(proj, sink, l, batch, seq)
        x1, h2, aff, afft = _mix(x2d, ya, yb, yc, proj, mod_l, norm2_g, wa, wb, wc, wo, wr, l, seq, latent)
        xs, gate, rankc = _gather(aff, afft, h2, batch, seq, cap)
        return dict(x1=x1, rankc=rankc, mod=mod_l, cap=cap, proj=proj, delta=d_new, mlstm=b_new,
                    xs=xs.reshape(N_EXPERTS, batch * cap, D_MODEL), gate=gate.reshape(N_EXPERTS, batch * cap, 1))

    def add_experts(routed, ye, l, batch, seq, latent):
        return _scatter(routed["x1"], routed["rankc"], ye.reshape(N_EXPERTS, batch, routed["cap"], D_MODEL),
                        routed["mod"], final_norm_g, batch, seq, routed["cap"], latent, l == DEPTH - 1)

    xp = x_prompt.reshape(batch_p * seq_p, D_MODEL)
    xs = x_sample.reshape(batch_s * seq_s, D_MODEL)
    ks, vs, ds, cs, ns, ms = [], [], [], [], [], []
    for l in range(DEPTH):
        rp = mix_and_route(xp, l, batch_p, seq_p, False)
        rs = mix_and_route(xs, l, batch_s, seq_s, True)
        ye_p, ye_s = _expert_ffn([rp["xs"], rs["xs"]], [rp["gate"], rs["gate"]],
                                 w_expert_gate, w_expert_up, w_expert_down, l)
        xp = add_experts(rp, ye_p, l, batch_p, seq_p, False)
        xs = add_experts(rs, ye_s, l, batch_s, seq_s, True)
        kv = rp["proj"][:, T_KC * LANE:(T_VC + 1) * LANE].astype(F32)
        ks.append(kv[:, :LANE].reshape(batch_p, seq_p, KV_HEADS_C, HEAD_DIM_C))
        vs.append(kv[:, LANE:].reshape(batch_p, seq_p, KV_HEADS_C, HEAD_DIM_C))
        ds.append(rp["delta"])
        c_new, n_new, m_new = rp["mlstm"]
        cs.append(c_new)
        ns.append(n_new)
        ms.append(m_new)

    stack = lambda parts: jnp.stack(parts, axis=1)
    return (xp.reshape(batch_p, seq_p, D_MODEL), xs.reshape(batch_s, seq_s, D_MODEL),
            stack(ks), stack(vs), stack(ds), stack(cs), stack(ns), stack(ms))
```

```python
import functools

import jax
import jax.numpy as jnp
from jax import lax
from jax.experimental import pallas as pl
from jax.experimental.pallas import tpu as pltpu

F32 = jnp.float32
BF16 = jnp.bfloat16

D_MODEL = 1024
DEPTH = 2
GRID_W = 64
EPS = 1e-6
HEADS_A = 4
DK_A = 128
DV_A = 128
CONV_K = 5
CHUNK = 64
HEADS_B = 4
DK_B = 64
DV_B = 128
HEADS_C = 8
KV_HEADS_C = 2
HEAD_DIM_C = 64
GROUP_C = HEADS_C // KV_HEADS_C
WINDOW = 128
Q_BLOCK = 128
ROPE_BASE = 10000.0
N_EXPERTS = 16
D_EXPERT = 512
EC_CAPACITY = 2
ADA_CHUNKS = 6

LANE = 128
MOD_ROWS = 8
COND_ROWS = 16

T_GA, T_GB, T_GC = 0, 8, 16
T_QA, T_KA, T_VA, T_ZA = 24, 28, 32, 36
T_QB, T_KB, T_VB, T_OB = 40, 42, 44, 48
T_QC, T_KC, T_VC = 52, 56, 57
T_GATES = 58
N_TILES = 60
N_PROJ = N_TILES * LANE
GATE_STRIDE = 8
GATE_B_OFF = HEADS_A * GATE_STRIDE

VMEM_LIMIT = 48 * 1024 * 1024


def _cparams(*sem):
    return pltpu.CompilerParams(dimension_semantics=sem, vmem_limit_bytes=VMEM_LIMIT)


def _bf(x):
    return x.astype(BF16)


def _mm(a, b):
    return jnp.dot(_bf(a), _bf(b), preferred_element_type=F32)


def _mm_nt(a, b):
    return lax.dot_general(_bf(a), _bf(b), (((1,), (1,)), ((), ())), preferred_element_type=F32)


def _mm_tn(a, b):
    return lax.dot_general(_bf(a), _bf(b), (((0,), (0,)), ((), ())), preferred_element_type=F32)


def _split2(x):
    hi = _bf(x)
    return hi, _bf(x - hi.astype(F32))


def _split3(x):
    hi = _bf(x)
    r = x - hi.astype(F32)
    mid = _bf(r)
    return hi, mid, _bf(r - mid.astype(F32))


def _mm_sel(sel, x):
    hi, mid, lo = _split3(x)
    d = functools.partial(jnp.dot, preferred_element_type=F32)
    return (d(sel, hi) + d(sel, mid)) + d(sel, lo)


def _mm_hi(a, b):
    ah, al = _split2(a)
    bh, bl = _split2(b)
    d = functools.partial(jnp.dot, preferred_element_type=F32)
    return d(ah, bh) + (d(ah, bl) + d(al, bh))


def _sigmoid(x):
    return 1.0 / (1.0 + jnp.exp(-x))


def _silu(x):
    return x * _sigmoid(x)


def _softplus(x):
    return jnp.maximum(x, 0.0) + jnp.log(1.0 + jnp.exp(-jnp.abs(x)))


def _rms(x, g):
    return x * lax.rsqrt(jnp.mean(x * x, axis=-1, keepdims=True) + EPS) * g


def _chunk_masks(backward):
    ri = lax.broadcasted_iota(jnp.int32, (CHUNK, CHUNK), 0)
    ci = lax.broadcasted_iota(jnp.int32, (CHUNK, CHUNK), 1)
    if backward:
        return ri <= ci, ri < ci, ri == ci
    return ri >= ci, ri > ci, ri == ci


def _mod_kernel(c_ref, w_ref, b_ref, o_ref):
    o_ref[0] = _mm(_silu(c_ref[...]), w_ref[0]) + b_ref[0]


def _modulation(cond, ada_w, ada_b):
    n_out = ADA_CHUNKS * D_MODEL
    tn = 512
    return pl.pallas_call(
        _mod_kernel,
        grid=(DEPTH, n_out // tn),
        in_specs=[
            pl.BlockSpec((COND_ROWS, D_MODEL), lambda l, j: (0, 0)),
            pl.BlockSpec((1, D_MODEL, tn), lambda l, j: (l, 0, j)),
            pl.BlockSpec((1, 1, tn), lambda l, j: (l, 0, j)),
        ],
        out_specs=pl.BlockSpec((1, COND_ROWS, tn), lambda l, j: (l, 0, j)),
        out_shape=jax.ShapeDtypeStruct((DEPTH, COND_ROWS, n_out), F32),
        compiler_params=_cparams("parallel", "parallel"),
    )(cond, ada_w, ada_b.reshape(DEPTH, 1, n_out))


def _activate_gates(x, bias, a_log):
    lane = lax.broadcasted_iota(jnp.int32, (1, LANE), 1)
    kind = lane % GATE_STRIDE
    is_delta = lane < GATE_B_OFF
    is_mlstm = (lane >= GATE_B_OFF) & (lane < GATE_B_OFF + HEADS_B * GATE_STRIDE)
    y = x + bias
    log_term = jnp.log(1.0 + jnp.exp(-jnp.abs(y)))
    out = jnp.where(is_delta & (kind < 2), _sigmoid(y), x)
    out = jnp.where(is_delta & (kind >= 2) & (kind < 4), -jnp.exp(a_log) * (jnp.maximum(y, 0.0) + log_term), out)
    out = jnp.where(is_mlstm & (kind < 2), y, out)
    return jnp.where(is_mlstm & (kind >= 2) & (kind < 4), -(jnp.maximum(-y, 0.0) + log_term), out)


def _in_proj_kernel(x_ref, mod_ref, g_ref, w_ref, wlo_ref, gb_ref, ga_ref, o_ref, gt_ref, h_scr, hlo_scr, *,
                    gate_off):
    @pl.when(pl.program_id(1) == 0)
    def _():
        m = mod_ref[0]
        hi, lo = _split2(_rms(x_ref[...], g_ref[0]) * (1.0 + m[1:2]) + m[0:1])
        h_scr[...] = hi
        hlo_scr[...] = lo

    res = jnp.dot(h_scr[...], w_ref[0], preferred_element_type=F32)
    o_ref[...] = _bf(res)

    @pl.when(pl.program_id(1) == pl.num_programs(1) - 1)
    def _():
        lanes = slice(gate_off, gate_off + LANE)
        d = functools.partial(jnp.dot, preferred_element_type=F32)
        logits = res[:, lanes] + (d(hlo_scr[...], w_ref[0][:, lanes]) + d(h_scr[...], wlo_ref[0]))
        gt_ref[...] = _activate_gates(logits, gb_ref[0], ga_ref[0])


def _in_proj(x2d, mod, norm_g, w_in, w_gates_lo, gate_bias, gate_a_log, layer, seq, per_request):
    m_rows = x2d.shape[0]
    tm = min(1024, m_rows)
    tn = 1280
    if per_request:
        assert seq % tm == 0
    gate_off = T_GATES * LANE - (N_PROJ // tn - 1) * tn
    assert 0 <= gate_off and gate_off + LANE <= tn
    lane_row = pl.BlockSpec((1, 1, LANE), lambda i, j: (layer, 0, 0))
    return pl.pallas_call(
        functools.partial(_in_proj_kernel, gate_off=gate_off),
        grid=(m_rows // tm, N_PROJ // tn),
        in_specs=[
            pl.BlockSpec((tm, D_MODEL), lambda i, j: (i, 0)),
            pl.BlockSpec((1, MOD_ROWS, D_MODEL), lambda i, j: ((i * tm) // seq if per_request else 0, 0, 0)),
            pl.BlockSpec((1, 1, D_MODEL), lambda i, j: (layer, 0, 0)),
            pl.BlockSpec((1, D_MODEL, tn), lambda i, j: (layer, 0, j)),
            pl.BlockSpec((1, D_MODEL, LANE), lambda i, j: (layer, 0, 0)),
            lane_row, lane_row,
        ],
        out_specs=[pl.BlockSpec((tm, tn), lambda i, j: (i, j)), pl.BlockSpec((tm, LANE), lambda i, j: (i, 0))],
        out_shape=[jax.ShapeDtypeStruct((m_rows, N_PROJ), BF16), jax.ShapeDtypeStruct((m_rows, LANE), F32)],
        scratch_shapes=[pltpu.VMEM((tm, D_MODEL), BF16), pltpu.VMEM((tm, D_MODEL), BF16)],
        compiler_params=_cparams("parallel", "arbitrary"),
    )(x2d, mod, norm_g.reshape(DEPTH, 1, D_MODEL), w_in, w_gates_lo, gate_bias, gate_a_log)


PREP_PROBLEMS = 16
CONV_PAD = 8
DELTA_HEAD_TOKENS = 2048
INV_GROUP = 32


def _gate_dense(gt_parts, lane_index):
    row = lax.broadcasted_iota(jnp.int32, (LANE, LANE), 0)
    sel = jnp.where(row == lane_index, 1.0, 0.0).astype(BF16)
    hi, mid, lo = gt_parts
    d = functools.partial(jnp.dot, preferred_element_type=F32)
    return (d(hi, sel) + d(mid, sel)) + d(lo, sel)


def _delta_kernel(q_ref, k_ref, v_ref, z_ref, gt_ref, cq_ref, ck_ref, cv_ref, ng_ref, *rest,
                  seq, heads, has_init, emit_state):
    rest = list(rest)
    s0_ref = rest.pop(0) if has_init else None
    y_ref = rest.pop(0)
    st_ref = rest.pop(0) if emit_state else None
    qs, ks, vs, gates, s_scr, o_scr, u_scr, wq_scr, akd_scr, dk_scr, pw_scr, inv_scr, rhs_scr, pad_scr = rest
    head_group = pl.program_id(1)
    n_chunks = seq // CHUNK
    n_chain = 2 * heads
    head_lanes = lambda hd: slice(hd * DK_A, (hd + 1) * DK_A)

    pad_scr[0:CONV_PAD, :] = jnp.zeros((CONV_PAD, pad_scr.shape[1]), F32)
    pad_scr[CONV_PAD + seq:, :] = jnp.zeros((CONV_PAD, pad_scr.shape[1]), F32)

    def conv_silu(x_ref, w_ref):
        w = w_ref[0]
        pad_scr[CONV_PAD:CONV_PAD + seq, :] = x_ref[...].astype(F32)
        acc = None
        for j in range(CONV_K):
            first = CONV_PAD + j - CONV_K // 2
            term = pad_scr[first:first + seq, :] * w[j:j + 1, :]
            acc = term if acc is None else acc + term
        return _silu(acc)

    def l2n(x):
        parts = [x[:, head_lanes(hd)] for hd in range(heads)]
        return jnp.concatenate(
            [p * lax.rsqrt(jnp.sum(p * p, axis=-1, keepdims=True) + EPS) for p in parts], axis=1)

    qs[...] = l2n(conv_silu(q_ref, cq_ref)) * (DK_A ** -0.5)
    ks[...] = l2n(conv_silu(k_ref, ck_ref))
    vs[...] = conv_silu(v_ref, cv_ref)

    gt_parts = _split3(gt_ref[...])
    for hd in range(heads):
        base = (head_group * heads + hd) * GATE_STRIDE
        for kind in range(4):
            gates[hd * 4 + kind] = _gate_dense(gt_parts, base + kind)
        for d in range(2):
            s_scr[hd * 2 + d] = s0_ref[0, 0, d, hd] if has_init else jnp.zeros((DK_A, DV_A), F32)

    def chunk_rows(c):
        return pl.ds(pl.multiple_of(c * CHUNK, CHUNK), CHUNK)

    n_prob = n_chain * n_chunks
    group = min(INV_GROUP, n_prob)
    prep_chunks = max(1, PREP_PROBLEMS // n_chain)
    ri = lax.broadcasted_iota(jnp.int32, (CHUNK, 2 * CHUNK), 0)
    ci = lax.broadcasted_iota(jnp.int32, (CHUNK, 2 * CHUNK), 1) % CHUNK
    dot = functools.partial(jnp.dot, preferred_element_type=F32)

    def setup_body(i, carry):
        loaded = []
        for cc in range(prep_chunks):
            c = i * prep_chunks + cc
            rows = chunk_rows(c)
            for hd in range(heads):
                loaded.append((c, hd, qs[rows, head_lanes(hd)], ks[rows, head_lanes(hd)], vs[rows, head_lanes(hd)],
                               [(gates[hd * 4 + d, rows, :], gates[hd * 4 + 2 + d, rows, :]) for d in range(2)]))
        products = []
        for c, hd, q, k, v, gate_cols in loaded:
            per_dir = []
            for d in range(2):
                g = gate_cols[d][1]
                incl, strict, _ = _chunk_masks(d == 1)
                rhs = jnp.concatenate(
                    [jnp.where(strict, g[:, :CHUNK], 0.0), jnp.zeros((CHUNK, LANE - CHUNK), F32), g], axis=1)
                per_dir.append(_mm_sel(jnp.where(incl, 1.0, 0.0).astype(BF16), rhs))
            gram = _mm_nt(jnp.concatenate([k, q], axis=0), k)
            products.append((gram[:CHUNK], gram[CHUNK:], per_dir))
        results = []
        pairs = []
        for (c, hd, q, k, v, gate_cols), (kk, qk, per_dir) in zip(loaded, products):
            lows = []
            for d in range(2):
                bt, g = gate_cols[d]
                incl, strict, _ = _chunk_masks(d == 1)
                cs = per_dir[d]
                gc = cs[:, LANE:]
                g_last = gc[0:1, :] if d == 1 else gc[CHUNK - 1:CHUNK, :]
                egc = jnp.exp(gc)
                dec = jnp.where(incl, jnp.exp(cs[:, :CHUNK]), 0.0)
                lows.append(jnp.where(strict, bt[:, :CHUNK] * kk * dec, 0.0))
                rhs2 = jnp.concatenate([v * bt, k * (bt * egc)], axis=1)
                akd = _bf(jnp.concatenate([qk * dec, (k * jnp.exp(g_last - gc)).T], axis=0))
                results.append(((hd * n_chunks + c) * 2 + d, rhs2, akd, _bf(q * egc), jnp.exp(g_last)))
            pairs.append((hd * n_chunks + c, jnp.concatenate(lows, axis=1)))
        for p, rhs2, akd, qd, decay in results:
            rhs_scr[p] = rhs2
            akd_scr[p] = akd
            wq_scr[p, CHUNK:, :] = qd
            dk_scr[p] = decay
        for pp, low in pairs:
            pw_scr[pp] = low
            inv_scr[pp] = jnp.where(ri == ci, 1.0, 0.0) - jnp.where((ri // 2) == (ci // 2), low, 0.0)
        return carry

    lax.fori_loop(0, n_chunks // prep_chunks, setup_body, 0)

    lane_pair = lax.broadcasted_iota(jnp.int32, (CHUNK, 2 * CHUNK), 1)

    def block_diagonal(y):
        zero = jnp.zeros_like(y)
        return jnp.concatenate(
            [jnp.where(lane_pair < CHUNK, y, zero), jnp.where(lane_pair < CHUNK, zero, y)], axis=0)

    def pair_product(x, y):
        xh, xl = _split2(x)
        both = dot(jnp.concatenate([xh, xl], axis=0), block_diagonal(_bf(y)))
        return both[:CHUNK] + both[CHUNK:]

    n_pairs = n_prob // 2
    pair_group = min(INV_GROUP // 2, n_pairs)

    def doubling_pass(size):
        joins = ((ri // (2 * size)) == (ci // (2 * size))) & ((ri // size) != (ci // size))

        def body(i, carry):
            loaded = []
            for j in range(pair_group):
                pp = i * pair_group + j
                loaded.append((pp, pw_scr[pp], inv_scr[pp]))
            partial = [(pp, inv, pair_product(inv, jnp.where(joins, low, 0.0))) for pp, low, inv in loaded]
            results = [(pp, inv - pair_product(t, inv)) for pp, inv, t in partial]
            for pp, new_inv in results:
                inv_scr[pp] = new_inv
            return carry

        lax.fori_loop(0, n_pairs // pair_group, body, 0)

    size = 2
    while size < CHUNK:
        doubling_pass(size)
        size *= 2

    def solve_body(i, carry):
        loaded = []
        for j in range(pair_group):
            pp = i * pair_group + j
            inv = inv_scr[pp]
            for d in range(2):
                loaded.append((pp * 2 + d, inv[:, d * CHUNK:(d + 1) * CHUNK], rhs_scr[pp * 2 + d]))
        results = []
        for p, inv, rhs2 in loaded:
            ih, il = _split2(inv)
            rh, rl = _split2(rhs2)
            by_hi = dot(jnp.concatenate([ih, il], axis=0), rh)
            results.append((p, by_hi[:CHUNK] + (dot(ih, rl) + by_hi[CHUNK:])))
        for p, sol in results:
            u_scr[p] = sol[:, :DV_A]
            wq_scr[p, :CHUNK, :] = _bf(sol[:, DV_A:])
        return carry

    lax.fori_loop(0, n_pairs // pair_group, solve_body, 0)

    def scan_body(i, carry):
        loaded = []
        for chain in range(n_chain):
            c = n_chunks - 1 - i if chain % 2 == 1 else i
            p = ((chain // 2) * n_chunks + c) * 2 + chain % 2
            loaded.append((c, s_scr[chain], u_scr[p], wq_scr[p], akd_scr[p], dk_scr[p]))
        first = [dot(wq, _bf(state)) for c, state, u, wq, akd, decay in loaded]
        second = [dot(akd, _bf(u - ws[:CHUNK])) for (c, state, u, wq, akd, decay), ws in zip(loaded, first)]
        results = [(c, ws[CHUNK:] + av[:CHUNK], decay * state + av[CHUNK:])
                   for (c, state, u, wq, akd, decay), ws, av in zip(loaded, first, second)]
        for chain, (c, o, state) in enumerate(results):
            o_scr[chain, chunk_rows(c), :] = o
            s_scr[chain] = state
        return carry

    lax.fori_loop(0, n_chunks, scan_body, 0)

    z = z_ref[...].astype(F32)
    for hd in range(heads):
        y_ref[:, head_lanes(hd)] = _bf(_rms(o_scr[hd * 2] + o_scr[hd * 2 + 1], ng_ref[0])
                                       * _silu(z[:, head_lanes(hd)]))
        if emit_state:
            for d in range(2):
                st_ref[0, d, hd] = s_scr[hd * 2 + d]


def _delta_mixer(proj, scan_gates, conv_w, norm_g, layer, batch, seq, state0):
    has_init = state0 is not None
    emit_state = not has_init
    heads = next(h for h in (4, 2, 1) if HEADS_A % h == 0 and h * seq <= max(DELTA_HEAD_TOKENS, seq))
    n_groups = HEADS_A // heads
    width = heads * DK_A
    col = lambda tile: (lambda r, g: (r, tile // heads + g))
    cw = lambda part: (lambda r, g: (layer, 0, part * n_groups + g))
    in_specs = [
        pl.BlockSpec((seq, width), col(T_QA)),
        pl.BlockSpec((seq, width), col(T_KA)),
        pl.BlockSpec((seq, width), col(T_VA)),
        pl.BlockSpec((seq, width), col(T_ZA)),
        pl.BlockSpec((seq, LANE), lambda r, g: (r, 0)),
        pl.BlockSpec((1, CONV_K, width), cw(0)),
        pl.BlockSpec((1, CONV_K, width), cw(1)),
        pl.BlockSpec((1, CONV_K, width), cw(2)),
        pl.BlockSpec((1, 1, DV_A), lambda r, g: (layer, 0, 0)),
    ]
    args = [proj, proj, proj, proj, scan_gates, conv_w, conv_w, conv_w, norm_g.reshape(DEPTH, 1, DV_A)]
    if has_init:
        in_specs.append(pl.BlockSpec((1, 1, 2, heads, DK_A, DV_A), lambda r, g: (r, layer, 0, g, 0, 0)))
        args.append(state0)
    out_specs = [pl.BlockSpec((seq, width), lambda r, g: (r, g))]
    out_shape = [jax.ShapeDtypeStruct((batch * seq, HEADS_A * DV_A), BF16)]
    if emit_state:
        out_specs.append(pl.BlockSpec((1, 2, heads, DK_A, DV_A), lambda r, g: (r, 0, g, 0, 0)))
        out_shape.append(jax.ShapeDtypeStruct((batch, 2, HEADS_A, DK_A, DV_A), F32))
    n_chain = 2 * heads
    n_prob = n_chain * (seq // CHUNK)
    outs = pl.pallas_call(
        functools.partial(_delta_kernel, seq=seq, heads=heads, has_init=has_init, emit_state=emit_state),
        grid=(batch, n_groups),
        in_specs=in_specs,
        out_specs=out_specs,
        out_shape=out_shape,
        scratch_shapes=[
            pltpu.VMEM((seq, width), F32), pltpu.VMEM((seq, width), F32), pltpu.VMEM((seq, width), F32),
            pltpu.VMEM((2 * n_chain, seq, LANE), F32), pltpu.VMEM((n_chain, DK_A, DV_A), F32),
            pltpu.VMEM((n_chain, seq, DV_A), F32),
            pltpu.VMEM((n_prob, CHUNK, DV_A), F32), pltpu.VMEM((n_prob, 2 * CHUNK, DK_A), BF16),
            pltpu.VMEM((n_prob, CHUNK + DK_A, CHUNK), BF16), pltpu.VMEM((n_prob, 1, DV_A), F32),
            pltpu.VMEM((n_prob // 2, CHUNK, 2 * CHUNK), F32), pltpu.VMEM((n_prob // 2, CHUNK, 2 * CHUNK), F32),
            pltpu.VMEM((n_prob, CHUNK, DV_A + DK_A), F32),
            pltpu.VMEM((seq + 2 * CONV_PAD, width), F32),
        ],
        compiler_params=_cparams("parallel", "parallel"),
    )(*args)
    return (outs[0], outs[1]) if emit_state else (outs[0], None)


HEADS_PER_STEP_B = 4


def _mlstm_kernel(q_ref, k_ref, v_ref, og_ref, gt_ref, ng_ref, *rest, seq, has_init, emit_state):
    rest = list(rest)
    if has_init:
        c0_ref, n0_ref, m0_ref = rest[:3]
        rest = rest[3:]
    y_ref = rest.pop(0)
    if emit_state:
        co_ref, no_ref, mo_ref = rest[:3]
        rest = rest[3:]
    gates, c_scr, n_scr, m_scr, h_scr = rest
    pair = pl.program_id(1)
    n_chunks = seq // CHUNK
    gt_parts = _split3(gt_ref[...])

    for j in range(HEADS_PER_STEP_B):
        base = GATE_B_OFF + (pair * HEADS_PER_STEP_B + j) * GATE_STRIDE
        for kind in range(4):
            gates[j * 4 + kind] = _gate_dense(gt_parts, base + kind)
        for d in range(2):
            idx = j * 2 + d
            if has_init:
                c_scr[idx] = c0_ref[0, 0, d, j]
                n_scr[idx] = n0_ref[0, 0, d, j]
                m_scr[idx] = m0_ref[0, 0, d, j]
            else:
                c_scr[idx] = jnp.zeros((DK_B, DV_B), F32)
                n_scr[idx] = jnp.zeros((1, DK_B), F32)
                m_scr[idx] = jnp.zeros((1, LANE), F32)

    n_chain = 2 * HEADS_PER_STEP_B

    def scan_body(i, carry):
        loaded = []
        for chain in range(n_chain):
            j, d = chain // 2, chain % 2
            c = n_chunks - 1 - i if d == 1 else i
            rows = pl.ds(pl.multiple_of(c * CHUNK, CHUNK), CHUNK)
            loaded.append((chain, rows, q_ref[rows, j * DK_B:(j + 1) * DK_B].astype(F32),
                           k_ref[rows, j * DK_B:(j + 1) * DK_B].astype(F32) * (DK_B ** -0.5),
                           v_ref[rows, j * DV_B:(j + 1) * DV_B],
                           gates[j * 4 + d, rows, :], gates[j * 4 + 2 + d, rows, :],
                           c_scr[chain], n_scr[chain], m_scr[chain]))
        stage1 = []
        for chain, rows, q, k, v, ig, lf, cmat, nvec, m_prev in loaded:
            incl, strict, diag = _chunk_masks(chain % 2 == 1)
            rhs = jnp.concatenate(
                [jnp.where(strict, lf[:, :CHUNK], 0.0) + jnp.where(diag, ig[:, :CHUNK], 0.0),
                 jnp.zeros((CHUNK, LANE - CHUNK), F32), lf], axis=1)
            stage1.append((_mm_sel(jnp.where(incl, 1.0, 0.0).astype(BF16), rhs), _mm_nt(q, k), _mm(q, cmat)))
        stage2 = []
        for (chain, rows, q, k, v, ig, lf, cmat, nvec, m_prev), (cs, qk, qc) in zip(loaded, stage1):
            incl, _, _ = _chunk_masks(chain % 2 == 1)
            bc = cs[:, LANE:]
            b_last = bc[0:1, :] if chain % 2 == 1 else bc[CHUNK - 1:CHUNK, :]
            d_log = jnp.where(incl, cs[:, :CHUNK], -jnp.inf)
            d_max = jnp.max(d_log, axis=1, keepdims=True)
            tok = b_last - bc + ig
            m_new = jnp.maximum(b_last + m_prev, jnp.max(tok, axis=0, keepdims=True))
            w_prev = jnp.exp(b_last + m_prev - m_new)
            kw = k * jnp.exp(tok - m_new)[:, :DK_B]
            stage2.append((bc, d_log, d_max, m_new, w_prev, kw))
        stage3 = []
        for (chain, rows, q, k, v, ig, lf, cmat, nvec, m_prev), (cs, qk, qc), (bc, d_log, d_max, m_new, w_prev, kw) \
                in zip(loaded, stage1, stage2):
            m_t = jnp.maximum(bc + m_prev, d_max)
            w_inter = jnp.exp(bc + m_prev - m_t)
            pm = jnp.exp(d_log - m_t[:, :CHUNK]) * qk
            den = jnp.sum(w_inter[:, :DK_B] * (q * nvec) + pm, axis=1, keepdims=True)
            stage3.append((m_t, w_inter, den, _mm(pm, v), _mm_tn(kw, v)))
        results = []
        for (chain, rows, q, k, v, ig, lf, cmat, nvec, m_prev), (cs, qk, qc), (bc, d_log, d_max, m_new, w_prev, kw), \
                (m_t, w_inter, den, pv, inc) in zip(loaded, stage1, stage2, stage3):
            results.append((chain, rows, (w_inter * qc + pv) / jnp.maximum(jnp.abs(den), jnp.exp(-m_t)),
                            w_prev * cmat + inc,
                            w_prev[:, :DK_B] * nvec + jnp.sum(kw, axis=0, keepdims=True), m_new))
        for chain, rows, h, cmat, nvec, m_new in results:
            h_scr[chain, rows, :] = h
            c_scr[chain] = cmat
            n_scr[chain] = nvec
            m_scr[chain] = m_new
        return carry

    lax.fori_loop(0, n_chunks, scan_body, 0)

    og = og_ref[...].astype(F32)
    for j in range(HEADS_PER_STEP_B):
        h = h_scr[j * 2] + h_scr[j * 2 + 1]
        y_ref[:, j * DV_B:(j + 1) * DV_B] = _bf(_rms(h, ng_ref[0]) * _sigmoid(og[:, j * DV_B:(j + 1) * DV_B]))
        if emit_state:
            for d in range(2):
                co_ref[0, d, j] = c_scr[j * 2 + d]
                no_ref[0, d, j] = n_scr[j * 2 + d]
                mo_ref[0, d, j] = m_scr[j * 2 + d]


def _mlstm_mixer(proj, scan_gates, norm_g, layer, batch, seq, state0):
    has_init = state0 is not None
    emit_state = not has_init
    hp = HEADS_PER_STEP_B
    n_pairs = HEADS_B // hp
    in_specs = [
        pl.BlockSpec((seq, hp * DK_B), lambda r, p: (r, T_QB * LANE // (hp * DK_B) + p)),
        pl.BlockSpec((seq, hp * DK_B), lambda r, p: (r, T_KB * LANE // (hp * DK_B) + p)),
        pl.BlockSpec((seq, hp * DV_B), lambda r, p: (r, T_VB * LANE // (hp * DV_B) + p)),
        pl.BlockSpec((seq, hp * DV_B), lambda r, p: (r, T_OB * LANE // (hp * DV_B) + p)),
        pl.BlockSpec((seq, LANE), lambda r, p: (r, 0)),
        pl.BlockSpec((1, 1, DV_B), lambda r, p: (layer, 0, 0)),
    ]
    args = [proj, proj, proj, proj, scan_gates, norm_g.reshape(DEPTH, 1, DV_B)]
    if has_init:
        c0, n0, m0 = state0
        in_specs += [
            pl.BlockSpec((1, 1, 2, hp, DK_B, DV_B), lambda r, p: (r, layer, 0, p, 0, 0)),
            pl.BlockSpec((1, 1, 2, hp, 1, DK_B), lambda r, p: (r, layer, 0, p, 0, 0)),
            pl.BlockSpec((1, 1, 2, hp, 1, LANE), lambda r, p: (r, layer, 0, p, 0, 0)),
        ]
        args += [c0, n0, m0]
    out_specs = [pl.BlockSpec((seq, hp * DV_B), lambda r, p: (r, p))]
    out_shape = [jax.ShapeDtypeStruct((batch * seq, HEADS_B * DV_B), BF16)]
    if emit_state:
        out_specs += [
            pl.BlockSpec((1, 2, hp, DK_B, DV_B), lambda r, p: (r, 0, p, 0, 0)),
            pl.BlockSpec((1, 2, hp, 1, DK_B), lambda r, p: (r, 0, p, 0, 0)),
            pl.BlockSpec((1, 2, hp, 1, LANE), lambda r, p: (r, 0, p, 0, 0)),
        ]
        out_shape += [
            jax.ShapeDtypeStruct((batch, 2, HEADS_B, DK_B, DV_B), F32),
            jax.ShapeDtypeStruct((batch, 2, HEADS_B, 1, DK_B), F32),
            jax.ShapeDtypeStruct((batch, 2, HEADS_B, 1, LANE), F32),
        ]
    outs = pl.pallas_call(
        functools.partial(_mlstm_kernel, seq=seq, has_init=has_init, emit_state=emit_state),
        grid=(batch, n_pairs),
        in_specs=in_specs,
        out_specs=out_specs,
        out_shape=out_shape,
        scratch_shapes=[
            pltpu.VMEM((4 * hp, seq, LANE), F32), pltpu.VMEM((2 * hp, DK_B, DV_B), F32),
            pltpu.VMEM((2 * hp, 1, DK_B), F32), pltpu.VMEM((2 * hp, 1, LANE), F32),
            pltpu.VMEM((2 * hp, seq, DV_B), F32),
        ],
        compiler_params=_cparams("parallel", "parallel"),
    )(*args)
    if emit_state:
        return outs[0], (outs[1], outs[2][:, :, :, 0, :], outs[3][:, :, :, 0, 0])
    return outs[0], None


Q_SLABS_C = HEADS_C * HEAD_DIM_C // LANE
HEAD_ORDER_C = [h for s in range(Q_SLABS_C) for h in (s, s + GROUP_C)]


def _attend(q_slabs, segments, sink):
    lane = lax.broadcasted_iota(jnp.int32, (1, LANE), 1)
    scale = HEAD_DIM_C ** -0.5
    nt = (((1,), (1,)), ((), ()))
    operands = []
    for kv in range(KV_HEADS_C):
        mine = (lane >= kv * HEAD_DIM_C) & (lane < (kv + 1) * HEAD_DIM_C)
        operands.append([(_bf(jnp.where(mine, k, 0.0)), _bf(jnp.where(mine, v, 0.0)), valid)
                         for k, v, valid in segments])
    problems = [(s, kv) for s in range(len(q_slabs)) for kv in range(KV_HEADS_C)]
    q_bf = [_bf(q) for q in q_slabs]
    scores = [[lax.dot_general(q_bf[s], kb, nt, preferred_element_type=F32) * scale for kb, _, _ in operands[kv]]
              for s, kv in problems]
    weights = []
    for (s, kv), per_seg in zip(problems, scores):
        per_seg = [x if valid is None else jnp.where(valid, x, -jnp.inf)
                   for x, (_, _, valid) in zip(per_seg, operands[kv])]
        head = kv * GROUP_C + s
        tiles = [x[:, t * LANE:(t + 1) * LANE] for x in per_seg for t in range(x.shape[1] // LANE)]
        m = jnp.maximum(jnp.max(functools.reduce(jnp.maximum, tiles), axis=1, keepdims=True),
                        sink[head:head + 1, 0:1])
        es = [jnp.exp(x - m) for x in per_seg]
        e_tiles = [e[:, t * LANE:(t + 1) * LANE] for e in es for t in range(e.shape[1] // LANE)]
        den = (jnp.sum(functools.reduce(jnp.add, e_tiles), axis=1, keepdims=True)
               + jnp.exp(sink[head:head + 1, 0:1] - m))
        weights.append((es, den))
    outs = []
    for (s, kv), (es, den) in zip(problems, weights):
        acc = None
        for e, (_, vb, _) in zip(es, operands[kv]):
            part = jnp.dot(_bf(e), vb, preferred_element_type=F32)
            acc = part if acc is None else acc + part
        outs.append(acc / den)
    return [sum(outs[s * KV_HEADS_C + 1:(s + 1) * KV_HEADS_C], outs[s * KV_HEADS_C]) for s in range(len(q_slabs))]


def _ctx_attn_kernel(q_ref, k_ref, v_ref, sink_ref, o_ref, *, seq):
    q_slabs = [q_ref[:, s * LANE:(s + 1) * LANE] for s in range(Q_SLABS_C)]
    outs = _attend(q_slabs, [(k_ref[...], v_ref[...], None)], sink_ref[0])
    for s, o in enumerate(outs):
        o_ref[:, s * LANE:(s + 1) * LANE] = _bf(o)


def _ctx_attention(proj, sink, layer, batch, seq):
    width = HEADS_C * HEAD_DIM_C
    return pl.pallas_call(
        functools.partial(_ctx_attn_kernel, seq=seq),
        grid=(batch,),
        in_specs=[
            pl.BlockSpec((seq, width), lambda r: (r, T_QC * LANE // width)),
            pl.BlockSpec((seq, LANE), lambda r: (r, T_KC)),
            pl.BlockSpec((seq, LANE), lambda r: (r, T_VC)),
            pl.BlockSpec((1, HEADS_C, LANE), lambda r: (layer, 0, 0)),
        ],
        out_specs=pl.BlockSpec((seq, width), lambda r: (r, 0)),
        out_shape=jax.ShapeDtypeStruct((batch * seq, width), BF16),
        compiler_params=_cparams("parallel"),
    )(proj, proj, proj, sink)


def _rope(x, cos, sin):
    quarter = HEAD_DIM_C // 4
    lane = lax.broadcasted_iota(jnp.int32, (1, LANE), 1)
    first = (lane % (2 * quarter)) < quarter
    partner = jnp.where(first, -pltpu.roll(x, LANE - quarter, 1), pltpu.roll(x, quarter, 1))
    return x * cos + partner * sin


def _latent_attn_kernel(q_ref, k_ref, v_ref, ck_ref, cv_ref, cq_ref, sq_ref, cos_ref, sin_ref, sink_ref, o_ref, *, seq):
    blk = pl.program_id(1)
    span = Q_BLOCK + 2 * WINDOW
    start = blk * Q_BLOCK
    k_start = pl.multiple_of(jnp.clip(start - WINDOW, 0, seq - span), Q_BLOCK)
    win = pl.ds(k_start, span)
    cq = cq_ref[...]
    sq = sq_ref[...]
    q_slabs = [_rope(q_ref[:, s * LANE:(s + 1) * LANE].astype(F32), cq, sq) for s in range(Q_SLABS_C)]
    k = _rope(k_ref[win, :].astype(F32), cos_ref[win, :], sin_ref[win, :])
    q_pos = start + lax.broadcasted_iota(jnp.int32, (Q_BLOCK, 1), 0)
    k_pos = k_start + lax.broadcasted_iota(jnp.int32, (1, span), 1)
    valid = jnp.abs(q_pos - k_pos) <= WINDOW
    outs = _attend(q_slabs, [(k, v_ref[win, :], valid), (ck_ref[0, 0], cv_ref[0, 0], None)], sink_ref[0])
    for s, o in enumerate(outs):
        o_ref[:, s * LANE:(s + 1) * LANE] = _bf(o)


def _latent_attention(proj, cache_k, cache_v, cos, sin, sink, layer, batch, seq):
    width = HEADS_C * HEAD_DIM_C
    n_blk = seq // Q_BLOCK
    past = cache_k.shape[2]
    return pl.pallas_call(
        functools.partial(_latent_attn_kernel, seq=seq),
        grid=(batch, n_blk),
        in_specs=[
            pl.BlockSpec((Q_BLOCK, width), lambda r, i: (r * n_blk + i, T_QC * LANE // width)),
            pl.BlockSpec((seq, LANE), lambda r, i: (r, T_KC)),
            pl.BlockSpec((seq, LANE), lambda r, i: (r, T_VC)),
            pl.BlockSpec((1, 1, past, LANE), lambda r, i: (r, layer, 0, 0)),
            pl.BlockSpec((1, 1, past, LANE), lambda r, i: (r, layer, 0, 0)),
            pl.BlockSpec((Q_BLOCK, LANE), lambda r, i: (i, 0)),
            pl.BlockSpec((Q_BLOCK, LANE), lambda r, i: (i, 0)),
            pl.BlockSpec((seq, LANE), lambda r, i: (0, 0)),
            pl.BlockSpec((seq, LANE), lambda r, i: (0, 0)),
            pl.BlockSpec((1, HEADS_C, LANE), lambda r, i: (layer, 0, 0)),
        ],
        out_specs=pl.BlockSpec((Q_BLOCK, width), lambda r, i: (r * n_blk + i, 0)),
        out_shape=jax.ShapeDtypeStruct((batch * seq, width), BF16),
        compiler_params=_cparams("parallel", "parallel"),
    )(proj, proj, proj, cache_k, cache_v, cos, sin, cos, sin, sink)


def _rope_tables(seq):
    quarter = HEAD_DIM_C // 4
    pos = jnp.arange(seq)
    row = (pos // GRID_W).astype(F32)
    col = (pos % GRID_W).astype(F32)
    inv = jnp.power(ROPE_BASE, -jnp.arange(quarter, dtype=F32) / quarter)
    ang_row = row[:, None] * inv[None, :]
    ang_col = col[:, None] * inv[None, :]
    ang = jnp.concatenate([ang_row, ang_row, ang_col, ang_col], axis=1)
    ang = jnp.concatenate([ang] * (LANE // HEAD_DIM_C), axis=1)
    return jnp.cos(ang), jnp.sin(ang)


def _mix_kernel(x_ref, ya_ref, yb_ref, yc_ref, ga_ref, gb_ref, gc_ref, mod_ref, n2_ref,
                wa_ref, wb_ref, wc_ref, wo_ref, wr_ref, xo_ref, h_ref, aff_ref, afft_ref):
    m = mod_ref[0]
    mixed = (_sigmoid(ga_ref[...].astype(F32)) * _mm(ya_ref[...], wa_ref[0])
             + _sigmoid(gb_ref[...].astype(F32)) * _mm(yb_ref[...], wb_ref[0])
             + _sigmoid(gc_ref[...].astype(F32)) * _mm(yc_ref[...], wc_ref[0]))
    x = x_ref[...] + m[2:3] * _mm(mixed, wo_ref[0])
    xo_ref[...] = x
    h = _bf(_rms(x, n2_ref[0]) * (1.0 + m[4:5]) + m[3:4])
    h_ref[...] = h
    logits = jnp.dot(h, wr_ref[0], preferred_element_type=F32)
    lane = lax.broadcasted_iota(jnp.int32, (1, LANE), 1)
    logits = jnp.where(lane < N_EXPERTS, logits, -jnp.inf)
    e = jnp.exp(logits - jnp.max(logits, axis=1, keepdims=True))
    aff = e / jnp.sum(e, axis=1, keepdims=True)
    aff_ref[...] = aff
    afft_ref[...] = aff.T[:N_EXPERTS, :]


def _mix(x2d, ya, yb, yc, proj, mod, norm2_g, wa, wb, wc, wo, wr, layer, seq, per_request):
    m_rows = x2d.shape[0]
    tm = min(512, seq)
    gate_blk = lambda tile: (lambda i: (i, tile * LANE // D_MODEL))
    wspec = lambda w: pl.BlockSpec((1,) + w.shape[1:], lambda i: (layer, 0, 0))
    branch = pl.BlockSpec((tm, ya.shape[1]), lambda i: (i, 0))
    return pl.pallas_call(
        _mix_kernel,
        grid=(m_rows // tm,),
        in_specs=[
            pl.BlockSpec((tm, D_MODEL), lambda i: (i, 0)),
            branch, branch, branch,
            pl.BlockSpec((tm, D_MODEL), gate_blk(T_GA)),
            pl.BlockSpec((tm, D_MODEL), gate_blk(T_GB)),
            pl.BlockSpec((tm, D_MODEL), gate_blk(T_GC)),
            pl.BlockSpec((1, MOD_ROWS, D_MODEL), lambda i: ((i * tm) // seq if per_request else 0, 0, 0)),
            pl.BlockSpec((1, 1, D_MODEL), lambda i: (layer, 0, 0)),
            wspec(wa), wspec(wb), wspec(wc), wspec(wo), wspec(wr),
        ],
        out_specs=[
            pl.BlockSpec((tm, D_MODEL), lambda i: (i, 0)),
            pl.BlockSpec((tm, D_MODEL), lambda i: (i, 0)),
            pl.BlockSpec((tm, LANE), lambda i: (i, 0)),
            pl.BlockSpec((N_EXPERTS, tm), lambda i: (0, i)),
        ],
        out_shape=[
            jax.ShapeDtypeStruct((m_rows, D_MODEL), F32),
            jax.ShapeDtypeStruct((m_rows, D_MODEL), BF16),
            jax.ShapeDtypeStruct((m_rows, LANE), F32),
            jax.ShapeDtypeStruct((N_EXPERTS, m_rows), F32),
        ],
        compiler_params=_cparams("parallel"),
    )(x2d, ya, yb, yc, proj, proj, proj, mod, norm2_g.reshape(DEPTH, 1, D_MODEL), wa, wb, wc, wo, wr)


SCATTER_K = 512


def _gather_kernel(aff_ref, afft_ref, h_ref, xs_ref, gate_ref, rankc_ref, rank_scr, onehot_scr, col_scr, *,
                   seq, cap):
    n_blk = seq // LANE
    ri = lax.broadcasted_iota(jnp.int32, (LANE, LANE), 0)
    ci = lax.broadcasted_iota(jnp.int32, (LANE, LANE), 1)
    lane = lax.broadcasted_iota(jnp.int32, (1, LANE), 1)
    slot = lax.broadcasted_iota(jnp.int32, (cap, 1), 0).astype(F32)

    def expert_body(e, carry):
        a_row = afft_ref[pl.ds(e, 1), :]
        a_col = jnp.sum(jnp.where(lane == e, aff_ref[...], 0.0), axis=1, keepdims=True)
        col_scr[...] = jnp.broadcast_to(a_col, (seq, LANE))
        rank_parts = []
        for tb in range(n_blk):
            a_t = a_row[:, tb * LANE:(tb + 1) * LANE]
            count = jnp.zeros((LANE, LANE), F32)
            for sb in range(n_blk):
                a_s = col_scr[sb * LANE:(sb + 1) * LANE, :]
                if sb < tb:
                    beats = a_s >= a_t
                elif sb > tb:
                    beats = a_s > a_t
                else:
                    beats = (a_s > a_t) | ((a_s == a_t) & (ri < ci))
                count = count + jnp.where(beats, 1.0, 0.0)
            rank_parts.append(jnp.sum(count, axis=0, keepdims=True))
        rank = jnp.concatenate(rank_parts, axis=1)
        rank_scr[pl.ds(e, 1), :] = rank
        chosen = rank == slot
        onehot_scr[pl.ds(pl.multiple_of(e * cap, cap), cap), :] = jnp.where(chosen, 1.0, 0.0).astype(BF16)
        gate_ref[e, 0] = jnp.sum(jnp.where(chosen, a_row, 0.0), axis=1, keepdims=True)
        return carry

    lax.fori_loop(0, N_EXPERTS, expert_body, 0)
    ranks = jnp.concatenate([rank_scr[...], jnp.zeros((LANE - N_EXPERTS, seq), F32)], axis=0)
    rankc_ref[...] = ranks.T
    per_group = SCATTER_K // cap
    h = h_ref[...]
    for i in range(N_EXPERTS // per_group):
        rows = jnp.dot(onehot_scr[i * SCATTER_K:(i + 1) * SCATTER_K, :], h, preferred_element_type=F32)
        xs_ref[i * per_group:(i + 1) * per_group, 0] = _bf(rows).reshape(per_group, cap, D_MODEL)


def _gather(aff, afft, h2, batch, seq, cap):
    return pl.pallas_call(
        functools.partial(_gather_kernel, seq=seq, cap=cap),
        grid=(batch,),
        in_specs=[
            pl.BlockSpec((seq, LANE), lambda r: (r, 0)),
            pl.BlockSpec((N_EXPERTS, seq), lambda r: (0, r)),
            pl.BlockSpec((seq, D_MODEL), lambda r: (r, 0)),
        ],
        out_specs=[
            pl.BlockSpec((N_EXPERTS, 1, cap, D_MODEL), lambda r: (0, r, 0, 0)),
            pl.BlockSpec((N_EXPERTS, 1, cap, 1), lambda r: (0, r, 0, 0)),
            pl.BlockSpec((seq, LANE), lambda r: (r, 0)),
        ],
        out_shape=[
            jax.ShapeDtypeStruct((N_EXPERTS, batch, cap, D_MODEL), BF16),
            jax.ShapeDtypeStruct((N_EXPERTS, batch, cap, 1), F32),
            jax.ShapeDtypeStruct((batch * seq, LANE), F32),
        ],
        scratch_shapes=[pltpu.VMEM((N_EXPERTS, seq), F32), pltpu.VMEM((N_EXPERTS * cap, seq), BF16),
                        pltpu.VMEM((seq, LANE), F32)],
        compiler_params=_cparams("parallel"),
    )(aff, afft, h2)


def _ffn_kernel(*refs, n_groups):
    x_refs, gate_refs = refs[:n_groups], refs[n_groups:2 * n_groups]
    wg_ref, wu_ref, wd_ref = refs[2 * n_groups:2 * n_groups + 3]
    y_refs = refs[2 * n_groups + 3:]
    wg, wu, wd = _bf(wg_ref[0, 0]), _bf(wu_ref[0, 0]), _bf(wd_ref[0, 0])
    d = functools.partial(jnp.dot, preferred_element_type=F32)
    for x_ref, gate_ref, y_ref in zip(x_refs, gate_refs, y_refs):
        x = x_ref[0]
        hid = _silu(d(x, wg)) * d(x, wu)
        y_ref[0] = _bf(d(_bf(hid), wd) * gate_ref[0])


def _expert_ffn(xs_groups, gate_groups, w_gate, w_up, w_down, layer):
    n_groups = len(xs_groups)
    row_specs = [pl.BlockSpec((1, xs.shape[1], D_MODEL), lambda e: (e, 0, 0)) for xs in xs_groups]
    gate_specs = [pl.BlockSpec((1, g.shape[1], 1), lambda e: (e, 0, 0)) for g in gate_groups]
    return pl.pallas_call(
        functools.partial(_ffn_kernel, n_groups=n_groups),
        grid=(N_EXPERTS,),
        in_specs=row_specs + gate_specs + [
            pl.BlockSpec((1, 1, D_MODEL, D_EXPERT), lambda e: (layer, e, 0, 0)),
            pl.BlockSpec((1, 1, D_MODEL, D_EXPERT), lambda e: (layer, e, 0, 0)),
            pl.BlockSpec((1, 1, D_EXPERT, D_MODEL), lambda e: (layer, e, 0, 0)),
        ],
        out_specs=row_specs,
        out_shape=[jax.ShapeDtypeStruct(xs.shape, BF16) for xs in xs_groups],
        compiler_params=_cparams("parallel"),
    )(*xs_groups, *gate_groups, w_gate, w_up, w_down)


def _scatter_kernel(x_ref, rankc_ref, ye_ref, mod_ref, fg_ref, o_ref, acc, *, cap, final):
    g = pl.program_id(1)
    n_groups = pl.num_programs(1)
    per_lane_tile = LANE // cap
    rank_parts = _split3(rankc_ref[...])
    row = lax.broadcasted_iota(jnp.int32, (LANE, LANE), 0)
    lane = lax.broadcasted_iota(jnp.int32, (LANE, LANE), 1)
    lane_slot = (lax.broadcasted_iota(jnp.int32, (1, LANE), 1) % cap).astype(F32)
    d = functools.partial(jnp.dot, preferred_element_type=F32)
    tiles = []
    for b in range(SCATTER_K // LANE):
        first = (g * (SCATTER_K // LANE) + b) * per_lane_tile
        sel = jnp.where(row == first + lane // cap, 1.0, 0.0).astype(BF16)
        hi, mid, lo = rank_parts
        token_rank = (d(hi, sel) + d(mid, sel)) + d(lo, sel)
        tiles.append(jnp.where(token_rank == lane_slot, 1.0, 0.0).astype(BF16))
    onehot = jnp.concatenate(tiles, axis=1)
    spread = d(onehot, ye_ref[:, 0].reshape(SCATTER_K, D_MODEL))

    @pl.when(g == 0)
    def _():
        acc[...] = spread

    @pl.when(g > 0)
    def _():
        acc[...] += spread

    @pl.when(g == n_groups - 1)
    def _():
        x = x_ref[...] + mod_ref[0][5:6] * acc[...]
        o_ref[...] = _rms(x, fg_ref[...]) if final else x


def _scatter(x2d, rankc, ye, mod, final_g, batch, seq, cap, per_request, final):
    experts_per_group = SCATTER_K // cap
    slots = pl.BlockSpec((experts_per_group, 1, cap, D_MODEL), lambda r, g: (g, r, 0, 0))
    return pl.pallas_call(
        functools.partial(_scatter_kernel, cap=cap, final=final),
        grid=(batch, N_EXPERTS // experts_per_group),
        in_specs=[
            pl.BlockSpec((seq, D_MODEL), lambda r, g: (r, 0)),
            pl.BlockSpec((seq, LANE), lambda r, g: (r, 0)),
            slots,
            pl.BlockSpec((1, MOD_ROWS, D_MODEL), lambda r, g: (r if per_request else 0, 0, 0)),
            pl.BlockSpec((1, D_MODEL), lambda r, g: (0, 0)),
        ],
        out_specs=pl.BlockSpec((seq, D_MODEL), lambda r, g: (r, 0)),
        out_shape=jax.ShapeDtypeStruct((batch * seq, D_MODEL), F32),
        scratch_shapes=[pltpu.VMEM((seq, D_MODEL), F32)],
        compiler_params=_cparams("parallel", "arbitrary"),
    )(x2d, rankc, ye, mod, final_g.reshape(1, D_MODEL))


def _reorder_w_in(w_in):
    a_main = w_in[:, :, 0:2048]
    a_gate = w_in[:, :, 2048:2064]
    b_main = w_in[:, :, 2064:3600]
    b_gate = w_in[:, :, 3600:3616]
    c_main = w_in[:, :, 3616:4384]
    merge = w_in[:, :, 4384:7456]

    def per_head(g, heads):
        g = g.reshape(DEPTH, D_MODEL, 4, heads).transpose(0, 1, 3, 2)
        g = jnp.pad(g, ((0, 0), (0, 0), (0, 0), (0, GATE_STRIDE - 4)))
        return g.reshape(DEPTH, D_MODEL, heads * GATE_STRIDE)

    width_q = HEADS_C * HEAD_DIM_C
    qc = c_main[:, :, :width_q].reshape(DEPTH, D_MODEL, HEADS_C, HEAD_DIM_C)
    qc = jnp.stack([qc[:, :, h] for h in HEAD_ORDER_C], axis=2).reshape(DEPTH, D_MODEL, width_q)
    c_main = jnp.concatenate([qc, c_main[:, :, width_q:]], axis=-1)
    gates = jnp.concatenate([per_head(a_gate, HEADS_A), per_head(b_gate, HEADS_B)], axis=-1)
    pad = jnp.zeros((DEPTH, D_MODEL, N_PROJ - T_GATES * LANE - gates.shape[-1]), w_in.dtype)
    gates_lo = _bf(gates - _bf(gates).astype(F32))
    gates_lo = jnp.pad(gates_lo, ((0, 0), (0, 0), (0, LANE - gates.shape[-1])))
    return _bf(jnp.concatenate([merge, a_main, b_main, c_main, gates, pad], axis=-1)), gates_lo


def _gate_lane_rows(delta_kinds, mlstm_kinds):
    def block(kinds, heads):
        zero = jnp.zeros((DEPTH, heads), F32)
        cols = [zero if k is None else k.astype(F32) for k in kinds] + [zero] * (GATE_STRIDE - len(kinds))
        return jnp.stack(cols, axis=-1).reshape(DEPTH, heads * GATE_STRIDE)

    used = jnp.concatenate([block(delta_kinds, HEADS_A), block(mlstm_kinds, HEADS_B)], axis=-1)
    return jnp.pad(used, ((0, 0), (0, LANE - used.shape[-1])))[:, None, :]


def kernel(x_prompt, x_sample, cache_attn_k, cache_attn_v, state_delta, state_mlstm_c, state_mlstm_n, state_mlstm_m, c, c_ctx, ada_w, ada_b, norm1_g, norm2_g, w_in, conv_qkv_a, delta_a_log, delta_dt_bias, delta_norm_g, mlstm_i_bias, mlstm_f_bias, mlstm_norm_g, attn_sink, w_branch_a, w_branch_b, w_branch_c, w_out, w_router, w_expert_gate, w_expert_up, w_expert_down, final_norm_g):
    batch_p, seq_p, _ = x_prompt.shape
    batch_s, seq_s, _ = x_sample.shape
    past = cache_attn_k.shape[2]

    cond = jnp.concatenate([c_ctx[None, :], c, jnp.zeros((COND_ROWS - 1 - batch_s, D_MODEL), F32)], axis=0)
    mod = _modulation(cond, ada_w, ada_b).reshape(DEPTH, COND_ROWS, ADA_CHUNKS, D_MODEL)
    mod = jnp.pad(mod, ((0, 0), (0, 0), (0, MOD_ROWS - ADA_CHUNKS), (0, 0)))

    w_in_r, w_gates_lo = _reorder_w_in(w_in)
    wa, wb, wo = _bf(w_branch_a), _bf(w_branch_b), _bf(w_out)
    wc = w_branch_c.reshape(DEPTH, HEADS_C, HEAD_DIM_C, D_MODEL)
    wc = _bf(jnp.stack([wc[:, h] for h in HEAD_ORDER_C], axis=1).reshape(DEPTH, HEADS_C * HEAD_DIM_C, D_MODEL))
    wr = _bf(jnp.pad(w_router, ((0, 0), (0, 0), (0, LANE - N_EXPERTS))))
    gate_bias = _gate_lane_rows([None, None, delta_dt_bias[:, 0], delta_dt_bias[:, 1]],
                                [mlstm_i_bias[:, 0], mlstm_i_bias[:, 1], mlstm_f_bias[:, 0], mlstm_f_bias[:, 1]])
    gate_a_log = _gate_lane_rows([None, None, delta_a_log[:, 0], delta_a_log[:, 1]], [])
    sink = jnp.broadcast_to(attn_sink[:, :, None], (DEPTH, HEADS_C, LANE)).astype(F32)
    cache_k = cache_attn_k.reshape(batch_s, DEPTH, past, KV_HEADS_C * HEAD_DIM_C)
    cache_v = cache_attn_v.reshape(batch_s, DEPTH, past, KV_HEADS_C * HEAD_DIM_C)
    state_n = state_mlstm_n.reshape(batch_s, DEPTH, 2, HEADS_B, 1, DK_B)
    state_m = jnp.broadcast_to(state_mlstm_m[..., None, None], (batch_s, DEPTH, 2, HEADS_B, 1, LANE)).astype(F32)
    cos, sin = _rope_tables(seq_s)

    def mix_and_route(x2d, l, batch, seq, latent):
        mod_l = mod[l, 1:1 + batch] if latent else mod[l, 0:1]
        cap = EC_CAPACITY * seq // N_EXPERTS
        proj, scan_gates = _in_proj(x2d, mod_l, norm1_g, w_in_r, w_gates_lo, gate_bias, gate_a_log, l, seq, latent)
        ya, d_new = _delta_mixer(proj, scan_gates, conv_qkv_a, delta_norm_g, l, batch, seq,
                                 state_delta if latent else None)
        yb, b_new = _mlstm_mixer(proj, scan_gates, mlstm_norm_g, l, batch, seq,
                                 (state_mlstm_c, state_n, state_m) if latent else None)
        if latent:
            yc = _latent_attention(proj, cache_k, cache_v, cos, sin, sink, l, batch, seq)
        else:
            yc = _ctx_attention(proj, sink, l, batch, seq)
        x1, h2, aff, afft = _mix(x2d, ya, yb, yc, proj, mod_l, norm2_g, wa, wb, wc, wo, wr, l, seq, latent)
        xs, gate, rankc = _gather(aff, afft, h2, batch, seq, cap)
        return dict(x1=x1, rankc=rankc, mod=mod_l, cap=cap, proj=proj, delta=d_new, mlstm=b_new,
                    xs=xs.reshape(N_EXPERTS, batch * cap, D_MODEL), gate=gate.reshape(N_EXPERTS, batch * cap, 1))

    def add_experts(routed, ye, l, batch, seq, latent):
        return _scatter(routed["x1"], routed["rankc"], ye.reshape(N_EXPERTS, batch, routed["cap"], D_MODEL),
                        routed["mod"], final_norm_g, batch, seq, routed["cap"], latent, l == DEPTH - 1)

    xp = x_prompt.reshape(batch_p * seq_p, D_MODEL)
    xs = x_sample.reshape(batch_s * seq_s, D_MODEL)
    ks, vs, ds, cs, ns, ms = [], [], [], [], [], []
    for l in range(DEPTH):
        rp = mix_and_route(xp, l, batch_p, seq_p, False)
        rs = mix_and_route(xs, l, batch_s, seq_s, True)
        ye_p, ye_s = _expert_ffn([rp["xs"], rs["xs"]], [rp["gate"], rs["gate"]],
                                 w_expert_gate, w_expert_up, w_expert_down, l)
        xp = add_experts(rp, ye_p, l, batch_p, seq_p, False)
        xs = add_experts(rs, ye_s, l, batch_s, seq_s, True)
        kv = rp["proj"][:, T_KC * LANE:(T_VC + 1) * LANE].astype(F32)
        ks.append(kv[:, :LANE].reshape(batch_p, seq_p, KV_HEADS_C, HEAD_DIM_C))
        vs.append(kv[:, LANE:].reshape(batch_p, seq_p, KV_HEADS_C, HEAD_DIM_C))
        ds.append(rp["delta"])
        c_new, n_new, m_new = rp["mlstm"]
        cs.append(c_new)
        ns.append(n_new)
        ms.append(m_new)

    stack = lambda parts: jnp.stack(parts, axis=1)
    return (xp.reshape(batch_p, seq_p, D_MODEL), xs.reshape(batch_s, seq_s, D_MODEL),
            stack(ks), stack(vs), stack(ds), stack(cs), stack(ns), stack(ms))
```
